```python
import math
import jax, jax.numpy as jnp
from jax import lax
import numpy as np

D_MODEL = 2048
BATCH = 2
SEQ = 8192
DEPTH = 2

GRID_W = 64
CTX_LEN = 256

RET_HEADS = 4
RET_DK = 256
RET_DV = 256
RET_W = RET_HEADS * RET_DV
RET_CHUNK = 128
ATT_HEADS = 8
ATT_KV_HEADS = 2
ATT_HD = 128
ATT_W = ATT_HEADS * ATT_HD
ATT_KV_W = ATT_KV_HEADS * ATT_HD
Q_BLOCK = 128
ROPE_THETA = 10000.0
S5_GROUP = 16
S5_W = 768
S5_GROUPS = S5_W // S5_GROUP
S5_STATE = 64

N_BRANCH = 3
DEEPNORM_ALPHA = (2.0 * DEPTH) ** 0.25
DEEPNORM_BETA = (8.0 * DEPTH) ** -0.25
LN_EPS = 1e-6
RMS_EPS = 1e-6

IN_LAYOUT = (
    ("ret_q", RET_HEADS * RET_DK), ("ret_k", RET_HEADS * RET_DK), ("ret_v", RET_W), ("ret_g", RET_W),
    ("att_q", ATT_W), ("att_k", ATT_KV_W), ("att_v", ATT_KV_W), ("att_g", ATT_W),
    ("s5_u", S5_W), ("s5_g", S5_W),
    ("merge", N_BRANCH * D_MODEL),
)
IN_WIDTH = 4 * RET_W + 2 * ATT_W + 2 * ATT_KV_W + 2 * S5_W + N_BRANCH * D_MODEL

kernel_name = "hybrid_retention_gqa_s5_prefix_dit"


def _in_proj(u, w_in, name):
    start = 0
    for seg_name, width in IN_LAYOUT:
        if seg_name == name:
            return u @ w_in[:, start:start + width]
        start += width
    raise KeyError(name)


def _layernorm(x, eps):
    xf = x.astype(jnp.float32)
    mu = jnp.mean(xf, axis=-1, keepdims=True)
    var = jnp.mean(jnp.square(xf - mu), axis=-1, keepdims=True)
    return ((xf - mu) * lax.rsqrt(var + eps)).astype(x.dtype)


def _rmsnorm(x, w):
    xf = x.astype(jnp.float32)
    y = xf * lax.rsqrt(jnp.mean(jnp.square(xf), axis=-1, keepdims=True) + RMS_EPS)
    return y.astype(x.dtype) * w


def _axial_rope(rows, head_dim):
    row = jnp.repeat(jnp.arange(rows, dtype=jnp.float32), GRID_W)
    col = jnp.tile(jnp.arange(GRID_W, dtype=jnp.float32), rows)
    per_axis = head_dim // 4
    inv = ROPE_THETA ** (-jnp.arange(per_axis, dtype=jnp.float32) / per_axis)
    ang = jnp.concatenate([row[:, None] * inv, col[:, None] * inv], axis=-1)
    return jnp.cos(ang), jnp.sin(ang)


def _rope(x, cos, sin):
    half = x.shape[-1] // 2
    x1, x2 = x[..., :half], x[..., half:]
    cs, sn = cos[:, None, :], sin[:, None, :]
    return jnp.concatenate([x1 * cs - x2 * sn, x2 * cs + x1 * sn], axis=-1)


def _to_chunks(t):
    bsz, nh, n, d = t.shape
    return t.reshape(bsz, nh, n // RET_CHUNK, RET_CHUNK, d).transpose(2, 0, 1, 3, 4)


def _retention_dir(q, k, v, log_g, s0, include_diag):
    bsz, nh, n, _ = q.shape
    dv = v.shape[-1]
    log_g = log_g.astype(jnp.float32)
    idx = jnp.arange(RET_CHUNK, dtype=jnp.float32)
    rel = idx[:, None] - idx[None, :]
    mask = (rel >= 0) if include_diag else (rel > 0)
    intra = jnp.where(mask, jnp.exp(log_g[:, None, None] * jnp.where(mask, rel, 0.0)), 0.0)
    q_dec = jnp.exp(log_g[:, None] * (idx + 1.0))[..., None]
    k_dec = jnp.exp(log_g[:, None] * (RET_CHUNK - 1.0 - idx))[..., None]
    chunk_dec = jnp.exp(log_g * RET_CHUNK)[:, None, None]

    def step(state, blk):
        qi, ki, vi = blk
        scores = jnp.einsum('bhqd,bhkd->bhqk', qi, ki) * intra
        out = (jnp.einsum('bhqk,bhkv->bhqv', scores, vi)
               + jnp.einsum('bhqd,bhdv->bhqv', qi * q_dec, state))
        state = state * chunk_dec + jnp.einsum('bhkd,bhkv->bhdv', ki * k_dec, vi)
        return state, out

    s_final, out = lax.scan(step, s0, (_to_chunks(q), _to_chunks(k), _to_chunks(v)))
    return out.transpose(1, 2, 0, 3, 4).reshape(bsz, nh, n, dv), s_final


def _bidir_retention(q_l, k_l, v_l, q_c, k_c, v_c, log_decay):
    bsz = q_l.shape[0]
    zero = jnp.zeros((bsz, RET_HEADS, RET_DK, RET_DV), jnp.float32)
    flip = lambda t: t[:, :, ::-1]
    o_cf, s_f = _retention_dir(q_c, k_c, v_c, log_decay[0], zero, True)
    o_lf, _ = _retention_dir(q_l, k_l, v_l, log_decay[0], s_f, True)
    o_cb, s_b = _retention_dir(flip(q_c), flip(k_c), flip(v_c), log_decay[1], zero, False)
    o_lb, _ = _retention_dir(flip(q_l), flip(k_l), flip(v_l), log_decay[1], s_b, False)
    return o_lf + flip(o_lb), o_cf + flip(o_cb)


def _attn_blocks(q, k, v):
    bsz, n = q.shape[:2]
    nb = n // Q_BLOCK
    grp = ATT_HEADS // ATT_KV_HEADS
    qb = q.reshape(bsz, nb, Q_BLOCK, ATT_KV_HEADS, grp, ATT_HD).swapaxes(0, 1)
    scale = ATT_HD ** -0.5

    def block(qi):
        s = jnp.einsum('bqhgd,bkhd->bhgqk', qi, k).astype(jnp.float32) * scale
        p = jax.nn.softmax(s, axis=-1)
        return jnp.einsum('bhgqk,bkhd->bqhgd', p.astype(v.dtype), v)

    o = lax.map(block, qb)
    return o.swapaxes(0, 1).reshape(bsz, n, ATT_W)


def _s5_dir(u, a_re, a_im, log_dt, b_re, b_im, c_re, c_im, s0_re, s0_im):
    f32 = jnp.float32
    a_re, a_im = a_re.astype(f32), a_im.astype(f32)
    dt = jnp.exp(log_dt.astype(f32))[:, None]
    mag = jnp.exp(a_re * dt)
    abr, abi = mag * jnp.cos(a_im * dt), mag * jnp.sin(a_im * dt)
    den = a_re * a_re + a_im * a_im
    fr = ((abr - 1.0) * a_re + abi * a_im) / den
    fi = (abi * a_re - (abr - 1.0) * a_im) / den
    b_re, b_im = b_re.astype(f32), b_im.astype(f32)
    bbr = fr[..., None] * b_re - fi[..., None] * b_im
    bbi = fr[..., None] * b_im + fi[..., None] * b_re
    bur = jnp.einsum('bngc,gpc->bngp', u, bbr)
    bui = jnp.einsum('bngc,gpc->bngp', u, bbi)
    bur = bur.at[:, 0].add(abr * s0_re - abi * s0_im)
    bui = bui.at[:, 0].add(abr * s0_im + abi * s0_re)
    n = u.shape[1]
    ar_seq = jnp.broadcast_to(abr, (1, n) + abr.shape)
    ai_seq = jnp.broadcast_to(abi, (1, n) + abi.shape)

    def combine(e1, e2):
        a1r, a1i, b1r, b1i = e1
        a2r, a2i, b2r, b2i = e2
        return (a2r * a1r - a2i * a1i, a2r * a1i + a2i * a1r,
                a2r * b1r - a2i * b1i + b2r, a2r * b1i + a2i * b1r + b2i)

    _, _, xr, xi = lax.associative_scan(combine, (ar_seq, ai_seq, bur, bui), axis=1)
    y = (jnp.einsum('bngp,gcp->bngc', xr, c_re.astype(f32))
         - jnp.einsum('bngp,gcp->bngc', xi, c_im.astype(f32)))
    return y, xr[:, -1], xi[:, -1]


def _bidir_s5(u_l, u_c, a_re, a_im, log_dt, b_re, b_im, c_re, c_im, d_skip):
    bsz, n, _ = u_l.shape
    n_ctx = u_c.shape[1]
    ul = u_l.reshape(bsz, n, S5_GROUPS, S5_GROUP)
    uc = u_c.reshape(bsz, n_ctx, S5_GROUPS, S5_GROUP)
    zero = jnp.zeros((bsz, S5_GROUPS, S5_STATE), jnp.float32)
    d = d_skip.reshape(S5_GROUPS, S5_GROUP)
    y_l = d * ul
    y_c = d * uc
    for direction in range(2):
        rev = (lambda t: t[:, ::-1]) if direction == 1 else (lambda t: t)
        pars = (a_re[direction], a_im[direction], log_dt[direction], b_re[direction], b_im[direction],
                c_re[direction], c_im[direction])
        yc, sr, si = _s5_dir(rev(uc), *pars, zero, zero)
        yl, _, _ = _s5_dir(rev(ul), *pars, sr, si)
        y_l = y_l + rev(yl)
        y_c = y_c + rev(yc)
    return y_l.reshape(bsz, n, S5_W), y_c.reshape(bsz, n_ctx, S5_W)


def _finish(u, r, a, s_pre, w_in, ret_gn_w, s5_glu_w, s5_glu_b, w_br_ret, w_br_att, w_br_s5, w_out):
    bsz, n, _ = u.shape
    r = _layernorm(r.transpose(0, 2, 1, 3), LN_EPS).reshape(bsz, n, RET_W) * ret_gn_w
    r = r * jax.nn.silu(_in_proj(u, w_in, 'ret_g'))
    a = a * jax.nn.silu(_in_proj(u, w_in, 'att_g'))
    s = jax.nn.gelu(s_pre)
    s = s * jax.nn.sigmoid(s @ s5_glu_w + s5_glu_b)
    s = s * jax.nn.silu(_in_proj(u, w_in, 's5_g'))
    gates = jax.nn.sigmoid(_in_proj(u, w_in, 'merge')).reshape(bsz, n, N_BRANCH, D_MODEL)
    m = (gates[..., 0, :] * (r @ w_br_ret) + gates[..., 1, :] * (a @ w_br_att)
         + gates[..., 2, :] * (s @ w_br_s5))
    return m @ w_out


def _layer(x, h, c, c_ctx, ada_w, ada_b, w_in, ret_log_decay, ret_gn_w, att_q_norm, att_k_norm,
           s5_a_re, s5_a_im, s5_log_dt, s5_b_re, s5_b_im, s5_c_re, s5_c_im, s5_d, s5_glu_w, s5_glu_b,
           w_br_ret, w_br_att, w_br_s5, w_out, ln_w, ln_b, cos_r, sin_r, cos_a, sin_a, need_ctx):
    bsz, n, _ = x.shape
    n_ctx = h.shape[1]
    shift_l, scale_l, gate_l = jnp.split(jax.nn.silu(c) @ ada_w + ada_b, 3, axis=-1)
    shift_c, scale_c, gate_c = jnp.split(jax.nn.silu(c_ctx) @ ada_w + ada_b, 3, axis=-1)
    u_l = x * (1.0 + scale_l[:, None]) + shift_l[:, None]
    u_c = h * (1.0 + scale_c) + shift_c

    def ret_qkv(u, m, rotary):
        q = _in_proj(u, w_in, 'ret_q').reshape(bsz, m, RET_HEADS, RET_DK)
        k = _in_proj(u, w_in, 'ret_k').reshape(bsz, m, RET_HEADS, RET_DK) * (RET_DK ** -0.5)
        v = _in_proj(u, w_in, 'ret_v').reshape(bsz, m, RET_HEADS, RET_DV)
        if rotary:
            q, k = _rope(q, cos_r, sin_r), _rope(k, cos_r, sin_r)
        return q.transpose(0, 2, 1, 3), k.transpose(0, 2, 1, 3), v.transpose(0, 2, 1, 3)

    r_l, r_c = _bidir_retention(*ret_qkv(u_l, n, True), *ret_qkv(u_c, n_ctx, False), ret_log_decay)

    def att_kv(u, m, rotary):
        k = _rmsnorm(_in_proj(u, w_in, 'att_k').reshape(bsz, m, ATT_KV_HEADS, ATT_HD), att_k_norm)
        v = _in_proj(u, w_in, 'att_v').reshape(bsz, m, ATT_KV_HEADS, ATT_HD)
        return (_rope(k, cos_a, sin_a) if rotary else k), v

    def att_q(u, m, rotary):
        q = _rmsnorm(_in_proj(u, w_in, 'att_q').reshape(bsz, m, ATT_HEADS, ATT_HD), att_q_norm)
        return _rope(q, cos_a, sin_a) if rotary else q

    k_l, v_l = att_kv(u_l, n, True)
    k_c, v_c = att_kv(u_c, n_ctx, False)
    a_l = _attn_blocks(att_q(u_l, n, True), jnp.concatenate([k_c, k_l], axis=1),
                       jnp.concatenate([v_c, v_l], axis=1))

    s_l, s_c = _bidir_s5(_in_proj(u_l, w_in, 's5_u'), _in_proj(u_c, w_in, 's5_u'), s5_a_re, s5_a_im,
                         s5_log_dt, s5_b_re, s5_b_im, s5_c_re, s5_c_im, s5_d)

    y_l = _finish(u_l, r_l, a_l, s_l, w_in, ret_gn_w, s5_glu_w, s5_glu_b, w_br_ret, w_br_att, w_br_s5, w_out)
    x_new = _layernorm(DEEPNORM_ALPHA * x + gate_l[:, None] * y_l, LN_EPS) * ln_w + ln_b
    if not need_ctx:
        return x_new, h
    a_c = _attn_blocks(att_q(u_c, n_ctx, False), k_c, v_c)
    y_c = _finish(u_c, r_c, a_c, s_c, w_in, ret_gn_w, s5_glu_w, s5_glu_b, w_br_ret, w_br_att, w_br_s5, w_out)
    h_new = _layernorm(DEEPNORM_ALPHA * h + gate_c * y_c, LN_EPS) * ln_w + ln_b
    return x_new, h_new


def setup_inputs(seed: int = 0) -> dict:
    key = jax.random.key(seed)
    ks = jax.random.split(key, 32)
    f32 = jnp.float32
    nrm = lambda k, shape, scale: jax.random.normal(k, shape, f32) * scale
    L, D, G, P = DEPTH, D_MODEL, S5_GROUPS, S5_STATE
    base_decay = jnp.log1p(-jnp.exp2(-5.0 - jnp.arange(RET_HEADS, dtype=f32)))
    n_idx = jnp.arange(P, dtype=f32)
    return {
        "x": nrm(ks[0], (BATCH, SEQ, D), 1.0),
        "c": nrm(ks[1], (BATCH, D), 1.0),
        "ctx": nrm(ks[2], (BATCH, CTX_LEN, D), 1.0),
        "c_ctx": nrm(ks[3], (D,), 1.0),
        "ada_w": nrm(ks[4], (L, D, 3 * D), 0.5 * D ** -0.5),
        "ada_b": nrm(ks[5], (L, 3 * D), 0.01),
        "w_in": nrm(ks[6], (L, D, IN_WIDTH), D ** -0.5),
        "ret_log_decay": base_decay * (1.0 + nrm(ks[7], (L, 2, RET_HEADS), 0.05)),
        "ret_gn_w": 1.0 + nrm(ks[8], (L, RET_W), 0.02),
        "att_q_norm": 1.0 + nrm(ks[9], (L, ATT_HD), 0.02),
        "att_k_norm": 1.0 + nrm(ks[10], (L, ATT_HD), 0.02),
        "s5_a_re": -0.5 + nrm(ks[11], (L, 2, G, P), 0.01),
        "s5_a_im": jnp.pi * n_idx + nrm(ks[12], (L, 2, G, P), 0.01),
        "s5_log_dt": jax.random.uniform(ks[13], (L, 2, G), f32, math.log(1e-3), math.log(1e-1)),
        "s5_b_re": nrm(ks[14], (L, 2, G, P, S5_GROUP), (2.0 * S5_GROUP) ** -0.5),
        "s5_b_im": nrm(ks[15], (L, 2, G, P, S5_GROUP), (2.0 * S5_GROUP) ** -0.5),
        "s5_c_re": nrm(ks[16], (L, 2, G, S5_GROUP, P), (2.0 * P) ** -0.5),
        "s5_c_im": nrm(ks[17], (L, 2, G, S5_GROUP, P), (2.0 * P) ** -0.5),
        "s5_d": nrm(ks[18], (L, S5_W), 1.0),
        "s5_glu_w": nrm(ks[19], (L, S5_W, S5_W), S5_W ** -0.5),
        "s5_glu_b": nrm(ks[20], (L, S5_W), 0.01),
        "w_br_ret": nrm(ks[21], (L, RET_W, D), DEEPNORM_BETA * RET_W ** -0.5),
        "w_br_att": nrm(ks[22], (L, ATT_W, D), DEEPNORM_BETA * ATT_W ** -0.5),
        "w_br_s5": nrm(ks[23], (L, S5_W, D), DEEPNORM_BETA * S5_W ** -0.5),
        "w_out": nrm(ks[24], (L, D, D), DEEPNORM_BETA * D ** -0.5),
        "ln_w": 1.0 + nrm(ks[25], (L, D), 0.02),
        "ln_b": nrm(ks[26], (L, D), 0.01),
    }


def reference(x, c, ctx, c_ctx, ada_w, ada_b, w_in, ret_log_decay, ret_gn_w, att_q_norm, att_k_norm,
              s5_a_re, s5_a_im, s5_log_dt, s5_b_re, s5_b_im, s5_c_re, s5_c_im, s5_d, s5_glu_w, s5_glu_b,
              w_br_ret, w_br_att, w_br_s5, w_out, ln_w, ln_b):
    rows = x.shape[1] // GRID_W
    cos_r, sin_r = _axial_rope(rows, RET_DK)
    cos_a, sin_a = _axial_rope(rows, ATT_HD)
    h = ctx
    for l in range(DEPTH):
        x, h = _layer(x, h, c, c_ctx, ada_w[l], ada_b[l], w_in[l], ret_log_decay[l], ret_gn_w[l],
                      att_q_norm[l], att_k_norm[l], s5_a_re[l], s5_a_im[l], s5_log_dt[l], s5_b_re[l],
                      s5_b_im[l], s5_c_re[l], s5_c_im[l], s5_d[l], s5_glu_w[l], s5_glu_b[l], w_br_ret[l],
                      w_br_att[l], w_br_s5[l], w_out[l], ln_w[l], ln_b[l], cos_r, sin_r, cos_a, sin_a,
                      l < DEPTH - 1)
    return x
```

```python
import functools
import math

import jax
import jax.numpy as jnp
from jax import lax
from jax.experimental import pallas as pl
from jax.experimental.pallas import tpu as pltpu

F32 = jnp.float32
BF16 = jnp.bfloat16

GRID_W = 64
RET_HEADS = 4
RET_DK = 256
RET_DV = 256
RET_W = RET_HEADS * RET_DV
ATT_HEADS = 8
ATT_KV_HEADS = 2
ATT_GROUP = ATT_HEADS // ATT_KV_HEADS
ATT_HD = 128
ATT_W = ATT_HEADS * ATT_HD
ATT_KV_W = ATT_KV_HEADS * ATT_HD
ROPE_THETA = 10000.0
S5_GROUP = 16
S5_W = 768
S5_GROUPS = S5_W // S5_GROUP
S5_STATE = 64
N_BRANCH = 3
LN_EPS = 1e-6
RMS_EPS = 1e-6

LANES = 128
VMEM_LIMIT_BYTES = 56 * 1024 * 1024

CHUNK = 128
S5_CHUNK = 16
ROW_TILE = 256
COL_TILE = 512
MIX_W = 5632
GATE_W = 3072


def _params(*sem):
    return pltpu.CompilerParams(dimension_semantics=sem, vmem_limit_bytes=VMEM_LIMIT_BYTES)


def _big_row_tile(t):
    for cand in (1056, 1024, 768, 640, 512, 256):
        if t % cand == 0:
            return cand
    raise ValueError(f"unsupported token count {t}")


def _mod_kernel(c_ref, w_ref, b_ref, o_ref):
    c = c_ref[...]
    s = c * jax.nn.sigmoid(c)
    s_hi = s.astype(BF16)
    s_lo = (s - s_hi.astype(F32)).astype(BF16)
    w = w_ref[...]
    w_hi = w.astype(BF16)
    w_lo = (w - w_hi.astype(F32)).astype(BF16)
    acc = jnp.dot(s_hi, w_hi, preferred_element_type=F32)
    acc += jnp.dot(s_lo, w_hi, preferred_element_type=F32)
    acc += jnp.dot(s_hi, w_lo, preferred_element_type=F32)
    o_ref[...] = acc + b_ref[...]


def _modulation(cvec, ada_w, ada_b):
    rows, d = cvec.shape
    n = ada_w.shape[1]
    tn = 768
    return pl.pallas_call(
        _mod_kernel,
        out_shape=jax.ShapeDtypeStruct((rows, n), F32),
        grid=(n // tn,),
        in_specs=[pl.BlockSpec((rows, d), lambda j: (0, 0)),
                  pl.BlockSpec((d, tn), lambda j: (0, j)),
                  pl.BlockSpec((1, tn), lambda j: (0, j))],
        out_specs=pl.BlockSpec((rows, tn), lambda j: (0, j)),
        compiler_params=_params("arbitrary"),
        name="adaln_modulation",
    )(cvec, ada_w, ada_b.reshape(1, n))


def _modulate_kernel(x_ref, mod_ref, o_ref):
    m = mod_ref[...]
    o_ref[...] = (x_ref[...] * (1.0 + m[1:2]) + m[0:1]).astype(BF16)


def _modulate(xa, modsel):
    b, t, d = xa.shape
    return pl.pallas_call(
        _modulate_kernel,
        out_shape=jax.ShapeDtypeStruct((b, t, d), BF16),
        grid=(b, t // ROW_TILE),
        in_specs=[pl.BlockSpec((None, ROW_TILE, d), lambda bi, i: (bi, i, 0)),
                  pl.BlockSpec((None, None, 3, d), lambda bi, i: (bi, jnp.minimum(i, 1), 0, 0))],
        out_specs=pl.BlockSpec((None, ROW_TILE, d), lambda bi, i: (bi, i, 0)),
        compiler_params=_params("parallel", "arbitrary"),
        name="modulate",
    )(xa, modsel)


def _mix_kernel(u_ref, w_ref, cr_ref, sr_ref, ca_ref, sa_ref, qn_ref, kn_ref, o_ref):
    j = pl.program_id(2)
    acc = jnp.dot(u_ref[...], w_ref[...], preferred_element_type=F32)

    def rms_rope(x, w, scale):
        ms = jnp.mean(x * x, axis=-1, keepdims=True)
        y = x * lax.rsqrt(ms + RMS_EPS) * w
        y = y * ca_ref[...] + pltpu.roll(y, ATT_HD // 2, 1) * sa_ref[...]
        return (y * scale).astype(BF16)

    @pl.when(j < 4)
    def _():
        scale = jnp.where(j >= 2, RET_DK ** -0.5, 1.0).astype(F32)
        cos = cr_ref[...]
        sin = sr_ref[...]
        for h in range(COL_TILE // RET_DK):
            x1 = acc[:, h * RET_DK: h * RET_DK + LANES]
            x2 = acc[:, h * RET_DK + LANES: (h + 1) * RET_DK]
            o_ref[:, h * RET_DK: h * RET_DK + LANES] = ((x1 * cos - x2 * sin) * scale).astype(BF16)
            o_ref[:, h * RET_DK + LANES: (h + 1) * RET_DK] = ((x2 * cos + x1 * sin) * scale).astype(BF16)

    @pl.when((j == 4) | (j == 5) | (j >= 9))
    def _():
        o_ref[...] = acc.astype(BF16)

    @pl.when((j == 6) | (j == 7))
    def _():
        for h in range(COL_TILE // ATT_HD):
            o_ref[:, h * ATT_HD:(h + 1) * ATT_HD] = rms_rope(
                acc[:, h * ATT_HD:(h + 1) * ATT_HD], qn_ref[...], ATT_HD ** -0.5)

    @pl.when(j == 8)
    def _():
        for h in range(ATT_KV_HEADS):
            o_ref[:, h * ATT_HD:(h + 1) * ATT_HD] = rms_rope(
                acc[:, h * ATT_HD:(h + 1) * ATT_HD], kn_ref[...], 1.0)
        o_ref[:, ATT_KV_W:] = acc[:, ATT_KV_W:].astype(BF16)


def _mix_proj(u, w_mix, cos_r, sin_r, cos_a, sin_a, qn, kn):
    b, t, d = u.shape
    tm = _big_row_tile(t)
    tab = lambda: pl.BlockSpec((tm, LANES), lambda bi, i, j: (i, 0))
    vec = lambda: pl.BlockSpec((1, LANES), lambda bi, i, j: (0, 0))
    return pl.pallas_call(
        _mix_kernel,
        out_shape=jax.ShapeDtypeStruct((b, t, MIX_W), BF16),
        grid=(b, t // tm, MIX_W // COL_TILE),
        in_specs=[pl.BlockSpec((None, tm, d), lambda bi, i, j: (bi, i, 0)),
                  pl.BlockSpec((d, COL_TILE), lambda bi, i, j: (0, j)),
                  tab(), tab(), tab(), tab(), vec(), vec()],
        out_specs=pl.BlockSpec((None, tm, COL_TILE), lambda bi, i, j: (bi, i, j)),
        compiler_params=_params("parallel", "parallel", "arbitrary"),
        name="mixer_in_proj",
    )(u, w_mix, cos_r, sin_r, cos_a, sin_a, qn, kn)


def _ret_kernel(ld_ref, q_ref, k_ref, v_ref, o_ref, ob_ref, s_ref, *, n_ctx_chunks, n_chunks):
    h = pl.program_id(1)
    lgf = ld_ref[0, h]
    lgb = ld_ref[1, h]
    c = CHUNK
    ri = lax.broadcasted_iota(jnp.int32, (c, c), 0).astype(F32)
    ci = lax.broadcasted_iota(jnp.int32, (c, c), 1).astype(F32)
    rel = ri - ci
    mask = jnp.where(rel >= 0, jnp.exp(lgf * jnp.maximum(rel, 0.0)), jnp.exp(lgb * jnp.maximum(-rel, 0.0)))
    row = lax.broadcasted_iota(jnp.int32, (c, RET_DV), 0).astype(F32)
    qdec_f = jnp.exp(lgf * (row + 1.0))
    qdec_b = jnp.exp(lgb * (c - row))
    kdec_f = jnp.exp(lgf * (c - 1.0 - row))
    kdec_b = jnp.exp(lgb * row)
    cdec_f = jnp.exp(jnp.full((1, RET_DV), lgf * c, F32))
    cdec_b = jnp.exp(jnp.full((1, RET_DV), lgb * c, F32))
    contract0 = (((0,), (0,)), ((), ()))
    contract1 = (((1,), (1,)), ((), ()))

    def load(ci_):
        r0 = pl.multiple_of(ci_ * c, c)
        rows = pl.ds(r0, c)
        return rows, q_ref[rows, :], k_ref[rows, :], v_ref[rows, :]

    def bwd(ci_):
        rows, q, k, v = load(ci_)
        s = s_ref[...]
        ob_ref[rows, :] = jnp.dot(q, s.astype(BF16), preferred_element_type=F32) * qdec_b
        kd = (k.astype(F32) * kdec_b).astype(BF16)
        s_ref[...] = s * cdec_b + lax.dot_general(kd, v, contract0, preferred_element_type=F32)

    def fwd(ci_):
        rows, q, k, v = load(ci_)
        s = s_ref[...]
        sc = lax.dot_general(q, k, contract1, preferred_element_type=F32) * mask
        o = jnp.dot(sc.astype(BF16), v, preferred_element_type=F32)
        o += jnp.dot(q, s.astype(BF16), preferred_element_type=F32) * qdec_f
        o += ob_ref[rows, :]
        mu = jnp.mean(o, axis=-1, keepdims=True)
        oc = o - mu
        var = jnp.mean(oc * oc, axis=-1, keepdims=True)
        o_ref[rows, :] = (oc * lax.rsqrt(var + LN_EPS)).astype(BF16)
        kd = (k.astype(F32) * kdec_f).astype(BF16)
        s_ref[...] = s * cdec_f + lax.dot_general(kd, v, contract0, preferred_element_type=F32)

    s_ref[...] = jnp.zeros_like(s_ref)

    def bwd_ctx(t, carry):
        bwd(n_ctx_chunks - 1 - t)
        return carry

    def bwd_lat(t, carry):
        bwd(n_chunks - 1 - t)
        return carry

    lax.fori_loop(0, n_ctx_chunks, bwd_ctx, 0)
    lax.fori_loop(0, n_chunks - n_ctx_chunks, bwd_lat, 0)

    s_ref[...] = jnp.zeros_like(s_ref)

    def fwd_all(t, carry):
        fwd(t)
        return carry

    lax.fori_loop(0, n_chunks, fwd_all, 0)


def _retention(mix, log_decay, n_ctx):
    b, t, _ = mix.shape
    kern = functools.partial(_ret_kernel, n_ctx_chunks=n_ctx // CHUNK, n_chunks=t // CHUNK)
    blk = lambda off: pl.BlockSpec((None, t, RET_DK), lambda bi, h: (bi, 0, off + h))
    return pl.pallas_call(
        kern,
        out_shape=jax.ShapeDtypeStruct((b, t, RET_W), BF16),
        grid=(b, RET_HEADS),
        in_specs=[pl.BlockSpec(memory_space=pltpu.SMEM), blk(0), blk(RET_HEADS), blk(2 * RET_HEADS)],
        out_specs=pl.BlockSpec((None, t, RET_DV), lambda bi, h: (bi, 0, h)),
        scratch_shapes=[pltpu.VMEM((t, RET_DV), F32), pltpu.VMEM((RET_DK, RET_DV), F32)],
        compiler_params=_params("parallel", "arbitrary"),
        name="retention",
    )(log_decay, mix, mix, mix)


def _att_kernel(q_ref, k_ref, v_ref, o_ref, v1_ref, *, tk, n_ctx, t):
    i = pl.program_id(2)
    contract1 = (((1,), (1,)), ((), ()))
    tq = q_ref.shape[0]

    @pl.when(i == 0)
    def _():
        v1_ref[:, :ATT_HD] = v_ref[...]
        v1_ref[:, ATT_HD:] = jnp.ones((t, ATT_HD), BF16)

    def chunk(q, m, acc, r0, size):
        k = k_ref[pl.ds(r0, size), :]
        s = lax.dot_general(q, k, contract1, preferred_element_type=F32)
        m_new = jnp.maximum(m, jnp.max(s, axis=-1, keepdims=True))
        p = jnp.exp(s - m_new)
        alpha = jnp.exp(m - m_new)
        acc = acc * alpha + jnp.dot(p.astype(BF16), v1_ref[pl.ds(r0, size), :], preferred_element_type=F32)
        return m_new, acc

    def finish(g, acc):
        o_ref[:, g * ATT_HD:(g + 1) * ATT_HD] = (acc[:, :ATT_HD] / acc[:, ATT_HD:ATT_HD + 1]).astype(BF16)

    m0 = jnp.full((tq, 1), -jnp.inf, F32)
    a0 = jnp.zeros((tq, 2 * ATT_HD), F32)

    @pl.when(i == 0)
    def _():
        for g in range(ATT_GROUP):
            q = q_ref[:, g * ATT_HD:(g + 1) * ATT_HD]
            _, acc = chunk(q, m0, a0, 0, n_ctx)
            finish(g, acc)

    @pl.when(i > 0)
    def _():
        for g in range(ATT_GROUP):
            q = q_ref[:, g * ATT_HD:(g + 1) * ATT_HD]

            def body(ci, carry):
                return chunk(q, carry[0], carry[1], pl.multiple_of(ci * tk, tk), tk)

            _, acc = lax.fori_loop(0, t // tk, body, (m0, a0))
            finish(g, acc)


def _attention(mix, n_ctx):
    b, t, _ = mix.shape
    tq = ROW_TILE
    tk = 768 if t % 768 == 0 else 256
    q_blk0 = (3 * RET_W) // (ATT_GROUP * ATT_HD)
    k_blk0 = (3 * RET_W + ATT_W) // ATT_HD
    v_blk0 = k_blk0 + ATT_KV_HEADS
    kern = functools.partial(_att_kernel, tk=tk, n_ctx=n_ctx, t=t)
    return pl.pallas_call(
        kern,
        out_shape=jax.ShapeDtypeStruct((b, t, ATT_W), BF16),
        grid=(b, ATT_KV_HEADS, t // tq),
        in_specs=[pl.BlockSpec((None, tq, ATT_GROUP * ATT_HD), lambda bi, hk, i: (bi, i, q_blk0 + hk)),
                  pl.BlockSpec((None, t, ATT_HD), lambda bi, hk, i: (bi, 0, k_blk0 + hk)),
                  pl.BlockSpec((None, t, ATT_HD), lambda bi, hk, i: (bi, 0, v_blk0 + hk))],
        out_specs=pl.BlockSpec((None, tq, ATT_GROUP * ATT_HD), lambda bi, hk, i: (bi, i, hk)),
        scratch_shapes=[pltpu.VMEM((t, 2 * ATT_HD), BF16)],
        compiler_params=_params("parallel", "parallel", "arbitrary"),
        name="gqa_attention",
    )(mix, mix, mix)


def _s5_kernel(uc_ref, win_ref, tsum_ref, wout_ref, lam_ref, y_ref, d_ref, xin_ref, *, nb, nc, ncc):
    p2 = 2 * S5_STATE
    sub = 8
    uc = uc_ref[...]
    d_ref[...] = jnp.dot(uc, win_ref[...], preferred_element_type=F32)
    rowid = lax.broadcasted_iota(jnp.int32, (sub, p2), 0)

    def rep(r):
        return jnp.broadcast_to(lam_ref[r:r + 1, :], (sub, p2))

    def direction(base, reverse):
        levels = [(rep(base + 3 * i), rep(base + 3 * i + 1), rep(base + 3 * i + 2)) for i in range(4)]
        pa = lam_ref[base + 16:base + 24, :]
        pb = lam_ref[base + 24:base + 32, :]

        def shifted(v, n):
            if reverse:
                return jnp.where(rowid < sub - n, pltpu.roll(v, sub - n, 0), 0.0)
            return jnp.where(rowid >= n, pltpu.roll(v, n, 0), 0.0)

        def tile(rows, col, c, cs):
            x = d_ref[rows, col:col + p2]
            xs = d_ref[rows, col + p2:col + 2 * p2]
            for (a, b, bs), n in zip(levels[:3], (1, 2, 4)):
                xr, xsr = shifted(x, n), shifted(xs, n)
                x, xs = x + a * xr + b * xsr, xs + a * xsr + bs * xr
            xin_ref[rows, col // 2:col // 2 + p2] = shifted(x, 1) + pa * c + pb * cs
            edge = 0 if reverse else sub - 1
            last = jnp.broadcast_to(x[edge:edge + 1, :], (sub, p2))
            lasts = jnp.broadcast_to(xs[edge:edge + 1, :], (sub, p2))
            a, b, bs = levels[3]
            return a * c + b * cs + last, a * cs + bs * c + lasts

        return tile

    tile_f = direction(0, False)
    tile_b = direction(32, True)
    nt = nc // sub
    nct = ncc // sub

    def step(mf, mb, carry):
        new = []
        for bi in range(nb):
            cf, cfs, cb, cbs = carry[bi]
            rf = pl.ds(pl.multiple_of(bi * nc + mf * sub, sub), sub)
            rb = pl.ds(pl.multiple_of(bi * nc + mb * sub, sub), sub)
            new.append(tile_f(rf, 0, cf, cfs) + tile_b(rb, 2 * p2, cb, cbs))
        return tuple(new)

    zero = jnp.zeros((sub, p2), F32)
    init = tuple((zero, zero, zero, zero) for _ in range(nb))
    carry = lax.fori_loop(0, nct, lambda t, cr: step(t, nct - 1 - t, cr), init)
    lax.fori_loop(nct, nt, lambda t, cr: step(t, nt - 1 - (t - nct), cr), carry)

    y = jnp.dot(uc, tsum_ref[...], preferred_element_type=F32)
    y += jnp.dot(xin_ref[...].astype(BF16), wout_ref[...], preferred_element_type=F32)
    y_ref[...] = y


def _s5_weights(a_re, a_im, log_dt, b_re, b_im, c_re, c_im, d_skip):
    hi = lax.Precision.HIGHEST
    ln = S5_CHUNK
    dt = jnp.exp(log_dt)[..., None]
    mag = jnp.exp(a_re * dt)
    abr, abi = mag * jnp.cos(a_im * dt), mag * jnp.sin(a_im * dt)
    den = a_re * a_re + a_im * a_im
    fr = ((abr - 1.0) * a_re + abi * a_im) / den
    fi = (abi * a_re - (abr - 1.0) * a_im) / den
    bbr = fr[..., None] * b_re - fi[..., None] * b_im
    bbi = fr[..., None] * b_im + fi[..., None] * b_re
    n = jnp.arange(ln + 1, dtype=F32)[:, None, None, None]
    pmag = jnp.exp(a_re * dt * n)
    pr, pi = pmag * jnp.cos(a_im * dt * n), pmag * jnp.sin(a_im * dt * n)
    cr = c_re[None] * pr[:, :, :, None, :] - c_im[None] * pi[:, :, :, None, :]
    ci = c_re[None] * pi[:, :, :, None, :] + c_im[None] * pr[:, :, :, None, :]
    m = (jnp.einsum('ndgcp,dgpe->ndgce', cr[:ln], bbr, precision=hi)
         - jnp.einsum('ndgcp,dgpe->ndgce', ci[:ln], bbi, precision=hi))
    s_idx = jnp.arange(ln)[:, None]
    i_idx = jnp.arange(ln)[None, :]
    lag_f = i_idx - s_idx
    tf = jnp.where((lag_f >= 0)[:, :, None, None, None], m[jnp.clip(lag_f, 0, ln - 1), 0], 0.0)
    tb = jnp.where((lag_f <= 0)[:, :, None, None, None], m[jnp.clip(-lag_f, 0, ln - 1), 1], 0.0)
    tsum = (tf + tb).transpose(2, 0, 4, 1, 3)
    g = a_re.shape[1]
    eye_s = jnp.eye(ln, dtype=F32)[None, :, None, :, None]
    eye_c = jnp.eye(S5_GROUP, dtype=F32)[None, None, :, None, :]
    tsum = tsum + eye_s * eye_c * d_skip.reshape(g, 1, S5_GROUP, 1, 1)
    tsum = tsum.reshape(g, ln * S5_GROUP, ln * S5_GROUP)

    def w_in(pw_r, pw_i, d):
        re = pw_r[..., None] * bbr[d][None] - pw_i[..., None] * bbi[d][None]
        im = pw_r[..., None] * bbi[d][None] + pw_i[..., None] * bbr[d][None]
        re = re.transpose(1, 0, 3, 2).reshape(g, ln * S5_GROUP, S5_STATE)
        im = im.transpose(1, 0, 3, 2).reshape(g, ln * S5_GROUP, S5_STATE)
        return jnp.concatenate([re, im, im, re], axis=-1)

    win = jnp.concatenate([w_in(pr[ln - 1 - jnp.arange(ln), 0], pi[ln - 1 - jnp.arange(ln), 0], 0),
                           w_in(pr[jnp.arange(ln), 1], pi[jnp.arange(ln), 1], 1)], axis=-1)

    def w_out(cr_d, ci_d):
        re = cr_d.transpose(1, 3, 0, 2).reshape(g, S5_STATE, ln * S5_GROUP)
        im = (-ci_d).transpose(1, 3, 0, 2).reshape(g, S5_STATE, ln * S5_GROUP)
        return jnp.concatenate([re, im], axis=1)

    wout = jnp.concatenate([w_out(cr[1:ln + 1, 0], ci[1:ln + 1, 0]),
                            w_out(cr[ln - jnp.arange(ln), 1], ci[ln - jnp.arange(ln), 1])], axis=1)
    def cpow(k):
        e = (jnp.asarray(k, F32) * ln)[..., None, None, None]
        mg = jnp.exp(a_re * dt * e)
        return mg * jnp.cos(a_im * dt * e), mg * jnp.sin(a_im * dt * e)

    mr, mi = cpow(jnp.array([1, 2, 4, 8]))
    rows = []
    for d in range(2):
        for i in range(4):
            r_, i_ = mr[i, d], mi[i, d]
            rows += [jnp.concatenate([r_, r_], -1), jnp.concatenate([-i_, i_], -1), jnp.concatenate([i_, -i_], -1)]
        rows += [jnp.zeros_like(rows[0])] * 4
        j = jnp.arange(8)
        qr, qi = cpow(j if d == 0 else 7 - j)
        rows += [jnp.concatenate([qr[k, d], qr[k, d]], -1) for k in range(8)]
        rows += [jnp.concatenate([-qi[k, d], qi[k, d]], -1) for k in range(8)]
    lam = jnp.stack(rows, axis=1)
    return win.astype(BF16), tsum.astype(BF16), wout.astype(BF16), lam


def _s5(mix, s5w, n_ctx):
    b, t, _ = mix.shape
    win, tsum, wout, lam = s5w
    g = S5_GROUPS
    nc = t // S5_CHUNK
    u0 = 3 * RET_W + ATT_W + 2 * ATT_KV_W
    u = mix[:, :, u0:u0 + S5_W]
    uc = u.reshape(b, nc, S5_CHUNK, g, S5_GROUP).transpose(3, 0, 1, 2, 4).reshape(g, b * nc, S5_CHUNK * S5_GROUP)
    r = b * nc
    kw = S5_CHUNK * S5_GROUP
    kern = functools.partial(_s5_kernel, nb=b, nc=nc, ncc=n_ctx // S5_CHUNK)
    y = pl.pallas_call(
        kern,
        out_shape=jax.ShapeDtypeStruct((g, r, kw), F32),
        grid=(g,),
        in_specs=[pl.BlockSpec((None, r, kw), lambda gi: (gi, 0, 0)),
                  pl.BlockSpec((None, kw, 8 * S5_STATE), lambda gi: (gi, 0, 0)),
                  pl.BlockSpec((None, kw, kw), lambda gi: (gi, 0, 0)),
                  pl.BlockSpec((None, 4 * S5_STATE, kw), lambda gi: (gi, 0, 0)),
                  pl.BlockSpec((None, 64, 2 * S5_STATE), lambda gi: (gi, 0, 0))],
        out_specs=pl.BlockSpec((None, r, kw), lambda gi: (gi, 0, 0)),
        scratch_shapes=[pltpu.VMEM((r, 8 * S5_STATE), F32), pltpu.VMEM((r, 4 * S5_STATE), F32)],
        compiler_params=_params("parallel"),
        name="s5_scan",
    )(uc, win, tsum, wout, lam)
    return y.reshape(g, b, nc, S5_CHUNK, S5_GROUP).transpose(1, 2, 3, 0, 4).reshape(b, t, S5_W)


def _glu_kernel(s_ref, w_ref, b_ref, o_ref):
    s = jax.nn.gelu(s_ref[...])
    z = jnp.dot(s.astype(BF16), w_ref[...], preferred_element_type=F32) + b_ref[...]
    o_ref[:, :S5_W] = (s * jax.nn.sigmoid(z)).astype(BF16)
    o_ref[:, S5_W:] = jnp.zeros((o_ref.shape[0], o_ref.shape[1] - S5_W), BF16)


def _s5_glu(s_pre, glu_w, glu_b):
    b, t, _ = s_pre.shape
    tm = _big_row_tile(t)
    wpad = 2 * COL_TILE
    return pl.pallas_call(
        _glu_kernel,
        out_shape=jax.ShapeDtypeStruct((b, t, wpad), BF16),
        grid=(b, t // tm),
        in_specs=[pl.BlockSpec((None, tm, S5_W), lambda bi, i: (bi, i, 0)),
                  pl.BlockSpec((S5_W, S5_W), lambda bi, i: (0, 0)),
                  pl.BlockSpec((1, S5_W), lambda bi, i: (0, 0))],
        out_specs=pl.BlockSpec((None, tm, wpad), lambda bi, i: (bi, i, 0)),
        compiler_params=_params("parallel", "arbitrary"),
        name="s5_glu",
    )(s_pre, glu_w, glu_b)


def _gate_kernel(u_ref, w_ref, r_ref, a_ref, s_ref, gn_ref, o_ref):
    j = pl.program_id(2)
    g = jnp.dot(u_ref[...], w_ref[...], preferred_element_type=F32)
    g = g * jax.nn.sigmoid(g)

    @pl.when(j < 2)
    def _():
        o_ref[...] = (r_ref[...].astype(F32) * gn_ref[...] * g).astype(BF16)

    @pl.when((j >= 2) & (j < 4))
    def _():
        o_ref[...] = (a_ref[...].astype(F32) * g).astype(BF16)

    @pl.when(j >= 4)
    def _():
        o_ref[...] = (s_ref[...].astype(F32) * g).astype(BF16)


def _branch_gates(u, w_gate, r, a, s2, gn_w):
    b, t, d = u.shape
    tm = _big_row_tile(t)
    act = lambda off: pl.BlockSpec((None, tm, COL_TILE), lambda bi, i, j: (bi, i, jnp.clip(j - off, 0, 1)))
    return pl.pallas_call(
        _gate_kernel,
        out_shape=jax.ShapeDtypeStruct((b, t, GATE_W), BF16),
        grid=(b, t // tm, GATE_W // COL_TILE),
        in_specs=[pl.BlockSpec((None, tm, d), lambda bi, i, j: (bi, i, 0)),
                  pl.BlockSpec((d, COL_TILE), lambda bi, i, j: (0, j)),
                  act(0), act(2), act(4),
                  pl.BlockSpec((1, COL_TILE), lambda bi, i, j: (0, jnp.clip(j, 0, 1)))],
        out_specs=pl.BlockSpec((None, tm, COL_TILE), lambda bi, i, j: (bi, i, j)),
        compiler_params=_params("parallel", "parallel", "arbitrary"),
        name="branch_gates",
    )(u, w_gate, r, a, s2, gn_w)


def _merge_kernel(u_ref, z_ref, wm_ref, wb_ref, o_ref):
    u = u_ref[...]
    acc = None
    for br in range(N_BRANCH):
        gate = jax.nn.sigmoid(jnp.dot(u, wm_ref[br], preferred_element_type=F32))
        zb = z_ref[:, br * 2 * COL_TILE:(br + 1) * 2 * COL_TILE]
        proj = jnp.dot(zb, wb_ref[br * 2 * COL_TILE:(br + 1) * 2 * COL_TILE, :], preferred_element_type=F32)
        acc = gate * proj if acc is None else acc + gate * proj
    o_ref[...] = acc.astype(BF16)


def _merge(u, z, w_merge, w_branch):
    b, t, d = u.shape
    tm = 528 if t % 528 == 0 else _big_row_tile(t)
    return pl.pallas_call(
        _merge_kernel,
        out_shape=jax.ShapeDtypeStruct((b, t, d), BF16),
        grid=(b, t // tm, d // COL_TILE),
        in_specs=[pl.BlockSpec((None, tm, d), lambda bi, i, j: (bi, i, 0)),
                  pl.BlockSpec((None, tm, GATE_W), lambda bi, i, j: (bi, i, 0)),
                  pl.BlockSpec((N_BRANCH, d, COL_TILE), lambda bi, i, j: (0, 0, j)),
                  pl.BlockSpec((GATE_W, COL_TILE), lambda bi, i, j: (0, j))],
        out_specs=pl.BlockSpec((None, tm, COL_TILE), lambda bi, i, j: (bi, i, j)),
        compiler_params=_params("parallel", "parallel", "arbitrary"),
        name="gated_merge",
    )(u, z, w_merge, w_branch)


def _out_kernel(m_ref, w_ref, x_ref, mod_ref, lnw_ref, lnb_ref, o_ref, *, alpha):
    y = jnp.dot(m_ref[...], w_ref[...], preferred_element_type=F32)
    v = alpha * x_ref[...] + mod_ref[2:3, :] * y
    mu = jnp.mean(v, axis=-1, keepdims=True)
    vc = v - mu
    var = jnp.mean(vc * vc, axis=-1, keepdims=True)
    o_ref[...] = vc * lax.rsqrt(var + LN_EPS) * lnw_ref[...] + lnb_ref[...]


def _out_proj(m, w_out, xa, modsel, ln_w, ln_b, alpha, skip_ctx):
    b, t, d = xa.shape
    off = 1 if skip_ctx else 0
    nt = t // ROW_TILE - off
    kern = functools.partial(_out_kernel, alpha=alpha)
    row = lambda: pl.BlockSpec((None, ROW_TILE, d), lambda bi, i: (bi, i + off, 0))
    vec = lambda: pl.BlockSpec((1, d), lambda bi, i: (0, 0))
    return pl.pallas_call(
        kern,
        out_shape=jax.ShapeDtypeStruct((b, nt * ROW_TILE, d), F32),
        grid=(b, nt),
        in_specs=[row(), pl.BlockSpec((d, d), lambda bi, i: (0, 0)), row(),
                  pl.BlockSpec((None, None, 3, d), lambda bi, i: (bi, jnp.minimum(i + off, 1), 0, 0)),
                  vec(), vec()],
        out_specs=pl.BlockSpec((None, ROW_TILE, d), lambda bi, i: (bi, i, 0)),
        compiler_params=_params("parallel", "arbitrary"),
        name="out_proj_norm",
    )(m, w_out, xa, modsel, ln_w.reshape(1, d), ln_b.reshape(1, d))


def _rope_tables(n, n_ctx):
    rows = n // GRID_W
    row = jnp.repeat(jnp.arange(rows, dtype=F32), GRID_W)
    col = jnp.tile(jnp.arange(GRID_W, dtype=F32), rows)

    def table(head_dim):
        per_axis = head_dim // 4
        inv = ROPE_THETA ** (-jnp.arange(per_axis, dtype=F32) / per_axis)
        ang = jnp.concatenate([row[:, None] * inv, col[:, None] * inv], axis=-1)
        cos = jnp.concatenate([jnp.ones((n_ctx, head_dim // 2), F32), jnp.cos(ang)], axis=0)
        sin = jnp.concatenate([jnp.zeros((n_ctx, head_dim // 2), F32), jnp.sin(ang)], axis=0)
        return cos, sin

    cos_r, sin_r = table(RET_DK)
    cos_a, sin_a = table(ATT_HD)
    return (cos_r, sin_r, jnp.concatenate([cos_a, cos_a], -1), jnp.concatenate([-sin_a, sin_a], -1))


def _pack_weights(w_in):
    d = w_in.shape[0]
    o = 0
    seg = {}
    for name, width in (("ret_q", RET_W), ("ret_k", RET_W), ("ret_v", RET_W), ("ret_g", RET_W),
                        ("att_q", ATT_W), ("att_k", ATT_KV_W), ("att_v", ATT_KV_W), ("att_g", ATT_W),
                        ("s5_u", S5_W), ("s5_g", S5_W), ("merge", N_BRANCH * d)):
        seg[name] = w_in[:, o:o + width]
        o += width
    pad = jnp.zeros((d, 2 * COL_TILE - S5_W), w_in.dtype)
    w_mix = jnp.concatenate([seg["ret_q"], seg["ret_k"], seg["ret_v"], seg["att_q"], seg["att_k"],
                             seg["att_v"], seg["s5_u"], pad], axis=1).astype(BF16)
    w_gate = jnp.concatenate([seg["ret_g"], seg["att_g"], seg["s5_g"], pad], axis=1).astype(BF16)
    w_merge = seg["merge"].reshape(d, N_BRANCH, d).transpose(1, 0, 2).astype(BF16)
    return w_mix, w_gate, w_merge


def kernel(x, c, ctx, c_ctx, ada_w, ada_b, w_in, ret_log_decay, ret_gn_w, att_q_norm, att_k_norm, s5_a_re, s5_a_im, s5_log_dt, s5_b_re, s5_b_im, s5_c_re, s5_c_im, s5_d, s5_glu_w, s5_glu_b, w_br_ret, w_br_att, w_br_s5, w_out, ln_w, ln_b):
    b, n, d = x.shape
    n_ctx = ctx.shape[1]
    depth = w_in.shape[0]
    assert n_ctx == ROW_TILE and n % ROW_TILE == 0 and n % GRID_W == 0
    assert w_in.shape[2] == 4 * RET_W + 2 * ATT_W + 2 * ATT_KV_W + 2 * S5_W + N_BRANCH * d
    alpha = (2.0 * depth) ** 0.25

    xa = jnp.concatenate([ctx, x], axis=1)
    cos_r, sin_r, cos_a, sin_a = _rope_tables(n, n_ctx)
    crows = 16
    cvec = jnp.concatenate([c, c_ctx[None, :], jnp.zeros((crows - b - 1, d), F32)], axis=0)

    for l in range(depth):
        last = l == depth - 1
        mod = _modulation(cvec, ada_w[l], ada_b[l]).reshape(crows, 3, d)
        modsel = jnp.stack([jnp.broadcast_to(mod[b][None], (b, 3, d)), mod[:b]], axis=1)
        w_mix, w_gate, w_merge = _pack_weights(w_in[l])
        w_branch = jnp.concatenate([w_br_ret[l], w_br_att[l], w_br_s5[l],
                                    jnp.zeros((2 * COL_TILE - S5_W, d), F32)], axis=0).astype(BF16)
        s5w = _s5_weights(s5_a_re[l], s5_a_im[l], s5_log_dt[l], s5_b_re[l], s5_b_im[l],
                          s5_c_re[l], s5_c_im[l], s5_d[l])

        u = _modulate(xa, modsel)
        mix = _mix_proj(u, w_mix, cos_r, sin_r, cos_a, sin_a,
                        att_q_norm[l].reshape(1, ATT_HD), att_k_norm[l].reshape(1, ATT_HD))
        r = _retention(mix, ret_log_decay[l], n_ctx)
        a = _attention(mix, n_ctx)
        s_pre = _s5(mix, s5w, n_ctx)
        s2 = _s5_glu(s_pre, s5_glu_w[l].astype(BF16), s5_glu_b[l].reshape(1, S5_W))
        z = _branch_gates(u, w_gate, r, a, s2, ret_gn_w[l].reshape(1, RET_W))
        m = _merge(u, z, w_merge, w_branch)
        xa = _out_proj(m, w_out[l].astype(BF16), xa, modsel, ln_w[l], ln_b[l], alpha, last)
    return xa
```

```python
import functools
import math

import jax
import jax.numpy as jnp
from jax import lax
from jax.experimental import pallas as pl
from jax.experimental.pallas import tpu as pltpu

F32 = jnp.float32
BF16 = jnp.bfloat16

GRID_W = 64
RET_HEADS = 4
RET_DK = 256
RET_DV = 256
RET_W = RET_HEADS * RET_DV
ATT_HEADS = 8
ATT_KV_HEADS = 2
ATT_GROUP = ATT_HEADS // ATT_KV_HEADS
ATT_HD = 128
ATT_W = ATT_HEADS * ATT_HD
ATT_KV_W = ATT_KV_HEADS * ATT_HD
ROPE_THETA = 10000.0
S5_GROUP = 16
S5_W = 768
S5_GROUPS = S5_W // S5_GROUP
S5_STATE = 64
N_BRANCH = 3
LN_EPS = 1e-6
RMS_EPS = 1e-6

LANES = 128
VMEM_LIMIT_BYTES = 56 * 1024 * 1024

CHUNK = 128
S5_CHUNK = 16
ROW_TILE = 256
COL_TILE = 512
MIX_W = 4608
ATT_SUB = 512
LOG2E = math.log2(math.e)


def _params(*sem):
    return pltpu.CompilerParams(dimension_semantics=sem, vmem_limit_bytes=VMEM_LIMIT_BYTES)


def _big_row_tile(t):
    for cand in (1056, 1024, 768, 640, 512, 256):
        if t % cand == 0:
            return cand
    raise ValueError(f"unsupported token count {t}")


def _mod_kernel(c_ref, w_ref, b_ref, o_ref):
    c = c_ref[...]
    s = c * jax.nn.sigmoid(c)
    s_hi = s.astype(BF16)
    s_lo = (s - s_hi.astype(F32)).astype(BF16)
    w = w_ref[...]
    w_hi = w.astype(BF16)
    w_lo = (w - w_hi.astype(F32)).astype(BF16)
    acc = jnp.dot(s_hi, w_hi, preferred_element_type=F32)
    acc += jnp.dot(s_lo, w_hi, preferred_element_type=F32)
    acc += jnp.dot(s_hi, w_lo, preferred_element_type=F32)
    o_ref[...] = acc + b_ref[...]


def _modulation(cvec, ada_w, ada_b):
    rows, d = cvec.shape
    n = ada_w.shape[1]
    tn = 768
    return pl.pallas_call(
        _mod_kernel,
        out_shape=jax.ShapeDtypeStruct((rows, n), F32),
        grid=(n // tn,),
        in_specs=[pl.BlockSpec((rows, d), lambda j: (0, 0)),
                  pl.BlockSpec((d, tn), lambda j: (0, j)),
                  pl.BlockSpec((1, tn), lambda j: (0, j))],
        out_specs=pl.BlockSpec((rows, tn), lambda j: (0, j)),
        compiler_params=_params("arbitrary"),
        name="adaln_modulation",
    )(cvec, ada_w, ada_b.reshape(1, n))


def _modulate_kernel(x_ref, mod_ref, o_ref):
    m = mod_ref[...]
    o_ref[...] = (x_ref[...] * (1.0 + m[1:2]) + m[0:1]).astype(BF16)


def _modulate(xa, modsel):
    b, t, d = xa.shape
    return pl.pallas_call(
        _modulate_kernel,
        out_shape=jax.ShapeDtypeStruct((b, t, d), BF16),
        grid=(b, t // ROW_TILE),
        in_specs=[pl.BlockSpec((None, ROW_TILE, d), lambda bi, i: (bi, i, 0)),
                  pl.BlockSpec((None, None, 3, d), lambda bi, i: (bi, jnp.minimum(i, 1), 0, 0))],
        out_specs=pl.BlockSpec((None, ROW_TILE, d), lambda bi, i: (bi, i, 0)),
        compiler_params=_params("parallel", "arbitrary"),
        name="modulate",
    )(xa, modsel)


def _mix_kernel(u_ref, w_ref, cr_ref, sr_ref, ca_ref, sa_ref, qn_ref, kn_ref, o_ref):
    j = pl.program_id(2)
    acc = jnp.dot(u_ref[...], w_ref[...], preferred_element_type=F32)

    def rms_rope(x, w, scale):
        ms = jnp.mean(x * x, axis=-1, keepdims=True)
        y = x * lax.rsqrt(ms + RMS_EPS) * w
        y = y * ca_ref[...] + pltpu.roll(y, ATT_HD // 2, 1) * sa_ref[...]
        return (y * scale).astype(BF16)

    @pl.when(j < 4)
    def _():
        scale = jnp.where(j >= 2, RET_DK ** -0.5, 1.0).astype(F32)
        cos = cr_ref[...]
        sin = sr_ref[...]
        for h in range(COL_TILE // RET_DK):
            x1 = acc[:, h * RET_DK: h * RET_DK + LANES]
            x2 = acc[:, h * RET_DK + LANES: (h + 1) * RET_DK]
            o_ref[:, h * RET_DK: h * RET_DK + LANES] = ((x1 * cos - x2 * sin) * scale).astype(BF16)
            o_ref[:, h * RET_DK + LANES: (h + 1) * RET_DK] = ((x2 * cos + x1 * sin) * scale).astype(BF16)

    @pl.when((j == 4) | (j == 5))
    def _():
        o_ref[...] = acc.astype(BF16)

    @pl.when((j == 6) | (j == 7))
    def _():
        for h in range(COL_TILE // ATT_HD):
            o_ref[:, h * ATT_HD:(h + 1) * ATT_HD] = rms_rope(
                acc[:, h * ATT_HD:(h + 1) * ATT_HD], qn_ref[...], ATT_HD ** -0.5 * LOG2E)

    @pl.when(j == 8)
    def _():
        for h in range(ATT_KV_HEADS):
            o_ref[:, h * ATT_HD:(h + 1) * ATT_HD] = rms_rope(
                acc[:, h * ATT_HD:(h + 1) * ATT_HD], kn_ref[...], 1.0)
        o_ref[:, ATT_KV_W:] = acc[:, ATT_KV_W:].astype(BF16)


def _mix_proj(u, wb, cos_r, sin_r, cos_a, sin_a, qn, kn):
    b, t, d = u.shape
    tm = _big_row_tile(t)
    tab = lambda: pl.BlockSpec((tm, LANES), lambda bi, i, j: (i, 0))
    vec = lambda: pl.BlockSpec((1, LANES), lambda bi, i, j: (0, 0))
    gap = RET_W // COL_TILE
    return pl.pallas_call(
        _mix_kernel,
        out_shape=jax.ShapeDtypeStruct((b, t, MIX_W), BF16),
        grid=(b, t // tm, MIX_W // COL_TILE),
        in_specs=[pl.BlockSpec((None, tm, d), lambda bi, i, j: (bi, i, 0)),
                  pl.BlockSpec((d, COL_TILE), lambda bi, i, j: (0, jnp.where(j >= 6, j + gap, j))),
                  tab(), tab(), tab(), tab(), vec(), vec()],
        out_specs=pl.BlockSpec((None, tm, COL_TILE), lambda bi, i, j: (bi, i, j)),
        compiler_params=_params("parallel", "parallel", "arbitrary"),
        name="mixer_in_proj",
    )(u, wb, cos_r, sin_r, cos_a, sin_a, qn, kn)


def _ret_kernel(ld_ref, q_ref, k_ref, v_ref, o_ref, ob_ref, s_ref, *, n_ctx_chunks, n_chunks):
    h = pl.program_id(1)
    lgf = ld_ref[0, h]
    lgb = ld_ref[1, h]
    c = CHUNK
    ri = lax.broadcasted_iota(jnp.int32, (c, c), 0).astype(F32)
    ci = lax.broadcasted_iota(jnp.int32, (c, c), 1).astype(F32)
    rel = ri - ci
    mask = jnp.where(rel >= 0, jnp.exp(lgf * jnp.maximum(rel, 0.0)), jnp.exp(lgb * jnp.maximum(-rel, 0.0)))
    row = lax.broadcasted_iota(jnp.int32, (c, RET_DV), 0).astype(F32)
    qdec_f = jnp.exp(lgf * (row + 1.0))
    qdec_b = jnp.exp(lgb * (c - row))
    kdec_f = jnp.exp(lgf * (c - 1.0 - row))
    kdec_b = jnp.exp(lgb * row)
    cdec_f = jnp.exp(jnp.full((1, RET_DV), lgf * c, F32))
    cdec_b = jnp.exp(jnp.full((1, RET_DV), lgb * c, F32))
    contract0 = (((0,), (0,)), ((), ()))
    contract1 = (((1,), (1,)), ((), ()))

    def load(ci_):
        r0 = pl.multiple_of(ci_ * c, c)
        rows = pl.ds(r0, c)
        return rows, q_ref[rows, :], k_ref[rows, :], v_ref[rows, :]

    def bwd(ci_):
        rows, q, k, v = load(ci_)
        s = s_ref[...]
        ob_ref[rows, :] = jnp.dot(q, s.astype(BF16), preferred_element_type=F32) * qdec_b
        kd = (k.astype(F32) * kdec_b).astype(BF16)
        s_ref[...] = s * cdec_b + lax.dot_general(kd, v, contract0, preferred_element_type=F32)

    def fwd(ci_):
        rows, q, k, v = load(ci_)
        s = s_ref[...]
        sc = lax.dot_general(q, k, contract1, preferred_element_type=F32) * mask
        o = jnp.dot(sc.astype(BF16), v, preferred_element_type=F32)
        o += jnp.dot(q, s.astype(BF16), preferred_element_type=F32) * qdec_f
        o += ob_ref[rows, :]
        mu = jnp.mean(o, axis=-1, keepdims=True)
        oc = o - mu
        var = jnp.mean(oc * oc, axis=-1, keepdims=True)
        o_ref[rows, :] = (oc * lax.rsqrt(var + LN_EPS)).astype(BF16)
        kd = (k.astype(F32) * kdec_f).astype(BF16)
        s_ref[...] = s * cdec_f + lax.dot_general(kd, v, contract0, preferred_element_type=F32)

    s_ref[...] = jnp.zeros_like(s_ref)

    def bwd_ctx(t, carry):
        bwd(n_ctx_chunks - 1 - t)
        return carry

    def bwd_lat(t, carry):
        bwd(n_chunks - 1 - t)
        return carry

    lax.fori_loop(0, n_ctx_chunks, bwd_ctx, 0)
    lax.fori_loop(0, n_chunks - n_ctx_chunks, bwd_lat, 0)

    s_ref[...] = jnp.zeros_like(s_ref)

    def fwd_all(t, carry):
        fwd(t)
        return carry

    lax.fori_loop(0, n_chunks, fwd_all, 0)


def _retention(mix, log_decay, n_ctx):
    b, t, _ = mix.shape
    kern = functools.partial(_ret_kernel, n_ctx_chunks=n_ctx // CHUNK, n_chunks=t // CHUNK)
    blk = lambda off: pl.BlockSpec((None, t, RET_DK), lambda bi, h: (bi, 0, off + h))
    return pl.pallas_call(
        kern,
        out_shape=jax.ShapeDtypeStruct((b, t, RET_W), BF16),
        grid=(b, RET_HEADS),
        in_specs=[pl.BlockSpec(memory_space=pltpu.SMEM), blk(0), blk(RET_HEADS), blk(2 * RET_HEADS)],
        out_specs=pl.BlockSpec((None, t, RET_DV), lambda bi, h: (bi, 0, h)),
        scratch_shapes=[pltpu.VMEM((t, RET_DV), F32), pltpu.VMEM((RET_DK, RET_DV), F32)],
        compiler_params=_params("parallel", "arbitrary"),
        name="retention",
    )(log_decay, mix, mix, mix)


def _att_kernel(q_ref, k_ref, v_ref, o_ref, v1_ref, q4_ref, s_ref, mx_ref, m_ref, acc_ref, *, tkc, n_ctx, t):
    i = pl.program_id(2)
    contract1 = (((1,), (1,)), ((), ()))
    tq = q_ref.shape[0]
    sb = ATT_SUB

    @pl.when(i == 0)
    def _():
        v1_ref[:, :ATT_HD] = v_ref[...]
        v1_ref[:, ATT_HD:] = jnp.ones((t, ATT_HD), BF16)

    for g in range(ATT_GROUP):
        q4_ref[g * tq:(g + 1) * tq, :] = q_ref[:, g * ATT_HD:(g + 1) * ATT_HD]
    neg = jnp.full(mx_ref.shape, -jnp.inf, F32)
    mx_ref[...] = neg
    m_ref[...] = neg
    acc_ref[...] = jnp.zeros(acc_ref.shape, F32)

    def a_step(buf, j, r0, size):
        k = k_ref[pl.ds(r0, size), :]
        s = lax.dot_general(q4_ref[...], k, contract1, preferred_element_type=F32)
        s_ref[buf, j, :, 0:size] = s
        mx = mx_ref[...]
        for tt in range(size // LANES):
            mx = jnp.maximum(mx, s[:, tt * LANES:(tt + 1) * LANES])
        mx_ref[...] = mx

    def settle():
        mx = mx_ref[...]
        mc = jnp.broadcast_to(jnp.max(mx, axis=-1, keepdims=True), mx.shape)
        m_old = m_ref[...]
        m_new = jnp.maximum(m_old, mc)
        alpha = jnp.exp2(m_old - m_new)
        m_ref[...] = m_new
        acc_ref[:, :ATT_HD] = acc_ref[:, :ATT_HD] * alpha
        acc_ref[:, ATT_HD:] = acc_ref[:, ATT_HD:] * alpha
        mx_ref[...] = neg

    def b_step(buf, j, r0, size):
        mb = m_ref[...]
        p = jnp.concatenate([jnp.exp2(s_ref[buf, j, :, tt * LANES:(tt + 1) * LANES] - mb)
                             for tt in range(size // LANES)], axis=1).astype(BF16)
        acc_ref[...] += jnp.dot(p, v1_ref[pl.ds(r0, size), :], preferred_element_type=F32)

    def stage(ca, cb, size_total):
        nfull, rem = size_total // sb, size_total % sb

        def both(j, r_off, size):
            if ca is not None:
                a_step(ca % 2, j, ca * tkc + r_off, size)
            if cb is not None:
                b_step(cb % 2, j, cb * tkc + r_off, size)

        def body(j, carry):
            both(j, pl.multiple_of(j * sb, sb), sb)
            return carry

        if nfull:
            lax.fori_loop(0, nfull, body, 0)
        if rem:
            both(nfull, nfull * sb, rem)
        if ca is not None:
            settle()

    @pl.when(i == 0)
    def _():
        stage(0, None, n_ctx)
        stage(None, 0, n_ctx)

    @pl.when(i > 0)
    def _():
        nch = t // tkc
        for c in range(nch + 1):
            stage(c if c < nch else None, c - 1 if c > 0 else None, tkc)

    for g in range(ATT_GROUP):
        rows = slice(g * tq, (g + 1) * tq)
        o_ref[:, g * ATT_HD:(g + 1) * ATT_HD] = (acc_ref[rows, :ATT_HD] / acc_ref[rows, ATT_HD:]).astype(BF16)


def _attention(mix, n_ctx):
    b, t, _ = mix.shape
    tq = ROW_TILE
    tkc = t // 3 if t % (3 * ATT_HD) == 0 else t
    mrows = ATT_GROUP * tq
    q_blk0 = (3 * RET_W) // (ATT_GROUP * ATT_HD)
    k_blk0 = (3 * RET_W + ATT_W) // ATT_HD
    v_blk0 = k_blk0 + ATT_KV_HEADS
    kern = functools.partial(_att_kernel, tkc=tkc, n_ctx=n_ctx, t=t)
    return pl.pallas_call(
        kern,
        out_shape=jax.ShapeDtypeStruct((b, t, ATT_W), BF16),
        grid=(b, ATT_KV_HEADS, t // tq),
        in_specs=[pl.BlockSpec((None, tq, ATT_GROUP * ATT_HD), lambda bi, hk, i: (bi, i, q_blk0 + hk)),
                  pl.BlockSpec((None, t, ATT_HD), lambda bi, hk, i: (bi, 0, k_blk0 + hk)),
                  pl.BlockSpec((None, t, ATT_HD), lambda bi, hk, i: (bi, 0, v_blk0 + hk))],
        out_specs=pl.BlockSpec((None, tq, ATT_GROUP * ATT_HD), lambda bi, hk, i: (bi, i, hk)),
        scratch_shapes=[pltpu.VMEM((t, 2 * ATT_HD), BF16),
                        pltpu.VMEM((mrows, ATT_HD), BF16),
                        pltpu.VMEM((2, -(-tkc // ATT_SUB), mrows, ATT_SUB), F32),
                        pltpu.VMEM((mrows, LANES), F32),
                        pltpu.VMEM((mrows, LANES), F32),
                        pltpu.VMEM((mrows, 2 * ATT_HD), F32)],
        compiler_params=_params("parallel", "parallel", "arbitrary"),
        name="gqa_attention",
    )(mix, mix, mix)


def _s5_kernel(u_ref, win_ref, tsum_ref, wout_ref, lam_ref, y_ref, uc_ref, d_ref, xin_ref, *, nc, ncc):
    p2 = 2 * S5_STATE
    sub = 8
    ln = S5_CHUNK
    gpb = LANES // S5_GROUP
    us = [u_ref[pl.ds(s, nc, stride=ln), :] for s in range(ln)]
    for g in range(gpb):
        uc = jnp.concatenate([us[s][:, g * S5_GROUP:(g + 1) * S5_GROUP] for s in range(ln)], axis=1).astype(BF16)
        uc_ref[g] = uc
        d_ref[g] = jnp.dot(uc, win_ref[g], preferred_element_type=F32)
    rowid = lax.broadcasted_iota(jnp.int32, (sub, p2), 0)

    def tile(g, reverse, rows, c, cs):
        base = 32 if reverse else 0
        col = 2 * p2 if reverse else 0

        def mult(i):
            return tuple(jnp.broadcast_to(lam_ref[g, base + 3 * i + r:base + 3 * i + r + 1, :], (sub, p2))
                         for r in range(3))

        def shifted(v, n):
            if reverse:
                return jnp.where(rowid < sub - n, pltpu.roll(v, sub - n, 0), 0.0)
            return jnp.where(rowid >= n, pltpu.roll(v, n, 0), 0.0)

        x = d_ref[g, rows, col:col + p2]
        xs = d_ref[g, rows, col + p2:col + 2 * p2]
        for i, n in enumerate((1, 2, 4)):
            a, b, bs = mult(i)
            xr, xsr = shifted(x, n), shifted(xs, n)
            x, xs = x + a * xr + b * xsr, xs + a * xsr + bs * xr
        pa = lam_ref[g, base + 16:base + 24, :]
        pb = lam_ref[g, base + 24:base + 32, :]
        xin_ref[g, rows, col // 2:col // 2 + p2] = shifted(x, 1) + pa * c + pb * cs
        edge = 0 if reverse else sub - 1
        last = jnp.broadcast_to(x[edge:edge + 1, :], (sub, p2))
        lasts = jnp.broadcast_to(xs[edge:edge + 1, :], (sub, p2))
        a, b, bs = mult(3)
        return a * c + b * cs + last, a * cs + bs * c + lasts

    nt = nc // sub
    nct = ncc // sub

    def step(mf, mb, carry):
        rf = pl.ds(pl.multiple_of(mf * sub, sub), sub)
        rb = pl.ds(pl.multiple_of(mb * sub, sub), sub)
        return tuple(tile(g, False, rf, carry[g][0], carry[g][1]) + tile(g, True, rb, carry[g][2], carry[g][3])
                     for g in range(gpb))

    zero = jnp.zeros((sub, p2), F32)
    init = tuple((zero, zero, zero, zero) for _ in range(gpb))
    carry = lax.fori_loop(0, nct, lambda t, cr: step(t, nct - 1 - t, cr), init)
    lax.fori_loop(nct, nt, lambda t, cr: step(t, nt - 1 - (t - nct), cr), carry)

    for g in range(gpb):
        y = jnp.dot(uc_ref[g], tsum_ref[g], preferred_element_type=F32)
        y += jnp.dot(xin_ref[g].astype(BF16), wout_ref[g], preferred_element_type=F32)
        d_ref[g, :, 0:ln * S5_GROUP] = y
    for s in range(ln):
        y_ref[pl.ds(s, nc, stride=ln), :] = jnp.concatenate(
            [d_ref[g, :, s * S5_GROUP:(s + 1) * S5_GROUP] for g in range(gpb)], axis=1)


def _s5_weights(a_re, a_im, log_dt, b_re, b_im, c_re, c_im, d_skip):
    hi = lax.Precision.HIGHEST
    ln = S5_CHUNK
    dt = jnp.exp(log_dt)[..., None]
    mag = jnp.exp(a_re * dt)
    abr, abi = mag * jnp.cos(a_im * dt), mag * jnp.sin(a_im * dt)
    den = a_re * a_re + a_im * a_im
    fr = ((abr - 1.0) * a_re + abi * a_im) / den
    fi = (abi * a_re - (abr - 1.0) * a_im) / den
    bbr = fr[..., None] * b_re - fi[..., None] * b_im
    bbi = fr[..., None] * b_im + fi[..., None] * b_re
    n = jnp.arange(ln + 1, dtype=F32)[:, None, None, None]
    pmag = jnp.exp(a_re * dt * n)
    pr, pi = pmag * jnp.cos(a_im * dt * n), pmag * jnp.sin(a_im * dt * n)
    cr = c_re[None] * pr[:, :, :, None, :] - c_im[None] * pi[:, :, :, None, :]
    ci = c_re[None] * pi[:, :, :, None, :] + c_im[None] * pr[:, :, :, None, :]
    m = (jnp.einsum('ndgcp,dgpe->ndgce', cr[:ln], bbr, precision=hi)
         - jnp.einsum('ndgcp,dgpe->ndgce', ci[:ln], bbi, precision=hi))
    lag = jnp.arange(ln)[None, :] - jnp.arange(ln)[:, None]
    n_idx = jnp.arange(ln)[:, None, None]
    oh_f = (lag[None] == n_idx).astype(F32)
    oh_b = (-lag[None] == n_idx).astype(F32)
    tsum = (jnp.einsum('nsi,ngce->gseic', oh_f, m[:, 0], precision=hi)
            + jnp.einsum('nsi,ngce->gseic', oh_b, m[:, 1], precision=hi))
    g = a_re.shape[1]
    eye_s = jnp.eye(ln, dtype=F32)[None, :, None, :, None]
    eye_c = jnp.eye(S5_GROUP, dtype=F32)[None, None, :, None, :]
    tsum = tsum + eye_s * eye_c * d_skip.reshape(g, 1, S5_GROUP, 1, 1)
    tsum = tsum.reshape(g, ln * S5_GROUP, ln * S5_GROUP)

    def w_in(pw_r, pw_i, d):
        re = pw_r[..., None] * bbr[d][None] - pw_i[..., None] * bbi[d][None]
        im = pw_r[..., None] * bbi[d][None] + pw_i[..., None] * bbr[d][None]
        re = re.transpose(1, 0, 3, 2).reshape(g, ln * S5_GROUP, S5_STATE)
        im = im.transpose(1, 0, 3, 2).reshape(g, ln * S5_GROUP, S5_STATE)
        return jnp.concatenate([re, im, im, re], axis=-1)

    win = jnp.concatenate([w_in(pr[:ln, 0][::-1], pi[:ln, 0][::-1], 0),
                           w_in(pr[:ln, 1], pi[:ln, 1], 1)], axis=-1)

    def w_out(cr_d, ci_d):
        re = cr_d.transpose(1, 3, 0, 2).reshape(g, S5_STATE, ln * S5_GROUP)
        im = (-ci_d).transpose(1, 3, 0, 2).reshape(g, S5_STATE, ln * S5_GROUP)
        return jnp.concatenate([re, im], axis=1)

    wout = jnp.concatenate([w_out(cr[1:ln + 1, 0], ci[1:ln + 1, 0]),
                            w_out(cr[1:ln + 1, 1][::-1], ci[1:ln + 1, 1][::-1])], axis=1)
    def cpow(k):
        e = (jnp.asarray(k, F32) * ln)[..., None, None, None]
        mg = jnp.exp(a_re * dt * e)
        return mg * jnp.cos(a_im * dt * e), mg * jnp.sin(a_im * dt * e)

    mr, mi = cpow(jnp.array([1, 2, 4, 8]))
    rows = []
    for d in range(2):
        for i in range(4):
            r_, i_ = mr[i, d], mi[i, d]
            rows += [jnp.concatenate([r_, r_], -1), jnp.concatenate([-i_, i_], -1), jnp.concatenate([i_, -i_], -1)]
        rows += [jnp.zeros_like(rows[0])] * 4
        j = jnp.arange(8)
        qr, qi = cpow(j if d == 0 else 7 - j)
        rows += [jnp.concatenate([qr[k, d], qr[k, d]], -1) for k in range(8)]
        rows += [jnp.concatenate([-qi[k, d], qi[k, d]], -1) for k in range(8)]
    lam = jnp.stack(rows, axis=1)
    return win.astype(BF16), tsum.astype(BF16), wout.astype(BF16), lam


def _s5_in_kernel(u_ref, w_ref, o_ref):
    o_ref[...] = jnp.dot(u_ref[...], w_ref[...], preferred_element_type=F32)


def _s5_in_proj(u, w_s5):
    b, t, d = u.shape
    tm = _big_row_tile(t)
    return pl.pallas_call(
        _s5_in_kernel,
        out_shape=jax.ShapeDtypeStruct((b, t, S5_W), F32),
        grid=(b, t // tm),
        in_specs=[pl.BlockSpec((None, tm, d), lambda bi, i: (bi, i, 0)),
                  pl.BlockSpec((d, S5_W), lambda bi, i: (0, 0))],
        out_specs=pl.BlockSpec((None, tm, S5_W), lambda bi, i: (bi, i, 0)),
        compiler_params=_params("parallel", "arbitrary"),
        name="s5_in_proj",
    )(u, w_s5)


def _s5(u5, s5w, n_ctx):
    b, t, _ = u5.shape
    win, tsum, wout, lam = s5w
    gpb = LANES // S5_GROUP
    nc = t // S5_CHUNK
    kw = S5_CHUNK * S5_GROUP
    assert (n_ctx // S5_CHUNK) % 8 == 0 and nc % 8 == 0
    kern = functools.partial(_s5_kernel, nc=nc, ncc=n_ctx // S5_CHUNK)
    wspec = lambda r, c: pl.BlockSpec((gpb, r, c), lambda bi, gb: (gb, 0, 0))
    return pl.pallas_call(
        kern,
        out_shape=jax.ShapeDtypeStruct((b, t, S5_W), F32),
        grid=(b, S5_W // LANES),
        in_specs=[pl.BlockSpec((None, t, LANES), lambda bi, gb: (bi, 0, gb)),
                  wspec(kw, 8 * S5_STATE), wspec(kw, kw), wspec(4 * S5_STATE, kw), wspec(64, 2 * S5_STATE)],
        out_specs=pl.BlockSpec((None, t, LANES), lambda bi, gb: (bi, 0, gb)),
        scratch_shapes=[pltpu.VMEM((gpb, nc, kw), BF16),
                        pltpu.VMEM((gpb, nc, 8 * S5_STATE), F32),
                        pltpu.VMEM((gpb, nc, 4 * S5_STATE), F32)],
        compiler_params=_params("parallel", "parallel"),
        name="s5_scan",
    )(u5, win, tsum, wout, lam)


def _glu_kernel(s_ref, u_ref, wg_ref, w_ref, b_ref, o_ref):
    s = jax.nn.gelu(s_ref[...])
    z = jnp.dot(s.astype(BF16), w_ref[...], preferred_element_type=F32) + b_ref[...]
    g = jnp.dot(u_ref[...], wg_ref[...], preferred_element_type=F32)
    o_ref[...] = (s * jax.nn.sigmoid(z) * (g * jax.nn.sigmoid(g))).astype(BF16)


def _s5_glu_gate(s_pre, u, w_s5, glu_w, glu_b):
    b, t, d = u.shape
    tm = _big_row_tile(t)
    return pl.pallas_call(
        _glu_kernel,
        out_shape=jax.ShapeDtypeStruct((b, t, S5_W), BF16),
        grid=(b, t // tm),
        in_specs=[pl.BlockSpec((None, tm, S5_W), lambda bi, i: (bi, i, 0)),
                  pl.BlockSpec((None, tm, d), lambda bi, i: (bi, i, 0)),
                  pl.BlockSpec((d, S5_W), lambda bi, i: (0, 1)),
                  pl.BlockSpec((S5_W, S5_W), lambda bi, i: (0, 0)),
                  pl.BlockSpec((1, S5_W), lambda bi, i: (0, 0))],
        out_specs=pl.BlockSpec((None, tm, S5_W), lambda bi, i: (bi, i, 0)),
        compiler_params=_params("parallel", "arbitrary"),
        name="s5_glu_gate",
    )(s_pre, u, w_s5, glu_w, glu_b)


def _gate_kernel(u_ref, w_ref, r_ref, a_ref, gn_ref, o_ref):
    j = pl.program_id(2)
    g = jnp.dot(u_ref[...], w_ref[...], preferred_element_type=F32)
    g = g * jax.nn.sigmoid(g)

    @pl.when(j < 2)
    def _():
        o_ref[...] = (r_ref[...].astype(F32) * gn_ref[...] * g).astype(BF16)

    @pl.when(j >= 2)
    def _():
        o_ref[...] = (a_ref[...].astype(F32) * g).astype(BF16)


def _branch_gates(u, wb, r, a, gn_w):
    b, t, d = u.shape
    tm = _big_row_tile(t)
    ret_g0 = 3 * RET_W // COL_TILE
    att_g0 = (4 * RET_W + ATT_W + 2 * ATT_KV_W) // COL_TILE
    act = lambda off: pl.BlockSpec((None, tm, COL_TILE), lambda bi, i, j: (bi, i, jnp.clip(j - off, 0, 1)))
    return pl.pallas_call(
        _gate_kernel,
        out_shape=jax.ShapeDtypeStruct((b, t, RET_W + ATT_W), BF16),
        grid=(b, t // tm, (RET_W + ATT_W) // COL_TILE),
        in_specs=[pl.BlockSpec((None, tm, d), lambda bi, i, j: (bi, i, 0)),
                  pl.BlockSpec((d, COL_TILE), lambda bi, i, j: (0, jnp.where(j < 2, ret_g0 + j, att_g0 + j - 2))),
                  act(0), act(2),
                  pl.BlockSpec((1, COL_TILE), lambda bi, i, j: (0, jnp.clip(j, 0, 1)))],
        out_specs=pl.BlockSpec((None, tm, COL_TILE), lambda bi, i, j: (bi, i, j)),
        compiler_params=_params("parallel", "parallel", "arbitrary"),
        name="branch_gates",
    )(u, wb, r, a, gn_w)


def _merge_kernel(u_ref, zra_ref, zs_ref, wm0_ref, wm1_ref, wm2_ref, wr_ref, wa_ref, ws_ref, o_ref):
    u = u_ref[...]
    acts = (zra_ref[:, :RET_W], zra_ref[:, RET_W:], zs_ref[...])
    acc = None
    for act, wm_ref, wbr_ref in zip(acts, (wm0_ref, wm1_ref, wm2_ref), (wr_ref, wa_ref, ws_ref)):
        gate = jax.nn.sigmoid(jnp.dot(u, wm_ref[...], preferred_element_type=F32))
        proj = jnp.dot(act, wbr_ref[...], preferred_element_type=F32)
        acc = gate * proj if acc is None else acc + gate * proj
    o_ref[...] = acc.astype(BF16)


def _merge(u, z_ra, z_s, wb, w_ret, w_att, w_s5):
    b, t, d = u.shape
    tm = 528 if t % 528 == 0 else _big_row_tile(t)
    m0 = (wb.shape[1] - N_BRANCH * d) // COL_TILE
    per = d // COL_TILE
    wm = lambda br: pl.BlockSpec((d, COL_TILE), lambda bi, i, j: (0, m0 + br * per + j))
    wbr = lambda k: pl.BlockSpec((k, COL_TILE), lambda bi, i, j: (0, j))
    return pl.pallas_call(
        _merge_kernel,
        out_shape=jax.ShapeDtypeStruct((b, t, d), BF16),
        grid=(b, t // tm, d // COL_TILE),
        in_specs=[pl.BlockSpec((None, tm, d), lambda bi, i, j: (bi, i, 0)),
                  pl.BlockSpec((None, tm, RET_W + ATT_W), lambda bi, i, j: (bi, i, 0)),
                  pl.BlockSpec((None, tm, S5_W), lambda bi, i, j: (bi, i, 0)),
                  wm(0), wm(1), wm(2), wbr(RET_W), wbr(ATT_W), wbr(S5_W)],
        out_specs=pl.BlockSpec((None, tm, COL_TILE), lambda bi, i, j: (bi, i, j)),
        compiler_params=_params("parallel", "parallel", "arbitrary"),
        name="gated_merge",
    )(u, z_ra, z_s, wb, wb, wb, w_ret, w_att, w_s5)


def _out_kernel(m_ref, w_ref, x_ref, mod_ref, lnw_ref, lnb_ref, o_ref, *, alpha):
    y = jnp.dot(m_ref[...], w_ref[...], preferred_element_type=F32)
    v = alpha * x_ref[...] + mod_ref[2:3, :] * y
    mu = jnp.mean(v, axis=-1, keepdims=True)
    vc = v - mu
    var = jnp.mean(vc * vc, axis=-1, keepdims=True)
    o_ref[...] = vc * lax.rsqrt(var + LN_EPS) * lnw_ref[...] + lnb_ref[...]


def _out_proj(m, w_out, xa, modsel, ln_w, ln_b, alpha, skip_ctx):
    b, t, d = xa.shape
    off = 1 if skip_ctx else 0
    nt = t // ROW_TILE - off
    kern = functools.partial(_out_kernel, alpha=alpha)
    row = lambda: pl.BlockSpec((None, ROW_TILE, d), lambda bi, i: (bi, i + off, 0))
    vec = lambda: pl.BlockSpec((1, d), lambda bi, i: (0, 0))
    return pl.pallas_call(
        kern,
        out_shape=jax.ShapeDtypeStruct((b, nt * ROW_TILE, d), F32),
        grid=(b, nt),
        in_specs=[row(), pl.BlockSpec((d, d), lambda bi, i: (0, 0)), row(),
                  pl.BlockSpec((None, None, 3, d), lambda bi, i: (bi, jnp.minimum(i + off, 1), 0, 0)),
                  vec(), vec()],
        out_specs=pl.BlockSpec((None, ROW_TILE, d), lambda bi, i: (bi, i, 0)),
        compiler_params=_params("parallel", "arbitrary"),
        name="out_proj_norm",
    )(m, w_out, xa, modsel, ln_w.reshape(1, d), ln_b.reshape(1, d))


def _rope_tables(n, n_ctx):
    rows = n // GRID_W
    row = jnp.repeat(jnp.arange(rows, dtype=F32), GRID_W)
    col = jnp.tile(jnp.arange(GRID_W, dtype=F32), rows)

    def table(head_dim):
        per_axis = head_dim // 4
        inv = ROPE_THETA ** (-jnp.arange(per_axis, dtype=F32) / per_axis)
        ang = jnp.concatenate([row[:, None] * inv, col[:, None] * inv], axis=-1)
        cos = jnp.concatenate([jnp.ones((n_ctx, head_dim // 2), F32), jnp.cos(ang)], axis=0)
        sin = jnp.concatenate([jnp.zeros((n_ctx, head_dim // 2), F32), jnp.sin(ang)], axis=0)
        return cos, sin

    cos_r, sin_r = table(RET_DK)
    cos_a, sin_a = table(ATT_HD)
    return (cos_r, sin_r, jnp.concatenate([cos_a, cos_a], -1), jnp.concatenate([-sin_a, sin_a], -1))


def kernel(x, c, ctx, c_ctx, ada_w, ada_b, w_in, ret_log_decay, ret_gn_w, att_q_norm, att_k_norm, s5_a_re, s5_a_im, s5_log_dt, s5_b_re, s5_b_im, s5_c_re, s5_c_im, s5_d, s5_glu_w, s5_glu_b, w_br_ret, w_br_att, w_br_s5, w_out, ln_w, ln_b):
    b, n, d = x.shape
    n_ctx = ctx.shape[1]
    depth = w_in.shape[0]
    assert n_ctx == ROW_TILE and n % ROW_TILE == 0 and n % GRID_W == 0
    assert w_in.shape[2] == 4 * RET_W + 2 * ATT_W + 2 * ATT_KV_W + 2 * S5_W + N_BRANCH * d
    alpha = (2.0 * depth) ** 0.25
    s5_col0 = 4 * RET_W + 2 * ATT_W + 2 * ATT_KV_W

    xa = jnp.concatenate([ctx, x], axis=1)
    cos_r, sin_r, cos_a, sin_a = _rope_tables(n, n_ctx)
    crows = 16
    cvec = jnp.concatenate([c, c_ctx[None, :], jnp.zeros((crows - b - 1, d), F32)], axis=0)

    for l in range(depth):
        last = l == depth - 1
        mod = _modulation(cvec, ada_w[l], ada_b[l]).reshape(crows, 3, d)
        modsel = jnp.stack([jnp.broadcast_to(mod[b][None], (b, 3, d)), mod[:b]], axis=1)
        wb = w_in[l].astype(BF16)
        w_s5 = wb[:, s5_col0:s5_col0 + 2 * S5_W]
        s5w = _s5_weights(s5_a_re[l], s5_a_im[l], s5_log_dt[l], s5_b_re[l], s5_b_im[l],
                          s5_c_re[l], s5_c_im[l], s5_d[l])

        u = _modulate(xa, modsel)
        mix = _mix_proj(u, wb, cos_r, sin_r, cos_a, sin_a,
                        att_q_norm[l].reshape(1, ATT_HD), att_k_norm[l].reshape(1, ATT_HD))
        r = _retention(mix, ret_log_decay[l], n_ctx)
        a = _attention(mix, n_ctx)
        s_pre = _s5(_s5_in_proj(u, w_s5), s5w, n_ctx)
        z_s = _s5_glu_gate(s_pre, u, w_s5, s5_glu_w[l].astype(BF16), s5_glu_b[l].reshape(1, S5_W))
        z_ra = _branch_gates(u, wb, r, a, ret_gn_w[l].reshape(1, RET_W))
        m = _merge(u, z_ra, z_s, wb, w_br_ret[l].astype(BF16), w_br_att[l].astype(BF16), w_br_s5[l].astype(BF16))
        xa = _out_proj(m, w_out[l].astype(BF16), xa, modsel, ln_w[l], ln_b[l], alpha, last)
    return xa
```

```python
import functools
import math

import jax
import jax.numpy as jnp
from jax import lax
from jax.experimental import pallas as pl
from jax.experimental.pallas import tpu as pltpu

F32 = jnp.float32
BF16 = jnp.bfloat16

GRID_W = 64
RET_HEADS = 4
RET_DK = 256
RET_DV = 256
RET_W = RET_HEADS * RET_DV
ATT_HEADS = 8
ATT_KV_HEADS = 2
ATT_GROUP = ATT_HEADS // ATT_KV_HEADS
ATT_HD = 128
ATT_W = ATT_HEADS * ATT_HD
ATT_KV_W = ATT_KV_HEADS * ATT_HD
ROPE_THETA = 10000.0
S5_GROUP = 16
S5_W = 768
S5_GROUPS = S5_W // S5_GROUP
S5_STATE = 64
N_BRANCH = 3
LN_EPS = 1e-6
RMS_EPS = 1e-6

LANES = 128
VMEM_LIMIT_BYTES = 56 * 1024 * 1024

CHUNK = 128
S5_CHUNK = 16
ROW_TILE = 256
COL_TILE = 512
MIX_W = 4608
ATT_SUB = 1024
LOG2E = math.log2(math.e)


def _params(*sem):
    return pltpu.CompilerParams(dimension_semantics=sem, vmem_limit_bytes=VMEM_LIMIT_BYTES)


def _big_row_tile(t):
    for cand in (1056, 1024, 768, 640, 512, 256):
        if t % cand == 0:
            return cand
    raise ValueError(f"unsupported token count {t}")


def _mod_kernel(c_ref, w_ref, b_ref, o_ref):
    c = c_ref[...]
    s = c * jax.nn.sigmoid(c)
    s_hi = s.astype(BF16)
    s_lo = (s - s_hi.astype(F32)).astype(BF16)
    w = w_ref[...]
    w_hi = w.astype(BF16)
    w_lo = (w - w_hi.astype(F32)).astype(BF16)
    acc = jnp.dot(s_hi, w_hi, preferred_element_type=F32)
    acc += jnp.dot(s_lo, w_hi, preferred_element_type=F32)
    acc += jnp.dot(s_hi, w_lo, preferred_element_type=F32)
    o_ref[...] = acc + b_ref[...]


def _modulation(cvec, ada_w, ada_b, layer):
    rows, d = cvec.shape
    n = ada_w.shape[2]
    tn = 768
    return pl.pallas_call(
        _mod_kernel,
        out_shape=jax.ShapeDtypeStruct((rows, n), F32),
        grid=(n // tn,),
        in_specs=[pl.BlockSpec((rows, d), lambda j: (0, 0)),
                  pl.BlockSpec((None, d, tn), lambda j: (layer, 0, j)),
                  pl.BlockSpec((None, 1, tn), lambda j: (layer, 0, j))],
        out_specs=pl.BlockSpec((rows, tn), lambda j: (0, j)),
        compiler_params=_params("arbitrary"),
        name="adaln_modulation",
    )(cvec, ada_w, ada_b.reshape(ada_b.shape[0], 1, n))


def _modulate_kernel(c_ref, x_ref, mod_ref, o_ref):
    i = pl.program_id(1)
    m = mod_ref[...]

    @pl.when(i == 0)
    def _():
        o_ref[...] = (c_ref[...] * (1.0 + m[1:2]) + m[0:1]).astype(BF16)

    @pl.when(i > 0)
    def _():
        o_ref[...] = (x_ref[...] * (1.0 + m[1:2]) + m[0:1]).astype(BF16)


def _token_specs(ctx_src, lat_src, first_tile=0):
    d = ctx_src.shape[2]
    shift = first_tile - (0 if lat_src is ctx_src else 1)
    return (pl.BlockSpec((None, ROW_TILE, d), lambda bi, i: (bi, 0, 0)),
            pl.BlockSpec((None, ROW_TILE, d), lambda bi, i: (bi, jnp.maximum(i + shift, 0), 0)))


def _modulate(ctx_src, lat_src, t, modsel):
    b, _, d = ctx_src.shape
    return pl.pallas_call(
        _modulate_kernel,
        out_shape=jax.ShapeDtypeStruct((b, t, d), BF16),
        grid=(b, t // ROW_TILE),
        in_specs=[*_token_specs(ctx_src, lat_src),
                  pl.BlockSpec((None, None, 3, d), lambda bi, i: (bi, jnp.minimum(i, 1), 0, 0))],
        out_specs=pl.BlockSpec((None, ROW_TILE, d), lambda bi, i: (bi, i, 0)),
        compiler_params=_params("parallel", "arbitrary"),
        name="modulate",
    )(ctx_src, lat_src, modsel)


def _mix_kernel(u_ref, w_ref, cr_ref, sr_ref, ca_ref, sa_ref, qn_ref, kn_ref, o_ref):
    j = pl.program_id(2)
    tm = u_ref.shape[0]
    nsplit = 2 if tm % 32 == 0 else 1
    hr = tm // nsplit

    def halves(epilogue):
        for r in range(nsplit):
            rows = slice(r * hr, (r + 1) * hr)
            epilogue(rows, jnp.dot(u_ref[rows, :], w_ref[...], preferred_element_type=F32))

    def rms_rope(x, rows, w, scale):
        ms = jnp.mean(x * x, axis=-1, keepdims=True)
        y = x * lax.rsqrt(ms + RMS_EPS) * w
        y = y * ca_ref[rows, :] + pltpu.roll(y, ATT_HD // 2, 1) * sa_ref[rows, :]
        return (y * scale).astype(BF16)

    @pl.when(j < 4)
    def _():
        scale = jnp.where(j >= 2, RET_DK ** -0.5, 1.0).astype(F32)

        def epilogue(rows, acc):
            cos = cr_ref[rows, :]
            sin = sr_ref[rows, :]
            for h in range(COL_TILE // RET_DK):
                x1 = acc[:, h * RET_DK: h * RET_DK + LANES]
                x2 = acc[:, h * RET_DK + LANES: (h + 1) * RET_DK]
                o_ref[rows, h * RET_DK: h * RET_DK + LANES] = ((x1 * cos - x2 * sin) * scale).astype(BF16)
                o_ref[rows, h * RET_DK + LANES: (h + 1) * RET_DK] = ((x2 * cos + x1 * sin) * scale).astype(BF16)

        halves(epilogue)

    @pl.when((j == 4) | (j == 5))
    def _():
        def epilogue(rows, acc):
            o_ref[rows, :] = acc.astype(BF16)

        halves(epilogue)

    @pl.when((j == 6) | (j == 7))
    def _():
        def epilogue(rows, acc):
            for h in range(COL_TILE // ATT_HD):
                o_ref[rows, h * ATT_HD:(h + 1) * ATT_HD] = rms_rope(
                    acc[:, h * ATT_HD:(h + 1) * ATT_HD], rows, qn_ref[...], ATT_HD ** -0.5 * LOG2E)

        halves(epilogue)

    @pl.when(j == 8)
    def _():
        def epilogue(rows, acc):
            for h in range(ATT_KV_HEADS):
                o_ref[rows, h * ATT_HD:(h + 1) * ATT_HD] = rms_rope(
                    acc[:, h * ATT_HD:(h + 1) * ATT_HD], rows, kn_ref[...], 1.0)
            o_ref[rows, ATT_KV_W:] = acc[:, ATT_KV_W:].astype(BF16)

        halves(epilogue)


def _mix_proj(u, wb, layer, cos_r, sin_r, cos_a, sin_a, qn, kn):
    b, t, d = u.shape
    tm = _big_row_tile(t)
    tab = lambda: pl.BlockSpec((tm, LANES), lambda bi, i, j: (i, 0))
    vec = lambda: pl.BlockSpec((1, LANES), lambda bi, i, j: (0, 0))
    gap = RET_W // COL_TILE
    return pl.pallas_call(
        _mix_kernel,
        out_shape=jax.ShapeDtypeStruct((b, t, MIX_W), BF16),
        grid=(b, t // tm, MIX_W // COL_TILE),
        in_specs=[pl.BlockSpec((None, tm, d), lambda bi, i, j: (bi, i, 0)),
                  pl.BlockSpec((None, d, COL_TILE), lambda bi, i, j: (layer, 0, jnp.where(j >= 6, j + gap, j))),
                  tab(), tab(), tab(), tab(), vec(), vec()],
        out_specs=pl.BlockSpec((None, tm, COL_TILE), lambda bi, i, j: (bi, i, j)),
        compiler_params=_params("parallel", "parallel", "arbitrary"),
        name="mixer_in_proj",
    )(u, wb, cos_r, sin_r, cos_a, sin_a, qn, kn)


def _ret_kernel(ld_ref, q_ref, k_ref, v_ref, o_ref, ob_ref, s_ref, *, n_ctx_chunks, n_chunks):
    h = pl.program_id(1)
    lgf = ld_ref[0, h]
    lgb = ld_ref[1, h]
    c = CHUNK
    ri = lax.broadcasted_iota(jnp.int32, (c, c), 0).astype(F32)
    ci = lax.broadcasted_iota(jnp.int32, (c, c), 1).astype(F32)
    rel = ri - ci
    mask = jnp.where(rel >= 0, jnp.exp(lgf * jnp.maximum(rel, 0.0)), jnp.exp(lgb * jnp.maximum(-rel, 0.0)))
    row = lax.broadcasted_iota(jnp.int32, (c, RET_DV), 0).astype(F32)
    qdec_f = jnp.exp(lgf * (row + 1.0))
    qdec_b = jnp.exp(lgb * (c - row))
    kdec_f = jnp.exp(lgf * (c - 1.0 - row))
    kdec_b = jnp.exp(lgb * row)
    cdec_f = jnp.exp(jnp.full((1, RET_DV), lgf * c, F32))
    cdec_b = jnp.exp(jnp.full((1, RET_DV), lgb * c, F32))
    contract0 = (((0,), (0,)), ((), ()))
    contract1 = (((1,), (1,)), ((), ()))

    def load(ci_):
        r0 = pl.multiple_of(ci_ * c, c)
        rows = pl.ds(r0, c)
        return rows, q_ref[rows, :], k_ref[rows, :], v_ref[rows, :]

    def bwd(ci_):
        rows, q, k, v = load(ci_)
        s = s_ref[...]
        ob_ref[rows, :] = jnp.dot(q, s.astype(BF16), preferred_element_type=F32) * qdec_b
        kd = (k.astype(F32) * kdec_b).astype(BF16)
        s_ref[...] = s * cdec_b + lax.dot_general(kd, v, contract0, preferred_element_type=F32)

    def fwd(ci_):
        rows, q, k, v = load(ci_)
        s = s_ref[...]
        sc = lax.dot_general(q, k, contract1, preferred_element_type=F32) * mask
        o = jnp.dot(sc.astype(BF16), v, preferred_element_type=F32)
        o += jnp.dot(q, s.astype(BF16), preferred_element_type=F32) * qdec_f
        o += ob_ref[rows, :]
        mu = jnp.mean(o, axis=-1, keepdims=True)
        oc = o - mu
        var = jnp.mean(oc * oc, axis=-1, keepdims=True)
        o_ref[rows, :] = (oc * lax.rsqrt(var + LN_EPS)).astype(BF16)
        kd = (k.astype(F32) * kdec_f).astype(BF16)
        s_ref[...] = s * cdec_f + lax.dot_general(kd, v, contract0, preferred_element_type=F32)

    s_ref[...] = jnp.zeros_like(s_ref)

    def bwd_ctx(t, carry):
        bwd(n_ctx_chunks - 1 - t)
        return carry

    def bwd_lat(t, carry):
        bwd(n_chunks - 1 - t)
        return carry

    lax.fori_loop(0, n_ctx_chunks, bwd_ctx, 0, unroll=2)
    lax.fori_loop(0, n_chunks - n_ctx_chunks, bwd_lat, 0, unroll=2)

    s_ref[...] = jnp.zeros_like(s_ref)

    def fwd_all(t, carry):
        fwd(t)
        return carry

    lax.fori_loop(0, n_chunks, fwd_all, 0, unroll=2)


def _retention(mix, log_decay, n_ctx):
    b, t, _ = mix.shape
    kern = functools.partial(_ret_kernel, n_ctx_chunks=n_ctx // CHUNK, n_chunks=t // CHUNK)
    blk = lambda off: pl.BlockSpec((None, t, RET_DK), lambda bi, h: (bi, 0, off + h))
    return pl.pallas_call(
        kern,
        out_shape=jax.ShapeDtypeStruct((b, t, RET_W), BF16),
        grid=(b, RET_HEADS),
        in_specs=[pl.BlockSpec(memory_space=pltpu.SMEM), blk(0), blk(RET_HEADS), blk(2 * RET_HEADS)],
        out_specs=pl.BlockSpec((None, t, RET_DV), lambda bi, h: (bi, 0, h)),
        scratch_shapes=[pltpu.VMEM((t, RET_DV), F32), pltpu.VMEM((RET_DK, RET_DV), F32)],
        compiler_params=_params("parallel", "arbitrary"),
        name="retention",
    )(log_decay, mix, mix, mix)


def _att_kernel(q_ref, k_ref, v_ref, o_ref, v1_ref, q4_ref, s_ref, mx_ref, m_ref, acc_ref, *, chunks, n_ctx, t):
    i = pl.program_id(2)
    contract1 = (((1,), (1,)), ((), ()))
    tq = q_ref.shape[0]
    sb = ATT_SUB

    @pl.when(i == 0)
    def _():
        v1_ref[:, :ATT_HD] = v_ref[...]
        v1_ref[:, ATT_HD:] = jnp.ones((t, ATT_HD), BF16)

    for g in range(ATT_GROUP):
        q4_ref[g * tq:(g + 1) * tq, :] = q_ref[:, g * ATT_HD:(g + 1) * ATT_HD]
    neg = jnp.full(mx_ref.shape, -jnp.inf, F32)
    mx_ref[...] = neg
    m_ref[...] = neg
    acc_ref[...] = jnp.zeros(acc_ref.shape, F32)

    def a_step(buf, j, r0, size):
        k = k_ref[pl.ds(r0, size), :]
        s = lax.dot_general(q4_ref[...], k, contract1, preferred_element_type=F32)
        s_ref[buf, j, :, 0:size] = s
        mx = mx_ref[...]
        for tt in range(size // LANES):
            mx = jnp.maximum(mx, s[:, tt * LANES:(tt + 1) * LANES])
        mx_ref[...] = mx

    def settle():
        mx = mx_ref[...]
        mc = jnp.broadcast_to(jnp.max(mx, axis=-1, keepdims=True), mx.shape)
        m_old = m_ref[...]
        m_new = jnp.maximum(m_old, mc)
        alpha = jnp.exp2(m_old - m_new)
        m_ref[...] = m_new
        acc_ref[:, :ATT_HD] = acc_ref[:, :ATT_HD] * alpha
        acc_ref[:, ATT_HD:] = acc_ref[:, ATT_HD:] * alpha
        mx_ref[...] = neg

    def b_step(buf, j, r0, size):
        mb = m_ref[...]
        p = jnp.concatenate([jnp.exp2(s_ref[buf, j, :, tt * LANES:(tt + 1) * LANES] - mb)
                             for tt in range(size // LANES)], axis=1).astype(BF16)
        acc_ref[...] += jnp.dot(p, v1_ref[pl.ds(r0, size), :], preferred_element_type=F32)

    def stage(ca, cb):
        sides = [(a_step, ca), (b_step, cb)]
        sides = [(fn, ch) for fn, ch in sides if ch is not None]
        common = min(ch[2] // sb for _, ch in sides)

        def body(j, carry):
            off = pl.multiple_of(j * sb, sb)
            for fn, (buf, start, _) in sides:
                fn(buf, j, start + off, sb)
            return carry

        if common:
            lax.fori_loop(0, common, body, 0)
        for fn, (buf, start, size) in sides:
            for j in range(common, -(-size // sb)):
                fn(buf, j, start + j * sb, min(sb, size - j * sb))
        if ca is not None:
            settle()

    @pl.when(i == 0)
    def _():
        stage((0, 0, n_ctx), None)
        stage(None, (0, 0, n_ctx))

    @pl.when(i > 0)
    def _():
        bufs = [(c % 2, start, size) for c, (start, size) in enumerate(chunks)]
        for c in range(len(bufs) + 1):
            stage(bufs[c] if c < len(bufs) else None, bufs[c - 1] if c > 0 else None)

    for g in range(ATT_GROUP):
        rows = slice(g * tq, (g + 1) * tq)
        o_ref[:, g * ATT_HD:(g + 1) * ATT_HD] = (acc_ref[rows, :ATT_HD] / acc_ref[rows, ATT_HD:]).astype(BF16)


def _attention(mix, n_ctx):
    b, t, _ = mix.shape
    tq = ROW_TILE
    n_lat = t - n_ctx
    nch = 4 if n_lat % (4 * COL_TILE) == 0 else 1
    chunks = tuple((0, n_ctx + n_lat // nch) if c == 0 else (n_ctx + c * (n_lat // nch), n_lat // nch)
                   for c in range(nch))
    mrows = ATT_GROUP * tq
    q_blk0 = (3 * RET_W) // (ATT_GROUP * ATT_HD)
    k_blk0 = (3 * RET_W + ATT_W) // ATT_HD
    v_blk0 = k_blk0 + ATT_KV_HEADS
    kern = functools.partial(_att_kernel, chunks=chunks, n_ctx=n_ctx, t=t)
    return pl.pallas_call(
        kern,
        out_shape=jax.ShapeDtypeStruct((b, t, ATT_W), BF16),
        grid=(b, ATT_KV_HEADS, t // tq),
        in_specs=[pl.BlockSpec((None, tq, ATT_GROUP * ATT_HD), lambda bi, hk, i: (bi, i, q_blk0 + hk)),
                  pl.BlockSpec((None, t, ATT_HD), lambda bi, hk, i: (bi, 0, k_blk0 + hk)),
                  pl.BlockSpec((None, t, ATT_HD), lambda bi, hk, i: (bi, 0, v_blk0 + hk))],
        out_specs=pl.BlockSpec((None, tq, ATT_GROUP * ATT_HD), lambda bi, hk, i: (bi, i, hk)),
        scratch_shapes=[pltpu.VMEM((t, 2 * ATT_HD), BF16),
                        pltpu.VMEM((mrows, ATT_HD), BF16),
                        pltpu.VMEM((2, -(-chunks[0][1] // ATT_SUB), mrows, ATT_SUB), F32),
                        pltpu.VMEM((mrows, LANES), F32),
                        pltpu.VMEM((mrows, LANES), F32),
                        pltpu.VMEM((mrows, 2 * ATT_HD), F32)],
        compiler_params=_params("parallel", "parallel", "arbitrary"),
        name="gqa_attention",
    )(mix, mix, mix)


def _s5_kernel(u_ref, win_ref, tsum_ref, wout_ref, lam_ref, y_ref, uc_ref, d_ref, xin_ref, *, nc, ncc):
    p2 = 2 * S5_STATE
    sub = 8
    ln = S5_CHUNK
    gpb = LANES // S5_GROUP
    us = [u_ref[pl.ds(s, nc, stride=ln), :] for s in range(ln)]
    for g in range(gpb):
        uc = jnp.concatenate([us[s][:, g * S5_GROUP:(g + 1) * S5_GROUP] for s in range(ln)], axis=1).astype(BF16)
        uc_ref[g] = uc
        d_ref[g] = jnp.dot(uc, win_ref[g], preferred_element_type=F32)
    rowid = lax.broadcasted_iota(jnp.int32, (sub, p2), 0)

    def tile(g, reverse, rows, c, cs):
        base = 32 if reverse else 0
        col = 2 * p2 if reverse else 0

        def mult(i):
            return tuple(jnp.broadcast_to(lam_ref[g, base + 3 * i + r:base + 3 * i + r + 1, :], (sub, p2))
                         for r in range(3))

        def shifted(v, n):
            if reverse:
                return jnp.where(rowid < sub - n, pltpu.roll(v, sub - n, 0), 0.0)
            return jnp.where(rowid >= n, pltpu.roll(v, n, 0), 0.0)

        x = d_ref[g, rows, col:col + p2]
        xs = d_ref[g, rows, col + p2:col + 2 * p2]
        for i, n in enumerate((1, 2, 4)):
            a, b, bs = mult(i)
            xr, xsr = shifted(x, n), shifted(xs, n)
            x, xs = x + a * xr + b * xsr, xs + a * xsr + bs * xr
        pa = lam_ref[g, base + 16:base + 24, :]
        pb = lam_ref[g, base + 24:base + 32, :]
        xin_ref[g, rows, col // 2:col // 2 + p2] = shifted(x, 1) + pa * c + pb * cs
        edge = 0 if reverse else sub - 1
        last = jnp.broadcast_to(x[edge:edge + 1, :], (sub, p2))
        lasts = jnp.broadcast_to(xs[edge:edge + 1, :], (sub, p2))
        a, b, bs = mult(3)
        return a * c + b * cs + last, a * cs + bs * c + lasts

    nt = nc // sub
    nct = ncc // sub

    def step(mf, mb, carry):
        rf = pl.ds(pl.multiple_of(mf * sub, sub), sub)
        rb = pl.ds(pl.multiple_of(mb * sub, sub), sub)
        return tuple(tile(g, False, rf, carry[g][0], carry[g][1]) + tile(g, True, rb, carry[g][2], carry[g][3])
                     for g in range(gpb))

    zero = jnp.zeros((sub, p2), F32)
    init = tuple((zero, zero, zero, zero) for _ in range(gpb))
    carry = lax.fori_loop(0, nct, lambda t, cr: step(t, nct - 1 - t, cr), init)
    lax.fori_loop(nct, nt, lambda t, cr: step(t, nt - 1 - (t - nct), cr), carry)

    for g in range(gpb):
        y = jnp.dot(uc_ref[g], tsum_ref[g], preferred_element_type=F32)
        y += jnp.dot(xin_ref[g].astype(BF16), wout_ref[g], preferred_element_type=F32)
        d_ref[g, :, 0:ln * S5_GROUP] = y
    for s in range(ln):
        y_ref[pl.ds(s, nc, stride=ln), :] = jnp.concatenate(
            [d_ref[g, :, s * S5_GROUP:(s + 1) * S5_GROUP] for g in range(gpb)], axis=1)


def _s5_weights(a_re, a_im, log_dt, b_re, b_im, c_re, c_im, d_skip):
    hi = lax.Precision.HIGHEST
    ln = S5_CHUNK
    dt = jnp.exp(log_dt)[..., None]
    mag = jnp.exp(a_re * dt)
    abr, abi = mag * jnp.cos(a_im * dt), mag * jnp.sin(a_im * dt)
    den = a_re * a_re + a_im * a_im
    fr = ((abr - 1.0) * a_re + abi * a_im) / den
    fi = (abi * a_re - (abr - 1.0) * a_im) / den
    bbr = fr[..., None] * b_re - fi[..., None] * b_im
    bbi = fr[..., None] * b_im + fi[..., None] * b_re
    n = jnp.arange(ln + 1, dtype=F32)[:, None, None, None]
    pmag = jnp.exp(a_re * dt * n)
    pr, pi = pmag * jnp.cos(a_im * dt * n), pmag * jnp.sin(a_im * dt * n)
    cr = c_re[None] * pr[:, :, :, None, :] - c_im[None] * pi[:, :, :, None, :]
    ci = c_re[None] * pi[:, :, :, None, :] + c_im[None] * pr[:, :, :, None, :]
    g_ = a_re.shape[1]
    lhs = jnp.concatenate([cr[:ln], -ci[:ln]], axis=-1).transpose(1, 2, 0, 3, 4)
    lhs = lhs.reshape(2, g_, ln * S5_GROUP, 2 * S5_STATE)
    rhs = jnp.concatenate([bbr, bbi], axis=2)
    m = jnp.einsum('dgmk,dgke->dgme', lhs, rhs, precision=hi)
    m = m.reshape(2, g_, ln, S5_GROUP, S5_GROUP).transpose(2, 0, 1, 3, 4)
    lag = jnp.arange(ln)[None, :] - jnp.arange(ln)[:, None]
    n_idx = jnp.arange(ln)[:, None, None]
    oh_f = (lag[None] == n_idx).astype(F32)
    oh_b = (-lag[None] == n_idx).astype(F32)
    tsum = (jnp.einsum('nsi,ngce->gseic', oh_f, m[:, 0], precision=hi)
            + jnp.einsum('nsi,ngce->gseic', oh_b, m[:, 1], precision=hi))
    g = a_re.shape[1]
    eye_s = jnp.eye(ln, dtype=F32)[None, :, None, :, None]
    eye_c = jnp.eye(S5_GROUP, dtype=F32)[None, None, :, None, :]
    tsum = tsum + eye_s * eye_c * d_skip.reshape(g, 1, S5_GROUP, 1, 1)
    tsum = tsum.reshape(g, ln * S5_GROUP, ln * S5_GROUP)

    def w_in(pw_r, pw_i, d):
        re = pw_r[..., None] * bbr[d][None] - pw_i[..., None] * bbi[d][None]
        im = pw_r[..., None] * bbi[d][None] + pw_i[..., None] * bbr[d][None]
        re = re.transpose(1, 0, 3, 2).reshape(g, ln * S5_GROUP, S5_STATE)
        im = im.transpose(1, 0, 3, 2).reshape(g, ln * S5_GROUP, S5_STATE)
        return jnp.concatenate([re, im, im, re], axis=-1)

    win = jnp.concatenate([w_in(pr[:ln, 0][::-1], pi[:ln, 0][::-1], 0),
                           w_in(pr[:ln, 1], pi[:ln, 1], 1)], axis=-1)

    def w_out(cr_d, ci_d):
        re = cr_d.transpose(1, 3, 0, 2).reshape(g, S5_STATE, ln * S5_GROUP)
        im = (-ci_d).transpose(1, 3, 0, 2).reshape(g, S5_STATE, ln * S5_GROUP)
        return jnp.concatenate([re, im], axis=1)

    wout = jnp.concatenate([w_out(cr[1:ln + 1, 0], ci[1:ln + 1, 0]),
                            w_out(cr[1:ln + 1, 1][::-1], ci[1:ln + 1, 1][::-1])], axis=1)
    def cpow(k):
        e = (jnp.asarray(k, F32) * ln)[..., None, None, None]
        mg = jnp.exp(a_re * dt * e)
        return mg * jnp.cos(a_im * dt * e), mg * jnp.sin(a_im * dt * e)

    mr, mi = cpow(jnp.array([1, 2, 4, 8]))
    rows = []
    for d in range(2):
        for i in range(4):
            r_, i_ = mr[i, d], mi[i, d]
            rows += [jnp.concatenate([r_, r_], -1), jnp.concatenate([-i_, i_], -1), jnp.concatenate([i_, -i_], -1)]
        rows += [jnp.zeros_like(rows[0])] * 4
        j = jnp.arange(8)
        qr, qi = cpow(j if d == 0 else 7 - j)
        rows += [jnp.concatenate([qr[k, d], qr[k, d]], -1) for k in range(8)]
        rows += [jnp.concatenate([-qi[k, d], qi[k, d]], -1) for k in range(8)]
    lam = jnp.stack(rows, axis=1)
    return win.astype(BF16), tsum.astype(BF16), wout.astype(BF16), lam


def _s5_in_kernel(u_ref, w_ref, o_ref):
    o_ref[...] = jnp.dot(u_ref[...], w_ref[...], preferred_element_type=F32)


def _s5_in_proj(u, w_s5):
    b, t, d = u.shape
    tm = _big_row_tile(t)
    return pl.pallas_call(
        _s5_in_kernel,
        out_shape=jax.ShapeDtypeStruct((b, t, S5_W), F32),
        grid=(b, t // tm),
        in_specs=[pl.BlockSpec((None, tm, d), lambda bi, i: (bi, i, 0)),
                  pl.BlockSpec((d, S5_W), lambda bi, i: (0, 0))],
        out_specs=pl.BlockSpec((None, tm, S5_W), lambda bi, i: (bi, i, 0)),
        compiler_params=_params("parallel", "arbitrary"),
        name="s5_in_proj",
    )(u, w_s5)


def _s5(u5, s5w, n_ctx):
    b, t, _ = u5.shape
    win, tsum, wout, lam = s5w
    gpb = LANES // S5_GROUP
    nc = t // S5_CHUNK
    kw = S5_CHUNK * S5_GROUP
    assert (n_ctx // S5_CHUNK) % 8 == 0 and nc % 8 == 0
    kern = functools.partial(_s5_kernel, nc=nc, ncc=n_ctx // S5_CHUNK)
    wspec = lambda r, c: pl.BlockSpec((gpb, r, c), lambda bi, gb: (gb, 0, 0))
    return pl.pallas_call(
        kern,
        out_shape=jax.ShapeDtypeStruct((b, t, S5_W), F32),
        grid=(b, S5_W // LANES),
        in_specs=[pl.BlockSpec((None, t, LANES), lambda bi, gb: (bi, 0, gb)),
                  wspec(kw, 8 * S5_STATE), wspec(kw, kw), wspec(4 * S5_STATE, kw), wspec(64, 2 * S5_STATE)],
        out_specs=pl.BlockSpec((None, t, LANES), lambda bi, gb: (bi, 0, gb)),
        scratch_shapes=[pltpu.VMEM((gpb, nc, kw), BF16),
                        pltpu.VMEM((gpb, nc, 8 * S5_STATE), F32),
                        pltpu.VMEM((gpb, nc, 4 * S5_STATE), F32)],
        compiler_params=_params("parallel", "parallel"),
        name="s5_scan",
    )(u5, win, tsum, wout, lam)


def _glu_kernel(s_ref, u_ref, wg_ref, w_ref, b_ref, o_ref):
    s = jax.nn.gelu(s_ref[...])
    z = jnp.dot(s.astype(BF16), w_ref[...], preferred_element_type=F32) + b_ref[...]
    g = jnp.dot(u_ref[...], wg_ref[...], preferred_element_type=F32)
    o_ref[...] = (s * jax.nn.sigmoid(z) * (g * jax.nn.sigmoid(g))).astype(BF16)


def _s5_glu_gate(s_pre, u, w_s5, glu_w, glu_b):
    b, t, d = u.shape
    tm = _big_row_tile(t)
    return pl.pallas_call(
        _glu_kernel,
        out_shape=jax.ShapeDtypeStruct((b, t, S5_W), BF16),
        grid=(b, t // tm),
        in_specs=[pl.BlockSpec((None, tm, S5_W), lambda bi, i: (bi, i, 0)),
                  pl.BlockSpec((None, tm, d), lambda bi, i: (bi, i, 0)),
                  pl.BlockSpec((d, S5_W), lambda bi, i: (0, 1)),
                  pl.BlockSpec((S5_W, S5_W), lambda bi, i: (0, 0)),
                  pl.BlockSpec((1, S5_W), lambda bi, i: (0, 0))],
        out_specs=pl.BlockSpec((None, tm, S5_W), lambda bi, i: (bi, i, 0)),
        compiler_params=_params("parallel", "arbitrary"),
        name="s5_glu_gate",
    )(s_pre, u, w_s5, glu_w, glu_b)


def _gate_kernel(u_ref, w_ref, r_ref, a_ref, gn_ref, o_ref):
    j = pl.program_id(2)
    tm = u_ref.shape[0]
    nsplit = 2 if tm % 32 == 0 else 1
    hr = tm // nsplit

    def halves(act_ref, scale):
        for r in range(nsplit):
            rows = slice(r * hr, (r + 1) * hr)
            g = jnp.dot(u_ref[rows, :], w_ref[...], preferred_element_type=F32)
            act = act_ref[rows, :].astype(F32)
            if scale is not None:
                act = act * scale
            o_ref[rows, :] = (act * (g * jax.nn.sigmoid(g))).astype(BF16)

    @pl.when(j < 2)
    def _():
        halves(r_ref, gn_ref[...])

    @pl.when(j >= 2)
    def _():
        halves(a_ref, None)


def _branch_gates(u, wb, layer, r, a, gn_w):
    b, t, d = u.shape
    tm = _big_row_tile(t)
    ret_g0 = 3 * RET_W // COL_TILE
    att_g0 = (4 * RET_W + ATT_W + 2 * ATT_KV_W) // COL_TILE
    act = lambda off: pl.BlockSpec((None, tm, COL_TILE), lambda bi, i, j: (bi, i, jnp.clip(j - off, 0, 1)))
    return pl.pallas_call(
        _gate_kernel,
        out_shape=jax.ShapeDtypeStruct((b, t, RET_W + ATT_W), BF16),
        grid=(b, t // tm, (RET_W + ATT_W) // COL_TILE),
        in_specs=[pl.BlockSpec((None, tm, d), lambda bi, i, j: (bi, i, 0)),
                  pl.BlockSpec((None, d, COL_TILE),
                               lambda bi, i, j: (layer, 0, jnp.where(j < 2, ret_g0 + j, att_g0 + j - 2))),
                  act(0), act(2),
                  pl.BlockSpec((1, COL_TILE), lambda bi, i, j: (0, jnp.clip(j, 0, 1)))],
        out_specs=pl.BlockSpec((None, tm, COL_TILE), lambda bi, i, j: (bi, i, j)),
        compiler_params=_params("parallel", "parallel", "arbitrary"),
        name="branch_gates",
    )(u, wb, r, a, gn_w)


def _merge_kernel(u_ref, zra_ref, zs_ref, wm0_ref, wm1_ref, wm2_ref, wr_ref, wa_ref, ws_ref, o_ref):
    u = u_ref[...]
    acts = (zra_ref[:, :RET_W], zra_ref[:, RET_W:], zs_ref[...])
    acc = None
    for act, wm_ref, wbr_ref in zip(acts, (wm0_ref, wm1_ref, wm2_ref), (wr_ref, wa_ref, ws_ref)):
        gate = jax.nn.sigmoid(jnp.dot(u, wm_ref[...], preferred_element_type=F32))
        proj = jnp.dot(act, wbr_ref[...], preferred_element_type=F32)
        acc = gate * proj if acc is None else acc + gate * proj
    o_ref[...] = acc.astype(BF16)


def _merge(u, z_ra, z_s, wb, layer, w_ret, w_att, w_s5):
    b, t, d = u.shape
    tm = 528 if t % 528 == 0 else _big_row_tile(t)
    m0 = (wb.shape[2] - N_BRANCH * d) // COL_TILE
    per = d // COL_TILE
    wm = lambda br: pl.BlockSpec((None, d, COL_TILE), lambda bi, i, j: (layer, 0, m0 + br * per + j))
    wbr = lambda k: pl.BlockSpec((k, COL_TILE), lambda bi, i, j: (0, j))
    return pl.pallas_call(
        _merge_kernel,
        out_shape=jax.ShapeDtypeStruct((b, t, d), BF16),
        grid=(b, t // tm, d // COL_TILE),
        in_specs=[pl.BlockSpec((None, tm, d), lambda bi, i, j: (bi, i, 0)),
                  pl.BlockSpec((None, tm, RET_W + ATT_W), lambda bi, i, j: (bi, i, 0)),
                  pl.BlockSpec((None, tm, S5_W), lambda bi, i, j: (bi, i, 0)),
                  wm(0), wm(1), wm(2), wbr(RET_W), wbr(ATT_W), wbr(S5_W)],
        out_specs=pl.BlockSpec((None, tm, COL_TILE), lambda bi, i, j: (bi, i, j)),
        compiler_params=_params("parallel", "parallel", "arbitrary"),
        name="gated_merge",
    )(u, z_ra, z_s, wb, wb, wb, w_ret, w_att, w_s5)


def _out_kernel(m_ref, w_ref, c_ref, x_ref, mod_ref, lnw_ref, lnb_ref, o_ref, v_ref, *, alpha, first_tile):
    i = pl.program_id(1)
    d = w_ref.shape[1]

    def run(res_ref):
        m = m_ref[...]
        s1 = jnp.zeros((m.shape[0], 1), F32)
        for c in range(d // COL_TILE):
            cols = slice(c * COL_TILE, (c + 1) * COL_TILE)
            y = jnp.dot(m, w_ref[:, cols], preferred_element_type=F32)
            v = alpha * res_ref[:, cols] + mod_ref[2:3, cols] * y
            v_ref[:, cols] = v
            s1 += jnp.sum(v, axis=-1, keepdims=True)
        mu = s1 * (1.0 / d)
        vc = v_ref[...] - mu
        var = jnp.mean(vc * vc, axis=-1, keepdims=True)
        o_ref[...] = vc * lax.rsqrt(var + LN_EPS) * lnw_ref[...] + lnb_ref[...]

    if first_tile == 0:
        @pl.when(i == 0)
        def _():
            run(c_ref)

        @pl.when(i > 0)
        def _():
            run(x_ref)
    else:
        run(x_ref)


def _out_proj(m, w_out, ctx_src, lat_src, modsel, ln_w, ln_b, alpha, skip_ctx):
    b, t, d = m.shape
    off = 1 if skip_ctx else 0
    nt = t // ROW_TILE - off
    kern = functools.partial(_out_kernel, alpha=alpha, first_tile=off)
    vec = lambda: pl.BlockSpec((1, d), lambda bi, i: (0, 0))
    return pl.pallas_call(
        kern,
        out_shape=jax.ShapeDtypeStruct((b, nt * ROW_TILE, d), F32),
        grid=(b, nt),
        in_specs=[pl.BlockSpec((None, ROW_TILE, d), lambda bi, i: (bi, i + off, 0)),
                  pl.BlockSpec((d, d), lambda bi, i: (0, 0)),
                  *_token_specs(ctx_src, lat_src, off),
                  pl.BlockSpec((None, None, 3, d), lambda bi, i: (bi, jnp.minimum(i + off, 1), 0, 0)),
                  vec(), vec()],
        out_specs=pl.BlockSpec((None, ROW_TILE, d), lambda bi, i: (bi, i, 0)),
        scratch_shapes=[pltpu.VMEM((ROW_TILE, d), F32)],
        compiler_params=_params("parallel", "arbitrary"),
        name="out_proj_norm",
    )(m, w_out, ctx_src, lat_src, modsel, ln_w.reshape(1, d), ln_b.reshape(1, d))


def _rope_tables(n, n_ctx):
    rows = n // GRID_W
    row = jnp.repeat(jnp.arange(rows, dtype=F32), GRID_W)
    col = jnp.tile(jnp.arange(GRID_W, dtype=F32), rows)

    def table(head_dim):
        per_axis = head_dim // 4
        inv = ROPE_THETA ** (-jnp.arange(per_axis, dtype=F32) / per_axis)
        ang = jnp.concatenate([row[:, None] * inv, col[:, None] * inv], axis=-1)
        cos = jnp.concatenate([jnp.ones((n_ctx, head_dim // 2), F32), jnp.cos(ang)], axis=0)
        sin = jnp.concatenate([jnp.zeros((n_ctx, head_dim // 2), F32), jnp.sin(ang)], axis=0)
        return cos, sin

    cos_r, sin_r = table(RET_DK)
    cos_a, sin_a = table(ATT_HD)
    return (cos_r, sin_r, jnp.concatenate([cos_a, cos_a], -1), jnp.concatenate([-sin_a, sin_a], -1))


def kernel(x, c, ctx, c_ctx, ada_w, ada_b, w_in, ret_log_decay, ret_gn_w, att_q_norm, att_k_norm, s5_a_re, s5_a_im, s5_log_dt, s5_b_re, s5_b_im, s5_c_re, s5_c_im, s5_d, s5_glu_w, s5_glu_b, w_br_ret, w_br_att, w_br_s5, w_out, ln_w, ln_b):
    b, n, d = x.shape
    n_ctx = ctx.shape[1]
    depth = w_in.shape[0]
    assert n_ctx == ROW_TILE and n % ROW_TILE == 0 and n % GRID_W == 0
    assert w_in.shape[2] == 4 * RET_W + 2 * ATT_W + 2 * ATT_KV_W + 2 * S5_W + N_BRANCH * d
    alpha = (2.0 * depth) ** 0.25
    s5_col0 = 4 * RET_W + 2 * ATT_W + 2 * ATT_KV_W

    t = n_ctx + n
    ctx_src, lat_src = ctx, x
    cos_r, sin_r, cos_a, sin_a = _rope_tables(n, n_ctx)
    crows = 16
    cvec = jnp.concatenate([c, c_ctx[None, :], jnp.zeros((crows - b - 1, d), F32)], axis=0)
    wb = w_in.astype(BF16)

    for l in range(depth):
        last = l == depth - 1
        mod = _modulation(cvec, ada_w, ada_b, l).reshape(crows, 3, d)
        modsel = jnp.stack([jnp.broadcast_to(mod[b][None], (b, 3, d)), mod[:b]], axis=1)
        w_s5 = wb[l, :, s5_col0:s5_col0 + 2 * S5_W]
        s5w = _s5_weights(s5_a_re[l], s5_a_im[l], s5_log_dt[l], s5_b_re[l], s5_b_im[l],
                          s5_c_re[l], s5_c_im[l], s5_d[l])

        u = _modulate(ctx_src, lat_src, t, modsel)
        mix = _mix_proj(u, wb, l, cos_r, sin_r, cos_a, sin_a,
                        att_q_norm[l].reshape(1, ATT_HD), att_k_norm[l].reshape(1, ATT_HD))
        r = _retention(mix, ret_log_decay[l], n_ctx)
        a = _attention(mix, n_ctx)
        s_pre = _s5(_s5_in_proj(u, w_s5), s5w, n_ctx)
        z_s = _s5_glu_gate(s_pre, u, w_s5, s5_glu_w[l].astype(BF16), s5_glu_b[l].reshape(1, S5_W))
        z_ra = _branch_gates(u, wb, l, r, a, ret_gn_w[l].reshape(1, RET_W))
        m = _merge(u, z_ra, z_s, wb, l, w_br_ret[l].astype(BF16), w_br_att[l].astype(BF16),
                   w_br_s5[l].astype(BF16))
        xa = _out_proj(m, w_out[l].astype(BF16), ctx_src, lat_src, modsel, ln_w[l], ln_b[l], alpha, last)
        ctx_src = lat_src = xa
    return xa
```

```python
import functools
import math

import jax
import jax.numpy as jnp
from jax import lax
from jax.experimental import pallas as pl
from jax.experimental.pallas import tpu as pltpu

F32 = jnp.float32
BF16 = jnp.bfloat16

GRID_W = 64
RET_HEADS = 4
RET_DK = 256
RET_DV = 256
RET_W = RET_HEADS * RET_DV
ATT_HEADS = 8
ATT_KV_HEADS = 2
ATT_GROUP = ATT_HEADS // ATT_KV_HEADS
ATT_HD = 128
ATT_W = ATT_HEADS * ATT_HD
ATT_KV_W = ATT_KV_HEADS * ATT_HD
ROPE_THETA = 10000.0
S5_GROUP = 16
S5_W = 768
S5_GROUPS = S5_W // S5_GROUP
S5_STATE = 64
N_BRANCH = 3
LN_EPS = 1e-6
RMS_EPS = 1e-6

LANES = 128
VMEM_LIMIT_BYTES = 56 * 1024 * 1024

CHUNK = 128
S5_CHUNK = 16
ROW_TILE = 256
COL_TILE = 512
MIX_W = 4608
ATT_SUB = 2048
LOG2E = math.log2(math.e)


def _params(*sem):
    return pltpu.CompilerParams(dimension_semantics=sem, vmem_limit_bytes=VMEM_LIMIT_BYTES)


def _big_row_tile(t):
    for cand in (1056, 1024, 768, 640, 512, 256):
        if t % cand == 0:
            return cand
    raise ValueError(f"unsupported token count {t}")


def _mod_kernel(c_ref, w_ref, b_ref, o_ref):
    c = c_ref[...]
    s = c * jax.nn.sigmoid(c)
    s_hi = s.astype(BF16)
    s_lo = (s - s_hi.astype(F32)).astype(BF16)
    w = w_ref[...]
    w_hi = w.astype(BF16)
    w_lo = (w - w_hi.astype(F32)).astype(BF16)
    acc = jnp.dot(s_hi, w_hi, preferred_element_type=F32)
    acc += jnp.dot(s_lo, w_hi, preferred_element_type=F32)
    acc += jnp.dot(s_hi, w_lo, preferred_element_type=F32)
    o_ref[...] = acc + b_ref[...]


def _modulation(cvec, ada_w, ada_b, layer):
    rows, d = cvec.shape
    n = ada_w.shape[2]
    tn = 768
    return pl.pallas_call(
        _mod_kernel,
        out_shape=jax.ShapeDtypeStruct((rows, n), F32),
        grid=(n // tn,),
        in_specs=[pl.BlockSpec((rows, d), lambda j: (0, 0)),
                  pl.BlockSpec((None, d, tn), lambda j: (layer, 0, j)),
                  pl.BlockSpec((None, 1, tn), lambda j: (layer, 0, j))],
        out_specs=pl.BlockSpec((rows, tn), lambda j: (0, j)),
        compiler_params=_params("arbitrary"),
        name="adaln_modulation",
    )(cvec, ada_w, ada_b.reshape(ada_b.shape[0], 1, n))


def _modulate_kernel(c_ref, x_ref, mod_ref, o_ref):
    i = pl.program_id(1)
    m = mod_ref[...]

    @pl.when(i == 0)
    def _():
        o_ref[...] = (c_ref[...] * (1.0 + m[1:2]) + m[0:1]).astype(BF16)

    @pl.when(i > 0)
    def _():
        o_ref[...] = (x_ref[...] * (1.0 + m[1:2]) + m[0:1]).astype(BF16)


def _token_specs(ctx_src, lat_src, first_tile=0):
    d = ctx_src.shape[2]
    shift = first_tile - (0 if lat_src is ctx_src else 1)
    return (pl.BlockSpec((None, ROW_TILE, d), lambda bi, i: (bi, 0, 0)),
            pl.BlockSpec((None, ROW_TILE, d), lambda bi, i: (bi, jnp.maximum(i + shift, 0), 0)))


def _modulate(ctx_src, lat_src, t, modsel):
    b, _, d = ctx_src.shape
    return pl.pallas_call(
        _modulate_kernel,
        out_shape=jax.ShapeDtypeStruct((b, t, d), BF16),
        grid=(b, t // ROW_TILE),
        in_specs=[*_token_specs(ctx_src, lat_src),
                  pl.BlockSpec((None, None, 3, d), lambda bi, i: (bi, jnp.minimum(i, 1), 0, 0))],
        out_specs=pl.BlockSpec((None, ROW_TILE, d), lambda bi, i: (bi, i, 0)),
        compiler_params=_params("parallel", "arbitrary"),
        name="modulate",
    )(ctx_src, lat_src, modsel)


def _mix_kernel(u_ref, w_ref, cr_ref, sr_ref, ca_ref, sa_ref, qn_ref, kn_ref, o_ref):
    j = pl.program_id(2)
    tm = u_ref.shape[0]
    nsplit = 2 if tm % 32 == 0 else 1
    hr = tm // nsplit

    def halves(epilogue):
        for r in range(nsplit):
            rows = slice(r * hr, (r + 1) * hr)
            epilogue(rows, jnp.dot(u_ref[rows, :], w_ref[...], preferred_element_type=F32))

    def rms_rope(x, rows, w, scale):
        ms = jnp.mean(x * x, axis=-1, keepdims=True)
        y = x * lax.rsqrt(ms + RMS_EPS) * w
        y = y * ca_ref[rows, :] + pltpu.roll(y, ATT_HD // 2, 1) * sa_ref[rows, :]
        return (y * scale).astype(BF16)

    @pl.when(j < 4)
    def _():
        scale = jnp.where(j >= 2, RET_DK ** -0.5, 1.0).astype(F32)

        def epilogue(rows, acc):
            cos = cr_ref[rows, :]
            sin = sr_ref[rows, :]
            for h in range(COL_TILE // RET_DK):
                x1 = acc[:, h * RET_DK: h * RET_DK + LANES]
                x2 = acc[:, h * RET_DK + LANES: (h + 1) * RET_DK]
                o_ref[rows, h * RET_DK: h * RET_DK + LANES] = ((x1 * cos - x2 * sin) * scale).astype(BF16)
                o_ref[rows, h * RET_DK + LANES: (h + 1) * RET_DK] = ((x2 * cos + x1 * sin) * scale).astype(BF16)

        halves(epilogue)

    @pl.when((j == 4) | (j == 5))
    def _():
        def epilogue(rows, acc):
            o_ref[rows, :] = acc.astype(BF16)

        halves(epilogue)

    @pl.when((j == 6) | (j == 7))
    def _():
        def epilogue(rows, acc):
            for h in range(COL_TILE // ATT_HD):
                o_ref[rows, h * ATT_HD:(h + 1) * ATT_HD] = rms_rope(
                    acc[:, h * ATT_HD:(h + 1) * ATT_HD], rows, qn_ref[...], ATT_HD ** -0.5 * LOG2E)

        halves(epilogue)

    @pl.when(j == 8)
    def _():
        def epilogue(rows, acc):
            for h in range(ATT_KV_HEADS):
                o_ref[rows, h * ATT_HD:(h + 1) * ATT_HD] = rms_rope(
                    acc[:, h * ATT_HD:(h + 1) * ATT_HD], rows, kn_ref[...], 1.0)
            o_ref[rows, ATT_KV_W:] = acc[:, ATT_KV_W:].astype(BF16)

        halves(epilogue)


def _mix_proj(u, wb, layer, cos_r, sin_r, cos_a, sin_a, qn, kn):
    b, t, d = u.shape
    tm = _big_row_tile(t)
    tab = lambda: pl.BlockSpec((tm, LANES), lambda bi, i, j: (i, 0))
    vec = lambda: pl.BlockSpec((1, LANES), lambda bi, i, j: (0, 0))
    gap = RET_W // COL_TILE
    return pl.pallas_call(
        _mix_kernel,
        out_shape=jax.ShapeDtypeStruct((b, t, MIX_W), BF16),
        grid=(b, t // tm, MIX_W // COL_TILE),
        in_specs=[pl.BlockSpec((None, tm, d), lambda bi, i, j: (bi, i, 0)),
                  pl.BlockSpec((None, d, COL_TILE), lambda bi, i, j: (layer, 0, jnp.where(j >= 6, j + gap, j))),
                  tab(), tab(), tab(), tab(), vec(), vec()],
        out_specs=pl.BlockSpec((None, tm, COL_TILE), lambda bi, i, j: (bi, i, j)),
        compiler_params=_params("parallel", "parallel", "arbitrary"),
        name="mixer_in_proj",
    )(u, wb, cos_r, sin_r, cos_a, sin_a, qn, kn)


def _ret_kernel(ld_ref, q_ref, k_ref, v_ref, o_ref, acc_ref, sf_ref, sb_ref, *, n_ctx_chunks, n_chunks):
    h = pl.program_id(1)
    lgf = ld_ref[0, h]
    lgb = ld_ref[1, h]
    c = CHUNK
    ri = lax.broadcasted_iota(jnp.int32, (c, c), 0).astype(F32)
    ci = lax.broadcasted_iota(jnp.int32, (c, c), 1).astype(F32)
    rel = ri - ci
    mask = jnp.where(rel >= 0, jnp.exp(lgf * jnp.maximum(rel, 0.0)), jnp.exp(lgb * jnp.maximum(-rel, 0.0)))
    row = lax.broadcasted_iota(jnp.int32, (c, RET_DV), 0).astype(F32)
    qdec_f = jnp.exp(lgf * (row + 1.0))
    qdec_b = jnp.exp(lgb * (c - row))
    kdec_f = jnp.exp(lgf * (c - 1.0 - row))
    kdec_b = jnp.exp(lgb * row)
    cdec_f = jnp.exp(jnp.full((1, RET_DV), lgf * c, F32))
    cdec_b = jnp.exp(jnp.full((1, RET_DV), lgb * c, F32))
    contract0 = (((0,), (0,)), ((), ()))
    contract1 = (((1,), (1,)), ((), ()))

    def load(ci_):
        r0 = pl.multiple_of(ci_ * c, c)
        rows = pl.ds(r0, c)
        return rows, q_ref[rows, :], k_ref[rows, :], v_ref[rows, :]

    def bwd(ci_):
        rows, q, k, v = load(ci_)
        s = sb_ref[...]
        acc_ref[rows, :] += jnp.dot(q, s.astype(BF16), preferred_element_type=F32) * qdec_b
        kd = (k.astype(F32) * kdec_b).astype(BF16)
        sb_ref[...] = s * cdec_b + lax.dot_general(kd, v, contract0, preferred_element_type=F32)

    def fwd(ci_):
        rows, q, k, v = load(ci_)
        s = sf_ref[...]
        sc = lax.dot_general(q, k, contract1, preferred_element_type=F32) * mask
        o = jnp.dot(sc.astype(BF16), v, preferred_element_type=F32)
        o += jnp.dot(q, s.astype(BF16), preferred_element_type=F32) * qdec_f
        acc_ref[rows, :] += o
        kd = (k.astype(F32) * kdec_f).astype(BF16)
        sf_ref[...] = s * cdec_f + lax.dot_general(kd, v, contract0, preferred_element_type=F32)

    sf_ref[...] = jnp.zeros_like(sf_ref)
    sb_ref[...] = jnp.zeros_like(sb_ref)
    acc_ref[...] = jnp.zeros_like(acc_ref)

    def ctx_part(t, carry):
        fwd(t)
        bwd(n_ctx_chunks - 1 - t)
        return carry

    def lat_part(t, carry):
        fwd(t)
        bwd(n_chunks - 1 - (t - n_ctx_chunks))
        return carry

    lax.fori_loop(0, n_ctx_chunks, ctx_part, 0, unroll=2)
    lax.fori_loop(n_ctx_chunks, n_chunks, lat_part, 0, unroll=4)

    def norm(t, carry):
        rows = pl.ds(pl.multiple_of(t * c, c), c)
        o = acc_ref[rows, :]
        mu = jnp.mean(o, axis=-1, keepdims=True)
        oc = o - mu
        var = jnp.mean(oc * oc, axis=-1, keepdims=True)
        o_ref[rows, :] = (oc * lax.rsqrt(var + LN_EPS)).astype(BF16)
        return carry

    lax.fori_loop(0, n_chunks, norm, 0, unroll=6)


def _retention(mix, log_decay, n_ctx):
    b, t, _ = mix.shape
    kern = functools.partial(_ret_kernel, n_ctx_chunks=n_ctx // CHUNK, n_chunks=t // CHUNK)
    blk = lambda off: pl.BlockSpec((None, t, RET_DK), lambda bi, h: (bi, 0, off + h))
    return pl.pallas_call(
        kern,
        out_shape=jax.ShapeDtypeStruct((b, t, RET_W), BF16),
        grid=(b, RET_HEADS),
        in_specs=[pl.BlockSpec(memory_space=pltpu.SMEM), blk(0), blk(RET_HEADS), blk(2 * RET_HEADS)],
        out_specs=pl.BlockSpec((None, t, RET_DV), lambda bi, h: (bi, 0, h)),
        scratch_shapes=[pltpu.VMEM((t, RET_DV), F32), pltpu.VMEM((RET_DK, RET_DV), F32),
                        pltpu.VMEM((RET_DK, RET_DV), F32)],
        compiler_params=_params("parallel", "arbitrary"),
        name="retention",
    )(log_decay, mix, mix, mix)


def _att_kernel(q_ref, k_ref, v_ref, o_ref, v1_ref, q4_ref, s_ref, mx_ref, m_ref, acc_ref, *, chunks, n_ctx, t):
    i = pl.program_id(2)
    contract1 = (((1,), (1,)), ((), ()))
    tq = q_ref.shape[0]
    sb = ATT_SUB

    @pl.when(i == 0)
    def _():
        v1_ref[:, :ATT_HD] = v_ref[...]
        v1_ref[:, ATT_HD:] = jnp.ones((t, ATT_HD), BF16)

    for g in range(ATT_GROUP):
        q4_ref[g * tq:(g + 1) * tq, :] = q_ref[:, g * ATT_HD:(g + 1) * ATT_HD]
    neg = jnp.full(mx_ref.shape, -jnp.inf, F32)
    mx_ref[...] = neg
    m_ref[...] = neg
    acc_ref[...] = jnp.zeros(acc_ref.shape, F32)

    def a_step(buf, j, r0, size):
        k = k_ref[pl.ds(r0, size), :]
        s = lax.dot_general(q4_ref[...], k, contract1, preferred_element_type=F32)
        s_ref[buf, j, :, 0:size] = s
        mx = mx_ref[...]
        for tt in range(size // LANES):
            mx = jnp.maximum(mx, s[:, tt * LANES:(tt + 1) * LANES])
        mx_ref[...] = mx

    def settle():
        mx = mx_ref[...]
        mc = jnp.broadcast_to(jnp.max(mx, axis=-1, keepdims=True), mx.shape)
        m_old = m_ref[...]
        m_new = jnp.maximum(m_old, mc)
        alpha = jnp.exp2(m_old - m_new)
        m_ref[...] = m_new
        acc_ref[:, :ATT_HD] = acc_ref[:, :ATT_HD] * alpha
        acc_ref[:, ATT_HD:] = acc_ref[:, ATT_HD:] * alpha
        mx_ref[...] = neg

    def b_step(buf, j, r0, size):
        mb = m_ref[...]
        p = jnp.concatenate([jnp.exp2(s_ref[buf, j, :, tt * LANES:(tt + 1) * LANES] - mb)
                             for tt in range(size // LANES)], axis=1).astype(BF16)
        acc_ref[...] += jnp.dot(p, v1_ref[pl.ds(r0, size), :], preferred_element_type=F32)

    def stage(ca, cb):
        sides = [(a_step, ca), (b_step, cb)]
        sides = [(fn, ch) for fn, ch in sides if ch is not None]
        common = min(ch[2] // sb for _, ch in sides)

        def body(j, carry):
            off = pl.multiple_of(j * sb, sb)
            for fn, (buf, start, _) in sides:
                fn(buf, j, start + off, sb)
            return carry

        if common:
            lax.fori_loop(0, common, body, 0)
        for fn, (buf, start, size) in sides:
            for j in range(common, -(-size // sb)):
                fn(buf, j, start + j * sb, min(sb, size - j * sb))
        if ca is not None:
            settle()

    @pl.when(i == 0)
    def _():
        stage((0, 0, n_ctx), None)
        stage(None, (0, 0, n_ctx))

    @pl.when(i > 0)
    def _():
        bufs = [(c % 2, start, size) for c, (start, size) in enumerate(chunks)]
        for c in range(len(bufs) + 1):
            stage(bufs[c] if c < len(bufs) else None, bufs[c - 1] if c > 0 else None)

    for g in range(ATT_GROUP):
        rows = slice(g * tq, (g + 1) * tq)
        o_ref[:, g * ATT_HD:(g + 1) * ATT_HD] = (acc_ref[rows, :ATT_HD] / acc_ref[rows, ATT_HD:]).astype(BF16)


def _attention(mix, n_ctx):
    b, t, _ = mix.shape
    tq = ROW_TILE
    n_lat = t - n_ctx
    nch = 4 if n_lat % (4 * COL_TILE) == 0 else 1
    chunks = tuple((0, n_ctx + n_lat // nch) if c == 0 else (n_ctx + c * (n_lat // nch), n_lat // nch)
                   for c in range(nch))
    mrows = ATT_GROUP * tq
    q_blk0 = (3 * RET_W) // (ATT_GROUP * ATT_HD)
    k_blk0 = (3 * RET_W + ATT_W) // ATT_HD
    v_blk0 = k_blk0 + ATT_KV_HEADS
    kern = functools.partial(_att_kernel, chunks=chunks, n_ctx=n_ctx, t=t)
    return pl.pallas_call(
        kern,
        out_shape=jax.ShapeDtypeStruct((b, t, ATT_W), BF16),
        grid=(b, ATT_KV_HEADS, t // tq),
        in_specs=[pl.BlockSpec((None, tq, ATT_GROUP * ATT_HD), lambda bi, hk, i: (bi, i, q_blk0 + hk)),
                  pl.BlockSpec((None, t, ATT_HD), lambda bi, hk, i: (bi, 0, k_blk0 + hk)),
                  pl.BlockSpec((None, t, ATT_HD), lambda bi, hk, i: (bi, 0, v_blk0 + hk))],
        out_specs=pl.BlockSpec((None, tq, ATT_GROUP * ATT_HD), lambda bi, hk, i: (bi, i, hk)),
        scratch_shapes=[pltpu.VMEM((t, 2 * ATT_HD), BF16),
                        pltpu.VMEM((mrows, ATT_HD), BF16),
                        pltpu.VMEM((2, -(-chunks[0][1] // ATT_SUB), mrows, ATT_SUB), F32),
                        pltpu.VMEM((mrows, LANES), F32),
                        pltpu.VMEM((mrows, LANES), F32),
                        pltpu.VMEM((mrows, 2 * ATT_HD), F32)],
        compiler_params=_params("parallel", "parallel", "arbitrary"),
        name="gqa_attention",
    )(mix, mix, mix)


def _s5_kernel(u_ref, win_ref, tsum_ref, wout_ref, lam_ref, y_ref, uc_ref, d_ref, xin_ref, *, nc, ncc):
    p2 = 2 * S5_STATE
    sub = 8
    ln = S5_CHUNK
    gpb = LANES // S5_GROUP
    us = [u_ref[pl.ds(s, nc, stride=ln), :] for s in range(ln)]
    for g in range(gpb):
        uc = jnp.concatenate([us[s][:, g * S5_GROUP:(g + 1) * S5_GROUP] for s in range(ln)], axis=1).astype(BF16)
        uc_ref[g] = uc
        d_ref[g] = jnp.dot(uc, win_ref[g], preferred_element_type=F32)
    rowid = lax.broadcasted_iota(jnp.int32, (sub, p2), 0)

    def tile(g, reverse, rows, c, cs):
        base = 32 if reverse else 0
        col = 2 * p2 if reverse else 0

        def mult(i):
            return tuple(jnp.broadcast_to(lam_ref[g, base + 3 * i + r:base + 3 * i + r + 1, :], (sub, p2))
                         for r in range(3))

        def shifted(v, n):
            if reverse:
                return jnp.where(rowid < sub - n, pltpu.roll(v, sub - n, 0), 0.0)
            return jnp.where(rowid >= n, pltpu.roll(v, n, 0), 0.0)

        x = d_ref[g, rows, col:col + p2]
        xs = d_ref[g, rows, col + p2:col + 2 * p2]
        for i, n in enumerate((1, 2, 4)):
            a, b, bs = mult(i)
            xr, xsr = shifted(x, n), shifted(xs, n)
            x, xs = x + a * xr + b * xsr, xs + a * xsr + bs * xr
        pa = lam_ref[g, base + 16:base + 24, :]
        pb = lam_ref[g, base + 24:base + 32, :]
        xin_ref[g, rows, col // 2:col // 2 + p2] = shifted(x, 1) + pa * c + pb * cs
        edge = 0 if reverse else sub - 1
        last = jnp.broadcast_to(x[edge:edge + 1, :], (sub, p2))
        lasts = jnp.broadcast_to(xs[edge:edge + 1, :], (sub, p2))
        a, b, bs = mult(3)
        return a * c + b * cs + last, a * cs + bs * c + lasts

    nt = nc // sub
    nct = ncc // sub

    def step(mf, mb, carry):
        rf = pl.ds(pl.multiple_of(mf * sub, sub), sub)
        rb = pl.ds(pl.multiple_of(mb * sub, sub), sub)
        return tuple(tile(g, False, rf, carry[g][0], carry[g][1]) + tile(g, True, rb, carry[g][2], carry[g][3])
                     for g in range(gpb))

    zero = jnp.zeros((sub, p2), F32)
    init = tuple((zero, zero, zero, zero) for _ in range(gpb))
    carry = lax.fori_loop(0, nct, lambda t, cr: step(t, nct - 1 - t, cr), init)
    lax.fori_loop(nct, nt, lambda t, cr: step(t, nt - 1 - (t - nct), cr), carry)

    for g in range(gpb):
        y = jnp.dot(uc_ref[g], tsum_ref[g], preferred_element_type=F32)
        y += jnp.dot(xin_ref[g].astype(BF16), wout_ref[g], preferred_element_type=F32)
        d_ref[g, :, 0:ln * S5_GROUP] = y
    for s in range(ln):
        y_ref[pl.ds(s, nc, stride=ln), :] = jnp.concatenate(
            [d_ref[g, :, s * S5_GROUP:(s + 1) * S5_GROUP] for g in range(gpb)], axis=1)


def _s5_weights(a_re, a_im, log_dt, b_re, b_im, c_re, c_im, d_skip):
    hi = lax.Precision.HIGHEST
    ln = S5_CHUNK
    dt = jnp.exp(log_dt)[..., None]
    mag = jnp.exp(a_re * dt)
    abr, abi = mag * jnp.cos(a_im * dt), mag * jnp.sin(a_im * dt)
    den = a_re * a_re + a_im * a_im
    fr = ((abr - 1.0) * a_re + abi * a_im) / den
    fi = (abi * a_re - (abr - 1.0) * a_im) / den
    bbr = fr[..., None] * b_re - fi[..., None] * b_im
    bbi = fr[..., None] * b_im + fi[..., None] * b_re
    n = jnp.arange(ln + 1, dtype=F32)[:, None, None, None]
    pmag = jnp.exp(a_re * dt * n)
    pr, pi = pmag * jnp.cos(a_im * dt * n), pmag * jnp.sin(a_im * dt * n)
    cr = c_re[None] * pr[:, :, :, None, :] - c_im[None] * pi[:, :, :, None, :]
    ci = c_re[None] * pi[:, :, :, None, :] + c_im[None] * pr[:, :, :, None, :]
    g_ = a_re.shape[1]
    lhs = jnp.concatenate([cr[:ln], -ci[:ln]], axis=-1).transpose(1, 2, 0, 3, 4)
    lhs = lhs.reshape(2, g_, ln * S5_GROUP, 2 * S5_STATE)
    rhs = jnp.concatenate([bbr, bbi], axis=2)
    m = jnp.einsum('dgmk,dgke->dgme', lhs, rhs, precision=hi)
    m = m.reshape(2, g_, ln, S5_GROUP, S5_GROUP).transpose(2, 0, 1, 3, 4)
    lag = jnp.arange(ln)[None, :] - jnp.arange(ln)[:, None]
    n_idx = jnp.arange(ln)[:, None, None]
    oh_f = (lag[None] == n_idx).astype(F32)
    oh_b = (-lag[None] == n_idx).astype(F32)
    tsum = (jnp.einsum('nsi,ngce->gseic', oh_f, m[:, 0], precision=hi)
            + jnp.einsum('nsi,ngce->gseic', oh_b, m[:, 1], precision=hi))
    g = a_re.shape[1]
    eye_s = jnp.eye(ln, dtype=F32)[None, :, None, :, None]
    eye_c = jnp.eye(S5_GROUP, dtype=F32)[None, None, :, None, :]
    tsum = tsum + eye_s * eye_c * d_skip.reshape(g, 1, S5_GROUP, 1, 1)
    tsum = tsum.reshape(g, ln * S5_GROUP, ln * S5_GROUP)

    def w_in(pw_r, pw_i, d):
        re = pw_r[..., None] * bbr[d][None] - pw_i[..., None] * bbi[d][None]
        im = pw_r[..., None] * bbi[d][None] + pw_i[..., None] * bbr[d][None]
        re = re.transpose(1, 0, 3, 2).reshape(g, ln * S5_GROUP, S5_STATE)
        im = im.transpose(1, 0, 3, 2).reshape(g, ln * S5_GROUP, S5_STATE)
        return jnp.concatenate([re, im, im, re], axis=-1)

    win = jnp.concatenate([w_in(pr[:ln, 0][::-1], pi[:ln, 0][::-1], 0),
                           w_in(pr[:ln, 1], pi[:ln, 1], 1)], axis=-1)

    def w_out(cr_d, ci_d):
        re = cr_d.transpose(1, 3, 0, 2).reshape(g, S5_STATE, ln * S5_GROUP)
        im = (-ci_d).transpose(1, 3, 0, 2).reshape(g, S5_STATE, ln * S5_GROUP)
        return jnp.concatenate([re, im], axis=1)

    wout = jnp.concatenate([w_out(cr[1:ln + 1, 0], ci[1:ln + 1, 0]),
                            w_out(cr[1:ln + 1, 1][::-1], ci[1:ln + 1, 1][::-1])], axis=1)
    def cpow(k):
        e = (jnp.asarray(k, F32) * ln)[..., None, None, None]
        mg = jnp.exp(a_re * dt * e)
        return mg * jnp.cos(a_im * dt * e), mg * jnp.sin(a_im * dt * e)

    mr, mi = cpow(jnp.array([1, 2, 4, 8]))
    rows = []
    for d in range(2):
        for i in range(4):
            r_, i_ = mr[i, d], mi[i, d]
            rows += [jnp.concatenate([r_, r_], -1), jnp.concatenate([-i_, i_], -1), jnp.concatenate([i_, -i_], -1)]
        rows += [jnp.zeros_like(rows[0])] * 4
        j = jnp.arange(8)
        qr, qi = cpow(j if d == 0 else 7 - j)
        rows += [jnp.concatenate([qr[k, d], qr[k, d]], -1) for k in range(8)]
        rows += [jnp.concatenate([-qi[k, d], qi[k, d]], -1) for k in range(8)]
    lam = jnp.stack(rows, axis=1)
    return win.astype(BF16), tsum.astype(BF16), wout.astype(BF16), lam


def _s5_in_kernel(u_ref, w_ref, o_ref):
    o_ref[...] = jnp.dot(u_ref[...], w_ref[...], preferred_element_type=F32)


def _s5_in_proj(u, w_s5):
    b, t, d = u.shape
    tm = _big_row_tile(t)
    return pl.pallas_call(
        _s5_in_kernel,
        out_shape=jax.ShapeDtypeStruct((b, t, S5_W), F32),
        grid=(b, t // tm),
        in_specs=[pl.BlockSpec((None, tm, d), lambda bi, i: (bi, i, 0)),
                  pl.BlockSpec((d, S5_W), lambda bi, i: (0, 0))],
        out_specs=pl.BlockSpec((None, tm, S5_W), lambda bi, i: (bi, i, 0)),
        compiler_params=_params("parallel", "arbitrary"),
        name="s5_in_proj",
    )(u, w_s5)


def _s5(u5, s5w, layer, n_ctx):
    b, t, _ = u5.shape
    win, tsum, wout, lam = s5w
    gpb = LANES // S5_GROUP
    nc = t // S5_CHUNK
    kw = S5_CHUNK * S5_GROUP
    assert (n_ctx // S5_CHUNK) % 8 == 0 and nc % 8 == 0
    kern = functools.partial(_s5_kernel, nc=nc, ncc=n_ctx // S5_CHUNK)
    wspec = lambda r, c: pl.BlockSpec((None, gpb, r, c), lambda bi, gb: (layer, gb, 0, 0))
    return pl.pallas_call(
        kern,
        out_shape=jax.ShapeDtypeStruct((b, t, S5_W), F32),
        grid=(b, S5_W // LANES),
        in_specs=[pl.BlockSpec((None, t, LANES), lambda bi, gb: (bi, 0, gb)),
                  wspec(kw, 8 * S5_STATE), wspec(kw, kw), wspec(4 * S5_STATE, kw), wspec(64, 2 * S5_STATE)],
        out_specs=pl.BlockSpec((None, t, LANES), lambda bi, gb: (bi, 0, gb)),
        scratch_shapes=[pltpu.VMEM((gpb, nc, kw), BF16),
                        pltpu.VMEM((gpb, nc, 8 * S5_STATE), F32),
                        pltpu.VMEM((gpb, nc, 4 * S5_STATE), F32)],
        compiler_params=_params("parallel", "parallel"),
        name="s5_scan",
    )(u5, win, tsum, wout, lam)


def _glu_kernel(s_ref, u_ref, wg_ref, w_ref, b_ref, o_ref):
    s = jax.nn.gelu(s_ref[...])
    z = jnp.dot(s.astype(BF16), w_ref[...], preferred_element_type=F32) + b_ref[...]
    g = jnp.dot(u_ref[...], wg_ref[...], preferred_element_type=F32)
    o_ref[...] = (s * jax.nn.sigmoid(z) * (g * jax.nn.sigmoid(g))).astype(BF16)


def _s5_glu_gate(s_pre, u, w_s5, glu_w, glu_b):
    b, t, d = u.shape
    tm = _big_row_tile(t)
    return pl.pallas_call(
        _glu_kernel,
        out_shape=jax.ShapeDtypeStruct((b, t, S5_W), BF16),
        grid=(b, t // tm),
        in_specs=[pl.BlockSpec((None, tm, S5_W), lambda bi, i: (bi, i, 0)),
                  pl.BlockSpec((None, tm, d), lambda bi, i: (bi, i, 0)),
                  pl.BlockSpec((d, S5_W), lambda bi, i: (0, 1)),
                  pl.BlockSpec((S5_W, S5_W), lambda bi, i: (0, 0)),
                  pl.BlockSpec((1, S5_W), lambda bi, i: (0, 0))],
        out_specs=pl.BlockSpec((None, tm, S5_W), lambda bi, i: (bi, i, 0)),
        compiler_params=_params("parallel", "arbitrary"),
        name="s5_glu_gate",
    )(s_pre, u, w_s5, glu_w, glu_b)


def _gate_kernel(u_ref, w_ref, r_ref, a_ref, gn_ref, o_ref):
    j = pl.program_id(2)
    tm = u_ref.shape[0]
    nsplit = 2 if tm % 32 == 0 else 1
    hr = tm // nsplit

    def halves(act_ref, scale):
        for r in range(nsplit):
            rows = slice(r * hr, (r + 1) * hr)
            g = jnp.dot(u_ref[rows, :], w_ref[...], preferred_element_type=F32)
            act = act_ref[rows, :].astype(F32)
            if scale is not None:
                act = act * scale
            o_ref[rows, :] = (act * (g * jax.nn.sigmoid(g))).astype(BF16)

    @pl.when(j < 2)
    def _():
        halves(r_ref, gn_ref[...])

    @pl.when(j >= 2)
    def _():
        halves(a_ref, None)


def _branch_gates(u, wb, layer, r, a, gn_w):
    b, t, d = u.shape
    tm = _big_row_tile(t)
    ret_g0 = 3 * RET_W // COL_TILE
    att_g0 = (4 * RET_W + ATT_W + 2 * ATT_KV_W) // COL_TILE
    act = lambda off: pl.BlockSpec((None, tm, COL_TILE), lambda bi, i, j: (bi, i, jnp.clip(j - off, 0, 1)))
    return pl.pallas_call(
        _gate_kernel,
        out_shape=jax.ShapeDtypeStruct((b, t, RET_W + ATT_W), BF16),
        grid=(b, t // tm, (RET_W + ATT_W) // COL_TILE),
        in_specs=[pl.BlockSpec((None, tm, d), lambda bi, i, j: (bi, i, 0)),
                  pl.BlockSpec((None, d, COL_TILE),
                               lambda bi, i, j: (layer, 0, jnp.where(j < 2, ret_g0 + j, att_g0 + j - 2))),
                  act(0), act(2),
                  pl.BlockSpec((1, COL_TILE), lambda bi, i, j: (0, jnp.clip(j, 0, 1)))],
        out_specs=pl.BlockSpec((None, tm, COL_TILE), lambda bi, i, j: (bi, i, j)),
        compiler_params=_params("parallel", "parallel", "arbitrary"),
        name="branch_gates",
    )(u, wb, r, a, gn_w)


def _merge_kernel(u_ref, zra_ref, zs_ref, wm0_ref, wm1_ref, wm2_ref, wr_ref, wa_ref, ws_ref, o_ref):
    u = u_ref[...]
    acts = (zra_ref[:, :RET_W], zra_ref[:, RET_W:], zs_ref[...])
    acc = None
    for act, wm_ref, wbr_ref in zip(acts, (wm0_ref, wm1_ref, wm2_ref), (wr_ref, wa_ref, ws_ref)):
        gate = jax.nn.sigmoid(jnp.dot(u, wm_ref[...], preferred_element_type=F32))
        proj = jnp.dot(act, wbr_ref[...], preferred_element_type=F32)
        acc = gate * proj if acc is None else acc + gate * proj
    o_ref[...] = acc.astype(BF16)


def _merge(u, z_ra, z_s, wb, layer, w_ret, w_att, w_s5):
    b, t, d = u.shape
    tm = 528 if t % 528 == 0 else _big_row_tile(t)
    m0 = (wb.shape[2] - N_BRANCH * d) // COL_TILE
    per = d // COL_TILE
    wm = lambda br: pl.BlockSpec((None, d, COL_TILE), lambda bi, i, j: (layer, 0, m0 + br * per + j))
    wbr = lambda k: pl.BlockSpec((k, COL_TILE), lambda bi, i, j: (0, j))
    return pl.pallas_call(
        _merge_kernel,
        out_shape=jax.ShapeDtypeStruct((b, t, d), BF16),
        grid=(b, t // tm, d // COL_TILE),
        in_specs=[pl.BlockSpec((None, tm, d), lambda bi, i, j: (bi, i, 0)),
                  pl.BlockSpec((None, tm, RET_W + ATT_W), lambda bi, i, j: (bi, i, 0)),
                  pl.BlockSpec((None, tm, S5_W), lambda bi, i, j: (bi, i, 0)),
                  wm(0), wm(1), wm(2), wbr(RET_W), wbr(ATT_W), wbr(S5_W)],
        out_specs=pl.BlockSpec((None, tm, COL_TILE), lambda bi, i, j: (bi, i, j)),
        compiler_params=_params("parallel", "parallel", "arbitrary"),
        name="gated_merge",
    )(u, z_ra, z_s, wb, wb, wb, w_ret, w_att, w_s5)


def _out_kernel(m_ref, w_ref, c_ref, x_ref, mod_ref, lnw_ref, lnb_ref, o_ref, v_ref, *, alpha, first_tile):
    i = pl.program_id(1)
    d = w_ref.shape[1]

    def run(res_ref):
        m = m_ref[...]
        s1 = jnp.zeros((m.shape[0], 1), F32)
        for c in range(d // COL_TILE):
            cols = slice(c * COL_TILE, (c + 1) * COL_TILE)
            y = jnp.dot(m, w_ref[:, cols], preferred_element_type=F32)
            v = alpha * res_ref[:, cols] + mod_ref[2:3, cols] * y
            v_ref[:, cols] = v
            s1 += jnp.sum(v, axis=-1, keepdims=True)
        mu = s1 * (1.0 / d)
        vc = v_ref[...] - mu
        var = jnp.mean(vc * vc, axis=-1, keepdims=True)
        o_ref[...] = vc * lax.rsqrt(var + LN_EPS) * lnw_ref[...] + lnb_ref[...]

    if first_tile == 0:
        @pl.when(i == 0)
        def _():
            run(c_ref)

        @pl.when(i > 0)
        def _():
            run(x_ref)
    else:
        run(x_ref)


def _out_proj(m, w_out, ctx_src, lat_src, modsel, ln_w, ln_b, alpha, skip_ctx):
    b, t, d = m.shape
    off = 1 if skip_ctx else 0
    nt = t // ROW_TILE - off
    kern = functools.partial(_out_kernel, alpha=alpha, first_tile=off)
    vec = lambda: pl.BlockSpec((1, d), lambda bi, i: (0, 0))
    return pl.pallas_call(
        kern,
        out_shape=jax.ShapeDtypeStruct((b, nt * ROW_TILE, d), F32),
        grid=(b, nt),
        in_specs=[pl.BlockSpec((None, ROW_TILE, d), lambda bi, i: (bi, i + off, 0)),
                  pl.BlockSpec((d, d), lambda bi, i: (0, 0)),
                  *_token_specs(ctx_src, lat_src, off),
                  pl.BlockSpec((None, None, 3, d), lambda bi, i: (bi, jnp.minimum(i + off, 1), 0, 0)),
                  vec(), vec()],
        out_specs=pl.BlockSpec((None, ROW_TILE, d), lambda bi, i: (bi, i, 0)),
        scratch_shapes=[pltpu.VMEM((ROW_TILE, d), F32)],
        compiler_params=_params("parallel", "arbitrary"),
        name="out_proj_norm",
    )(m, w_out, ctx_src, lat_src, modsel, ln_w.reshape(1, d), ln_b.reshape(1, d))


def _rope_tables(n, n_ctx):
    rows = n // GRID_W
    row = jnp.repeat(jnp.arange(rows, dtype=F32), GRID_W)
    col = jnp.tile(jnp.arange(GRID_W, dtype=F32), rows)

    def table(head_dim):
        per_axis = head_dim // 4
        inv = ROPE_THETA ** (-jnp.arange(per_axis, dtype=F32) / per_axis)
        ang = jnp.concatenate([row[:, None] * inv, col[:, None] * inv], axis=-1)
        cos = jnp.concatenate([jnp.ones((n_ctx, head_dim // 2), F32), jnp.cos(ang)], axis=0)
        sin = jnp.concatenate([jnp.zeros((n_ctx, head_dim // 2), F32), jnp.sin(ang)], axis=0)
        return cos, sin

    cos_r, sin_r = table(RET_DK)
    cos_a, sin_a = table(ATT_HD)
    return (cos_r, sin_r, jnp.concatenate([cos_a, cos_a], -1), jnp.concatenate([-sin_a, sin_a], -1))


def kernel(x, c, ctx, c_ctx, ada_w, ada_b, w_in, ret_log_decay, ret_gn_w, att_q_norm, att_k_norm, s5_a_re, s5_a_im, s5_log_dt, s5_b_re, s5_b_im, s5_c_re, s5_c_im, s5_d, s5_glu_w, s5_glu_b, w_br_ret, w_br_att, w_br_s5, w_out, ln_w, ln_b):
    b, n, d = x.shape
    n_ctx = ctx.shape[1]
    depth = w_in.shape[0]
    assert n_ctx == ROW_TILE and n % ROW_TILE == 0 and n % GRID_W == 0
    assert w_in.shape[2] == 4 * RET_W + 2 * ATT_W + 2 * ATT_KV_W + 2 * S5_W + N_BRANCH * d
    alpha = (2.0 * depth) ** 0.25
    s5_col0 = 4 * RET_W + 2 * ATT_W + 2 * ATT_KV_W

    t = n_ctx + n
    ctx_src, lat_src = ctx, x
    cos_r, sin_r, cos_a, sin_a = _rope_tables(n, n_ctx)
    crows = 16
    cvec = jnp.concatenate([c, c_ctx[None, :], jnp.zeros((crows - b - 1, d), F32)], axis=0)
    wb = w_in.astype(BF16)
    s5w = jax.vmap(_s5_weights)(s5_a_re, s5_a_im, s5_log_dt, s5_b_re, s5_b_im, s5_c_re, s5_c_im, s5_d)

    for l in range(depth):
        last = l == depth - 1
        mod = _modulation(cvec, ada_w, ada_b, l).reshape(crows, 3, d)
        modsel = jnp.stack([jnp.broadcast_to(mod[b][None], (b, 3, d)), mod[:b]], axis=1)
        w_s5 = wb[l, :, s5_col0:s5_col0 + 2 * S5_W]

        u = _modulate(ctx_src, lat_src, t, modsel)
        mix = _mix_proj(u, wb, l, cos_r, sin_r, cos_a, sin_a,
                        att_q_norm[l].reshape(1, ATT_HD), att_k_norm[l].reshape(1, ATT_HD))
        r = _retention(mix, ret_log_decay[l], n_ctx)
        a = _attention(mix, n_ctx)
        s_pre = _s5(_s5_in_proj(u, w_s5), s5w, l, n_ctx)
        z_s = _s5_glu_gate(s_pre, u, w_s5, s5_glu_w[l].astype(BF16), s5_glu_b[l].reshape(1, S5_W))
        z_ra = _branch_gates(u, wb, l, r, a, ret_gn_w[l].reshape(1, RET_W))
        m = _merge(u, z_ra, z_s, wb, l, w_br_ret[l].astype(BF16), w_br_att[l].astype(BF16),
                   w_br_s5[l].astype(BF16))
        xa = _out_proj(m, w_out[l].astype(BF16), ctx_src, lat_src, modsel, ln_w[l], ln_b[l], alpha, last)
        ctx_src = lat_src = xa
    return xa
```

```python
import functools
import math

import jax
import jax.numpy as jnp
from jax import lax
from jax.experimental import pallas as pl
from jax.experimental.pallas import tpu as pltpu

F32 = jnp.float32
BF16 = jnp.bfloat16

GRID_W = 64
RET_HEADS = 4
RET_DK = 256
RET_DV = 256
RET_W = RET_HEADS * RET_DV
ATT_HEADS = 8
ATT_KV_HEADS = 2
ATT_GROUP = ATT_HEADS // ATT_KV_HEADS
ATT_HD = 128
ATT_W = ATT_HEADS * ATT_HD
ATT_KV_W = ATT_KV_HEADS * ATT_HD
ROPE_THETA = 10000.0
S5_GROUP = 16
S5_W = 768
S5_GROUPS = S5_W // S5_GROUP
S5_STATE = 64
N_BRANCH = 3
LN_EPS = 1e-6
RMS_EPS = 1e-6

LANES = 128
VMEM_LIMIT_BYTES = 56 * 1024 * 1024

CHUNK = 128
S5_CHUNK = 16
ROW_TILE = 256
COL_TILE = 512
MIX_W = 4608
ATT_SUB = 2048
LOG2E = math.log2(math.e)


def _params(*sem):
    return pltpu.CompilerParams(dimension_semantics=sem, vmem_limit_bytes=VMEM_LIMIT_BYTES)


def _big_row_tile(t):
    for cand in (1056, 1024, 768, 640, 512, 256):
        if t % cand == 0:
            return cand
    raise ValueError(f"unsupported token count {t}")


def _mod_kernel(c_ref, w_ref, b_ref, o_ref):
    c = c_ref[...]
    s = c * jax.nn.sigmoid(c)
    s_hi = s.astype(BF16)
    s_lo = (s - s_hi.astype(F32)).astype(BF16)
    w = w_ref[...]
    w_hi = w.astype(BF16)
    w_lo = (w - w_hi.astype(F32)).astype(BF16)
    acc = jnp.dot(s_hi, w_hi, preferred_element_type=F32)
    acc += jnp.dot(s_lo, w_hi, preferred_element_type=F32)
    acc += jnp.dot(s_hi, w_lo, preferred_element_type=F32)
    o_ref[...] = acc + b_ref[...]


def _modulation(cvec, ada_w, ada_b, layer):
    rows, d = cvec.shape
    n = ada_w.shape[2]
    tn = 768
    return pl.pallas_call(
        _mod_kernel,
        out_shape=jax.ShapeDtypeStruct((rows, n), F32),
        grid=(n // tn,),
        in_specs=[pl.BlockSpec((rows, d), lambda j: (0, 0)),
                  pl.BlockSpec((None, d, tn), lambda j: (layer, 0, j)),
                  pl.BlockSpec((None, 1, tn), lambda j: (layer, 0, j))],
        out_specs=pl.BlockSpec((rows, tn), lambda j: (0, j)),
        compiler_params=_params("arbitrary"),
        name="adaln_modulation",
    )(cvec, ada_w, ada_b.reshape(ada_b.shape[0], 1, n))


def _modulate_kernel(c_ref, x_ref, mod_ref, o_ref):
    i = pl.program_id(1)
    m = mod_ref[...]

    @pl.when(i == 0)
    def _():
        o_ref[...] = (c_ref[...] * (1.0 + m[1:2]) + m[0:1]).astype(BF16)

    @pl.when(i > 0)
    def _():
        o_ref[...] = (x_ref[...] * (1.0 + m[1:2]) + m[0:1]).astype(BF16)


def _token_specs(ctx_src, lat_src, first_tile=0):
    d = ctx_src.shape[2]
    shift = first_tile - (0 if lat_src is ctx_src else 1)
    return (pl.BlockSpec((None, ROW_TILE, d), lambda bi, i: (bi, 0, 0)),
            pl.BlockSpec((None, ROW_TILE, d), lambda bi, i: (bi, jnp.maximum(i + shift, 0), 0)))


def _modulate(ctx_src, lat_src, t, modsel):
    b, _, d = ctx_src.shape
    return pl.pallas_call(
        _modulate_kernel,
        out_shape=jax.ShapeDtypeStruct((b, t, d), BF16),
        grid=(b, t // ROW_TILE),
        in_specs=[*_token_specs(ctx_src, lat_src),
                  pl.BlockSpec((None, None, 3, d), lambda bi, i: (bi, jnp.minimum(i, 1), 0, 0))],
        out_specs=pl.BlockSpec((None, ROW_TILE, d), lambda bi, i: (bi, i, 0)),
        compiler_params=_params("parallel", "arbitrary"),
        name="modulate",
    )(ctx_src, lat_src, modsel)


def _mix_kernel(u_ref, w_ref, cr_ref, sr_ref, ca_ref, sa_ref, qn_ref, kn_ref, o_ref, s_ref):
    j = pl.program_id(2)
    tm = u_ref.shape[0]
    nsplit = 2 if tm % 32 == 0 else 1
    hr = tm // nsplit

    def halves(epilogue):
        for r in range(nsplit):
            rows = slice(r * hr, (r + 1) * hr)
            epilogue(rows, jnp.dot(u_ref[rows, :], w_ref[...], preferred_element_type=F32))

    def rms_rope(x, rows, w, scale):
        ms = jnp.mean(x * x, axis=-1, keepdims=True)
        y = x * lax.rsqrt(ms + RMS_EPS) * w
        y = y * ca_ref[rows, :] + pltpu.roll(y, ATT_HD // 2, 1) * sa_ref[rows, :]
        return (y * scale).astype(BF16)

    @pl.when(j < 4)
    def _():
        scale = jnp.where(j >= 2, RET_DK ** -0.5, 1.0).astype(F32)

        def epilogue(rows, acc):
            cos = cr_ref[rows, :]
            sin = sr_ref[rows, :]
            for h in range(COL_TILE // RET_DK):
                x1 = acc[:, h * RET_DK: h * RET_DK + LANES]
                x2 = acc[:, h * RET_DK + LANES: (h + 1) * RET_DK]
                o_ref[rows, h * RET_DK: h * RET_DK + LANES] = ((x1 * cos - x2 * sin) * scale).astype(BF16)
                o_ref[rows, h * RET_DK + LANES: (h + 1) * RET_DK] = ((x2 * cos + x1 * sin) * scale).astype(BF16)

        halves(epilogue)

    @pl.when((j == 4) | (j == 5))
    def _():
        def epilogue(rows, acc):
            o_ref[rows, :] = acc.astype(BF16)

        halves(epilogue)

    @pl.when((j == 6) | (j == 7))
    def _():
        def epilogue(rows, acc):
            for h in range(COL_TILE // ATT_HD):
                o_ref[rows, h * ATT_HD:(h + 1) * ATT_HD] = rms_rope(
                    acc[:, h * ATT_HD:(h + 1) * ATT_HD], rows, qn_ref[...], ATT_HD ** -0.5 * LOG2E)

        halves(epilogue)

    @pl.when(j == 8)
    def _():
        def epilogue(rows, acc):
            for h in range(ATT_KV_HEADS):
                o_ref[rows, h * ATT_HD:(h + 1) * ATT_HD] = rms_rope(
                    acc[:, h * ATT_HD:(h + 1) * ATT_HD], rows, kn_ref[...], 1.0)
            o_ref[rows, ATT_KV_W:] = acc[:, ATT_KV_W:].astype(BF16)

        halves(epilogue)

    def silu(g):
        return g * jax.nn.sigmoid(g)

    @pl.when(j == 9)
    def _():
        def epilogue(rows, acc):
            s_ref[rows, :] = acc

        halves(epilogue)

    @pl.when(j == 10)
    def _():
        def epilogue(rows, acc):
            split = S5_W - COL_TILE
            s_ref[rows, :split] = acc[:, :split]
            s_ref[rows, split:] = silu(acc[:, split:])

        halves(epilogue)

    @pl.when(j == 11)
    def _():
        def epilogue(rows, acc):
            s_ref[rows, :] = silu(acc)

        halves(epilogue)


def _mix_proj(u, wb, layer, cos_r, sin_r, cos_a, sin_a, qn, kn):
    b, t, d = u.shape
    tm = _big_row_tile(t)
    tab = lambda: pl.BlockSpec((tm, LANES), lambda bi, i, j: (i, 0))
    vec = lambda: pl.BlockSpec((1, LANES), lambda bi, i, j: (0, 0))
    n_mix = MIX_W // COL_TILE
    n_s5 = 2 * S5_W // COL_TILE
    gap_g = RET_W // COL_TILE
    gap_a = ATT_W // COL_TILE

    def w_tile(j):
        return jnp.where(j >= n_mix, j + gap_g + gap_a, jnp.where(j >= 6, j + gap_g, j))

    return pl.pallas_call(
        _mix_kernel,
        out_shape=(jax.ShapeDtypeStruct((b, t, MIX_W), BF16), jax.ShapeDtypeStruct((b, t, 2 * S5_W), F32)),
        grid=(b, t // tm, n_mix + n_s5),
        in_specs=[pl.BlockSpec((None, tm, d), lambda bi, i, j: (bi, i, 0)),
                  pl.BlockSpec((None, d, COL_TILE), lambda bi, i, j: (layer, 0, w_tile(j))),
                  tab(), tab(), tab(), tab(), vec(), vec()],
        out_specs=(pl.BlockSpec((None, tm, COL_TILE), lambda bi, i, j: (bi, i, jnp.minimum(j, n_mix - 1))),
                   pl.BlockSpec((None, tm, COL_TILE), lambda bi, i, j: (bi, i, jnp.maximum(j - n_mix, 0)))),
        compiler_params=_params("parallel", "parallel", "arbitrary"),
        name="mixer_in_proj",
    )(u, wb, cos_r, sin_r, cos_a, sin_a, qn, kn)


def _ret_kernel(ld_ref, q_ref, k_ref, v_ref, o_ref, acc_ref, sf_ref, sb_ref, *, n_ctx_chunks, n_chunks):
    h = pl.program_id(1)
    lgf = ld_ref[0, h]
    lgb = ld_ref[1, h]
    c = CHUNK
    ri = lax.broadcasted_iota(jnp.int32, (c, c), 0).astype(F32)
    ci = lax.broadcasted_iota(jnp.int32, (c, c), 1).astype(F32)
    rel = ri - ci
    mask = jnp.where(rel >= 0, jnp.exp(lgf * jnp.maximum(rel, 0.0)), jnp.exp(lgb * jnp.maximum(-rel, 0.0)))
    row = lax.broadcasted_iota(jnp.int32, (c, RET_DV), 0).astype(F32)
    qdec_f = jnp.exp(lgf * (row + 1.0))
    qdec_b = jnp.exp(lgb * (c - row))
    kdec_f = jnp.exp(lgf * (c - 1.0 - row))
    kdec_b = jnp.exp(lgb * row)
    cdec_f = jnp.exp(jnp.full((1, RET_DV), lgf * c, F32))
    cdec_b = jnp.exp(jnp.full((1, RET_DV), lgb * c, F32))
    contract0 = (((0,), (0,)), ((), ()))
    contract1 = (((1,), (1,)), ((), ()))

    def load(ci_):
        r0 = pl.multiple_of(ci_ * c, c)
        rows = pl.ds(r0, c)
        return rows, q_ref[rows, :], k_ref[rows, :], v_ref[rows, :]

    def bwd(ci_):
        rows, q, k, v = load(ci_)
        s = sb_ref[...]
        acc_ref[rows, :] += jnp.dot(q, s.astype(BF16), preferred_element_type=F32) * qdec_b
        kd = (k.astype(F32) * kdec_b).astype(BF16)
        sb_ref[...] = s * cdec_b + lax.dot_general(kd, v, contract0, preferred_element_type=F32)

    def fwd(ci_):
        rows, q, k, v = load(ci_)
        s = sf_ref[...]
        sc = lax.dot_general(q, k, contract1, preferred_element_type=F32) * mask
        o = jnp.dot(sc.astype(BF16), v, preferred_element_type=F32)
        o += jnp.dot(q, s.astype(BF16), preferred_element_type=F32) * qdec_f
        acc_ref[rows, :] += o
        kd = (k.astype(F32) * kdec_f).astype(BF16)
        sf_ref[...] = s * cdec_f + lax.dot_general(kd, v, contract0, preferred_element_type=F32)

    sf_ref[...] = jnp.zeros_like(sf_ref)
    sb_ref[...] = jnp.zeros_like(sb_ref)
    acc_ref[...] = jnp.zeros_like(acc_ref)

    def ctx_part(t, carry):
        fwd(t)
        bwd(n_ctx_chunks - 1 - t)
        return carry

    def lat_part(t, carry):
        fwd(t)
        bwd(n_chunks - 1 - (t - n_ctx_chunks))
        return carry

    lax.fori_loop(0, n_ctx_chunks, ctx_part, 0, unroll=2)
    lax.fori_loop(n_ctx_chunks, n_chunks, lat_part, 0, unroll=4)

    def norm(t, carry):
        rows = pl.ds(pl.multiple_of(t * c, c), c)
        o = acc_ref[rows, :]
        mu = jnp.mean(o, axis=-1, keepdims=True)
        oc = o - mu
        var = jnp.mean(oc * oc, axis=-1, keepdims=True)
        o_ref[rows, :] = (oc * lax.rsqrt(var + LN_EPS)).astype(BF16)
        return carry

    lax.fori_loop(0, n_chunks, norm, 0, unroll=6)


def _retention(mix, log_decay, n_ctx):
    b, t, _ = mix.shape
    kern = functools.partial(_ret_kernel, n_ctx_chunks=n_ctx // CHUNK, n_chunks=t // CHUNK)
    blk = lambda off: pl.BlockSpec((None, t, RET_DK), lambda bi, h: (bi, 0, off + h))
    return pl.pallas_call(
        kern,
        out_shape=jax.ShapeDtypeStruct((b, t, RET_W), BF16),
        grid=(b, RET_HEADS),
        in_specs=[pl.BlockSpec(memory_space=pltpu.SMEM), blk(0), blk(RET_HEADS), blk(2 * RET_HEADS)],
        out_specs=pl.BlockSpec((None, t, RET_DV), lambda bi, h: (bi, 0, h)),
        scratch_shapes=[pltpu.VMEM((t, RET_DV), F32), pltpu.VMEM((RET_DK, RET_DV), F32),
                        pltpu.VMEM((RET_DK, RET_DV), F32)],
        compiler_params=_params("parallel", "arbitrary"),
        name="retention",
    )(log_decay, mix, mix, mix)


def _att_kernel(q_ref, k_ref, v_ref, o_ref, v1_ref, q4_ref, s_ref, mx_ref, m_ref, acc_ref, *, chunks, n_ctx, t):
    i = pl.program_id(2)
    contract1 = (((1,), (1,)), ((), ()))
    tq = q_ref.shape[0]
    sb = ATT_SUB

    @pl.when(i == 0)
    def _():
        v1_ref[:, :ATT_HD] = v_ref[...]
        v1_ref[:, ATT_HD:] = jnp.ones((t, ATT_HD), BF16)

    for g in range(ATT_GROUP):
        q4_ref[g * tq:(g + 1) * tq, :] = q_ref[:, g * ATT_HD:(g + 1) * ATT_HD]
    neg = jnp.full(mx_ref.shape, -jnp.inf, F32)
    mx_ref[...] = neg
    m_ref[...] = neg
    acc_ref[...] = jnp.zeros(acc_ref.shape, F32)

    def a_step(buf, j, r0, size):
        k = k_ref[pl.ds(r0, size), :]
        s = lax.dot_general(q4_ref[...], k, contract1, preferred_element_type=F32)
        s_ref[buf, j, :, 0:size] = s
        mx = mx_ref[...]
        for tt in range(size // LANES):
            mx = jnp.maximum(mx, s[:, tt * LANES:(tt + 1) * LANES])
        mx_ref[...] = mx

    def settle():
        mx = mx_ref[...]
        mc = jnp.broadcast_to(jnp.max(mx, axis=-1, keepdims=True), mx.shape)
        m_old = m_ref[...]
        m_new = jnp.maximum(m_old, mc)
        alpha = jnp.exp2(m_old - m_new)
        m_ref[...] = m_new
        acc_ref[:, :ATT_HD] = acc_ref[:, :ATT_HD] * alpha
        acc_ref[:, ATT_HD:] = acc_ref[:, ATT_HD:] * alpha
        mx_ref[...] = neg

    def b_step(buf, j, r0, size):
        mb = m_ref[...]
        p = jnp.concatenate([jnp.exp2(s_ref[buf, j, :, tt * LANES:(tt + 1) * LANES] - mb)
                             for tt in range(size // LANES)], axis=1).astype(BF16)
        acc_ref[...] += jnp.dot(p, v1_ref[pl.ds(r0, size), :], preferred_element_type=F32)

    def stage(ca, cb):
        sides = [(a_step, ca), (b_step, cb)]
        sides = [(fn, ch) for fn, ch in sides if ch is not None]
        common = min(ch[2] // sb for _, ch in sides)

        def body(j, carry):
            off = pl.multiple_of(j * sb, sb)
            for fn, (buf, start, _) in sides:
                fn(buf, j, start + off, sb)
            return carry

        if common:
            lax.fori_loop(0, common, body, 0)
        for fn, (buf, start, size) in sides:
            for j in range(common, -(-size // sb)):
                fn(buf, j, start + j * sb, min(sb, size - j * sb))
        if ca is not None:
            settle()

    @pl.when(i == 0)
    def _():
        stage((0, 0, n_ctx), None)
        stage(None, (0, 0, n_ctx))

    @pl.when(i > 0)
    def _():
        bufs = [(c % 2, start, size) for c, (start, size) in enumerate(chunks)]
        for c in range(len(bufs) + 1):
            stage(bufs[c] if c < len(bufs) else None, bufs[c - 1] if c > 0 else None)

    for g in range(ATT_GROUP):
        rows = slice(g * tq, (g + 1) * tq)
        o_ref[:, g * ATT_HD:(g + 1) * ATT_HD] = (acc_ref[rows, :ATT_HD] / acc_ref[rows, ATT_HD:]).astype(BF16)


def _attention(mix, n_ctx):
    b, t, _ = mix.shape
    tq = ROW_TILE
    n_lat = t - n_ctx
    nch = 4 if n_lat % (4 * COL_TILE) == 0 else 1
    chunks = tuple((0, n_ctx + n_lat // nch) if c == 0 else (n_ctx + c * (n_lat // nch), n_lat // nch)
                   for c in range(nch))
    mrows = ATT_GROUP * tq
    q_blk0 = (3 * RET_W) // (ATT_GROUP * ATT_HD)
    k_blk0 = (3 * RET_W + ATT_W) // ATT_HD
    v_blk0 = k_blk0 + ATT_KV_HEADS
    kern = functools.partial(_att_kernel, chunks=chunks, n_ctx=n_ctx, t=t)
    return pl.pallas_call(
        kern,
        out_shape=jax.ShapeDtypeStruct((b, t, ATT_W), BF16),
        grid=(b, ATT_KV_HEADS, t // tq),
        in_specs=[pl.BlockSpec((None, tq, ATT_GROUP * ATT_HD), lambda bi, hk, i: (bi, i, q_blk0 + hk)),
                  pl.BlockSpec((None, t, ATT_HD), lambda bi, hk, i: (bi, 0, k_blk0 + hk)),
                  pl.BlockSpec((None, t, ATT_HD), lambda bi, hk, i: (bi, 0, v_blk0 + hk))],
        out_specs=pl.BlockSpec((None, tq, ATT_GROUP * ATT_HD), lambda bi, hk, i: (bi, i, hk)),
        scratch_shapes=[pltpu.VMEM((t, 2 * ATT_HD), BF16),
                        pltpu.VMEM((mrows, ATT_HD), BF16),
                        pltpu.VMEM((2, -(-chunks[0][1] // ATT_SUB), mrows, ATT_SUB), F32),
                        pltpu.VMEM((mrows, LANES), F32),
                        pltpu.VMEM((mrows, LANES), F32),
                        pltpu.VMEM((mrows, 2 * ATT_HD), F32)],
        compiler_params=_params("parallel", "parallel", "arbitrary"),
        name="gqa_attention",
    )(mix, mix, mix)


def _s5_kernel(u_ref, win_ref, tsum_ref, wout_ref, lam_ref, y_ref, uc_ref, d_ref, xin_ref, yw_ref, *, nc, ncc):
    p2 = 2 * S5_STATE
    sub = 8
    ln = S5_CHUNK
    gpb = LANES // S5_GROUP
    us = [pltpu.bitcast(u_ref[pl.ds(s, nc, stride=ln), :].astype(BF16), jnp.uint32) for s in range(ln)]
    for g in range(gpb):
        uc = pltpu.bitcast(
            jnp.concatenate([us[s][:, g * S5_GROUP:(g + 1) * S5_GROUP] for s in range(ln)], axis=1), BF16)
        uc_ref[g] = uc
        d_ref[g] = jnp.dot(uc, win_ref[g], preferred_element_type=F32)
    rowid = lax.broadcasted_iota(jnp.int32, (sub, p2), 0)

    def tile(g, reverse, rows, c, cs):
        base = 32 if reverse else 0
        col = 2 * p2 if reverse else 0

        def mult(i):
            return tuple(jnp.broadcast_to(lam_ref[g, base + 3 * i + r:base + 3 * i + r + 1, :], (sub, p2))
                         for r in range(3))

        def shifted(v, n):
            if reverse:
                return jnp.where(rowid < sub - n, pltpu.roll(v, sub - n, 0), 0.0)
            return jnp.where(rowid >= n, pltpu.roll(v, n, 0), 0.0)

        x = d_ref[g, rows, col:col + p2]
        xs = d_ref[g, rows, col + p2:col + 2 * p2]
        for i, n in enumerate((1, 2, 4)):
            a, b, bs = mult(i)
            xr, xsr = shifted(x, n), shifted(xs, n)
            x, xs = x + a * xr + b * xsr, xs + a * xsr + bs * xr
        pa = lam_ref[g, base + 16:base + 24, :]
        pb = lam_ref[g, base + 24:base + 32, :]
        xin_ref[g, rows, col // 2:col // 2 + p2] = shifted(x, 1) + pa * c + pb * cs
        edge = 0 if reverse else sub - 1
        last = jnp.broadcast_to(x[edge:edge + 1, :], (sub, p2))
        lasts = jnp.broadcast_to(xs[edge:edge + 1, :], (sub, p2))
        a, b, bs = mult(3)
        return a * c + b * cs + last, a * cs + bs * c + lasts

    nt = nc // sub
    nct = ncc // sub

    def step(mf, mb, carry):
        rf = pl.ds(pl.multiple_of(mf * sub, sub), sub)
        rb = pl.ds(pl.multiple_of(mb * sub, sub), sub)
        return tuple(tile(g, False, rf, carry[g][0], carry[g][1]) + tile(g, True, rb, carry[g][2], carry[g][3])
                     for g in range(gpb))

    zero = jnp.zeros((sub, p2), F32)
    init = tuple((zero, zero, zero, zero) for _ in range(gpb))
    carry = lax.fori_loop(0, nct, lambda t, cr: step(t, nct - 1 - t, cr), init)
    lax.fori_loop(nct, nt, lambda t, cr: step(t, nt - 1 - (t - nct), cr), carry)

    for g in range(gpb):
        y = jnp.dot(uc_ref[g], tsum_ref[g], preferred_element_type=F32)
        y += jnp.dot(xin_ref[g].astype(BF16), wout_ref[g], preferred_element_type=F32)
        yw_ref[g] = pltpu.bitcast(y.astype(BF16), jnp.uint32)
    for s in range(ln):
        piece = jnp.concatenate([yw_ref[g, :, s * S5_GROUP:(s + 1) * S5_GROUP] for g in range(gpb)], axis=1)
        y_ref[pl.ds(s, nc, stride=ln), :] = pltpu.bitcast(piece, BF16).astype(F32)


def _s5_weights(a_re, a_im, log_dt, b_re, b_im, c_re, c_im, d_skip):
    hi = lax.Precision.HIGHEST
    ln = S5_CHUNK
    dt = jnp.exp(log_dt)[..., None]
    mag = jnp.exp(a_re * dt)
    abr, abi = mag * jnp.cos(a_im * dt), mag * jnp.sin(a_im * dt)
    den = a_re * a_re + a_im * a_im
    fr = ((abr - 1.0) * a_re + abi * a_im) / den
    fi = (abi * a_re - (abr - 1.0) * a_im) / den
    bbr = fr[..., None] * b_re - fi[..., None] * b_im
    bbi = fr[..., None] * b_im + fi[..., None] * b_re
    n = jnp.arange(ln + 1, dtype=F32)[:, None, None, None]
    pmag = jnp.exp(a_re * dt * n)
    pr, pi = pmag * jnp.cos(a_im * dt * n), pmag * jnp.sin(a_im * dt * n)
    cr = c_re[None] * pr[:, :, :, None, :] - c_im[None] * pi[:, :, :, None, :]
    ci = c_re[None] * pi[:, :, :, None, :] + c_im[None] * pr[:, :, :, None, :]
    g_ = a_re.shape[1]
    lhs = jnp.concatenate([cr[:ln], -ci[:ln]], axis=-1).transpose(1, 2, 0, 3, 4)
    lhs = lhs.reshape(2, g_, ln * S5_GROUP, 2 * S5_STATE)
    rhs = jnp.concatenate([bbr, bbi], axis=2)
    m = jnp.einsum('dgmk,dgke->dgme', lhs, rhs, precision=hi)
    m = m.reshape(2, g_, ln, S5_GROUP, S5_GROUP).transpose(2, 0, 1, 3, 4)
    lag = jnp.arange(ln)[None, :] - jnp.arange(ln)[:, None]
    n_idx = jnp.arange(ln)[:, None, None]
    oh_f = (lag[None] == n_idx).astype(F32)
    oh_b = (-lag[None] == n_idx).astype(F32)
    tsum = (jnp.einsum('nsi,ngce->gseic', oh_f, m[:, 0], precision=hi)
            + jnp.einsum('nsi,ngce->gseic', oh_b, m[:, 1], precision=hi))
    g = a_re.shape[1]
    eye_s = jnp.eye(ln, dtype=F32)[None, :, None, :, None]
    eye_c = jnp.eye(S5_GROUP, dtype=F32)[None, None, :, None, :]
    tsum = tsum + eye_s * eye_c * d_skip.reshape(g, 1, S5_GROUP, 1, 1)
    tsum = tsum.reshape(g, ln * S5_GROUP, ln * S5_GROUP)

    def w_in(pw_r, pw_i, d):
        re = pw_r[..., None] * bbr[d][None] - pw_i[..., None] * bbi[d][None]
        im = pw_r[..., None] * bbi[d][None] + pw_i[..., None] * bbr[d][None]
        re = re.transpose(1, 0, 3, 2).reshape(g, ln * S5_GROUP, S5_STATE)
        im = im.transpose(1, 0, 3, 2).reshape(g, ln * S5_GROUP, S5_STATE)
        return jnp.concatenate([re, im, im, re], axis=-1)

    win = jnp.concatenate([w_in(pr[:ln, 0][::-1], pi[:ln, 0][::-1], 0),
                           w_in(pr[:ln, 1], pi[:ln, 1], 1)], axis=-1)

    def w_out(cr_d, ci_d):
        re = cr_d.transpose(1, 3, 0, 2).reshape(g, S5_STATE, ln * S5_GROUP)
        im = (-ci_d).transpose(1, 3, 0, 2).reshape(g, S5_STATE, ln * S5_GROUP)
        return jnp.concatenate([re, im], axis=1)

    wout = jnp.concatenate([w_out(cr[1:ln + 1, 0], ci[1:ln + 1, 0]),
                            w_out(cr[1:ln + 1, 1][::-1], ci[1:ln + 1, 1][::-1])], axis=1)
    def cpow(k):
        e = (jnp.asarray(k, F32) * ln)[..., None, None, None]
        mg = jnp.exp(a_re * dt * e)
        return mg * jnp.cos(a_im * dt * e), mg * jnp.sin(a_im * dt * e)

    mr, mi = cpow(jnp.array([1, 2, 4, 8]))
    rows = []
    for d in range(2):
        for i in range(4):
            r_, i_ = mr[i, d], mi[i, d]
            rows += [jnp.concatenate([r_, r_], -1), jnp.concatenate([-i_, i_], -1), jnp.concatenate([i_, -i_], -1)]
        rows += [jnp.zeros_like(rows[0])] * 4
        j = jnp.arange(8)
        qr, qi = cpow(j if d == 0 else 7 - j)
        rows += [jnp.concatenate([qr[k, d], qr[k, d]], -1) for k in range(8)]
        rows += [jnp.concatenate([-qi[k, d], qi[k, d]], -1) for k in range(8)]
    lam = jnp.stack(rows, axis=1)
    return win.astype(BF16), tsum.astype(BF16), wout.astype(BF16), lam


def _s5(u5, s5w, layer, n_ctx):
    b, t, _ = u5.shape
    win, tsum, wout, lam = s5w
    gpb = LANES // S5_GROUP
    nc = t // S5_CHUNK
    kw = S5_CHUNK * S5_GROUP
    assert (n_ctx // S5_CHUNK) % 8 == 0 and nc % 8 == 0
    kern = functools.partial(_s5_kernel, nc=nc, ncc=n_ctx // S5_CHUNK)
    wspec = lambda r, c: pl.BlockSpec((None, gpb, r, c), lambda bi, gb: (layer, gb, 0, 0))
    return pl.pallas_call(
        kern,
        out_shape=jax.ShapeDtypeStruct((b, t, S5_W), F32),
        grid=(b, S5_W // LANES),
        in_specs=[pl.BlockSpec((None, t, LANES), lambda bi, gb: (bi, 0, gb)),
                  wspec(kw, 8 * S5_STATE), wspec(kw, kw), wspec(4 * S5_STATE, kw), wspec(64, 2 * S5_STATE)],
        out_specs=pl.BlockSpec((None, t, LANES), lambda bi, gb: (bi, 0, gb)),
        scratch_shapes=[pltpu.VMEM((gpb, nc, kw), BF16),
                        pltpu.VMEM((gpb, nc, 8 * S5_STATE), F32),
                        pltpu.VMEM((gpb, nc, 4 * S5_STATE), F32),
                        pltpu.VMEM((gpb, nc // 2, kw), jnp.uint32)],
        compiler_params=_params("parallel", "parallel"),
        name="s5_scan",
    )(u5, win, tsum, wout, lam)


def _glu_kernel(s_ref, g_ref, w_ref, b_ref, o_ref):
    s = jax.nn.gelu(s_ref[...])
    z = jnp.dot(s.astype(BF16), w_ref[...], preferred_element_type=F32) + b_ref[...]
    o_ref[...] = (s * jax.nn.sigmoid(z) * g_ref[...]).astype(BF16)


def _s5_glu_gate(s_pre, s5_stream, glu_w, glu_b):
    b, t, _ = s_pre.shape
    tm = _big_row_tile(t)
    return pl.pallas_call(
        _glu_kernel,
        out_shape=jax.ShapeDtypeStruct((b, t, S5_W), BF16),
        grid=(b, t // tm),
        in_specs=[pl.BlockSpec((None, tm, S5_W), lambda bi, i: (bi, i, 0)),
                  pl.BlockSpec((None, tm, S5_W), lambda bi, i: (bi, i, 1)),
                  pl.BlockSpec((S5_W, S5_W), lambda bi, i: (0, 0)),
                  pl.BlockSpec((1, S5_W), lambda bi, i: (0, 0))],
        out_specs=pl.BlockSpec((None, tm, S5_W), lambda bi, i: (bi, i, 0)),
        compiler_params=_params("parallel", "arbitrary"),
        name="s5_glu_gate",
    )(s_pre, s5_stream, glu_w, glu_b)


def _gate_kernel(u_ref, w_ref, r_ref, a_ref, gn_ref, o_ref):
    j = pl.program_id(2)
    tm = u_ref.shape[0]
    nsplit = 2 if tm % 32 == 0 else 1
    hr = tm // nsplit

    def halves(act_ref, scale):
        for r in range(nsplit):
            rows = slice(r * hr, (r + 1) * hr)
            g = jnp.dot(u_ref[rows, :], w_ref[...], preferred_element_type=F32)
            act = act_ref[rows, :].astype(F32)
            if scale is not None:
                act = act * scale
            o_ref[rows, :] = (act * (g * jax.nn.sigmoid(g))).astype(BF16)

    @pl.when(j < 2)
    def _():
        halves(r_ref, gn_ref[...])

    @pl.when(j >= 2)
    def _():
        halves(a_ref, None)


def _branch_gates(u, wb, layer, r, a, gn_w):
    b, t, d = u.shape
    tm = _big_row_tile(t)
    ret_g0 = 3 * RET_W // COL_TILE
    att_g0 = (4 * RET_W + ATT_W + 2 * ATT_KV_W) // COL_TILE
    act = lambda off: pl.BlockSpec((None, tm, COL_TILE), lambda bi, i, j: (bi, i, jnp.clip(j - off, 0, 1)))
    return pl.pallas_call(
        _gate_kernel,
        out_shape=jax.ShapeDtypeStruct((b, t, RET_W + ATT_W), BF16),
        grid=(b, t // tm, (RET_W + ATT_W) // COL_TILE),
        in_specs=[pl.BlockSpec((None, tm, d), lambda bi, i, j: (bi, i, 0)),
                  pl.BlockSpec((None, d, COL_TILE),
                               lambda bi, i, j: (layer, 0, jnp.where(j < 2, ret_g0 + j, att_g0 + j - 2))),
                  act(0), act(2),
                  pl.BlockSpec((1, COL_TILE), lambda bi, i, j: (0, jnp.clip(j, 0, 1)))],
        out_specs=pl.BlockSpec((None, tm, COL_TILE), lambda bi, i, j: (bi, i, j)),
        compiler_params=_params("parallel", "parallel", "arbitrary"),
        name="branch_gates",
    )(u, wb, r, a, gn_w)


def _merge_kernel(u_ref, zra_ref, zs_ref, wm0_ref, wm1_ref, wm2_ref, wr_ref, wa_ref, ws_ref, o_ref):
    u = u_ref[...]
    acts = (zra_ref[:, :RET_W], zra_ref[:, RET_W:], zs_ref[...])
    acc = None
    for act, wm_ref, wbr_ref in zip(acts, (wm0_ref, wm1_ref, wm2_ref), (wr_ref, wa_ref, ws_ref)):
        gate = jax.nn.sigmoid(jnp.dot(u, wm_ref[...], preferred_element_type=F32))
        proj = jnp.dot(act, wbr_ref[...], preferred_element_type=F32)
        acc = gate * proj if acc is None else acc + gate * proj
    o_ref[...] = acc.astype(BF16)


def _merge(u, z_ra, z_s, wb, layer, w_ret, w_att, w_s5):
    b, t, d = u.shape
    tm = 528 if t % 528 == 0 else _big_row_tile(t)
    m0 = (wb.shape[2] - N_BRANCH * d) // COL_TILE
    per = d // COL_TILE
    wm = lambda br: pl.BlockSpec((None, d, COL_TILE), lambda bi, i, j: (layer, 0, m0 + br * per + j))
    wbr = lambda k: pl.BlockSpec((k, COL_TILE), lambda bi, i, j: (0, j))
    return pl.pallas_call(
        _merge_kernel,
        out_shape=jax.ShapeDtypeStruct((b, t, d), BF16),
        grid=(b, t // tm, d // COL_TILE),
        in_specs=[pl.BlockSpec((None, tm, d), lambda bi, i, j: (bi, i, 0)),
                  pl.BlockSpec((None, tm, RET_W + ATT_W), lambda bi, i, j: (bi, i, 0)),
                  pl.BlockSpec((None, tm, S5_W), lambda bi, i, j: (bi, i, 0)),
                  wm(0), wm(1), wm(2), wbr(RET_W), wbr(ATT_W), wbr(S5_W)],
        out_specs=pl.BlockSpec((None, tm, COL_TILE), lambda bi, i, j: (bi, i, j)),
        compiler_params=_params("parallel", "parallel", "arbitrary"),
        name="gated_merge",
    )(u, z_ra, z_s, wb, wb, wb, w_ret, w_att, w_s5)


def _out_kernel(*refs, alpha, first_tile, emit_u):
    if emit_u:
        m_ref, w_ref, c_ref, x_ref, mod_ref, lnw_ref, lnb_ref, nmod_ref, o_ref, u_ref, v_ref = refs
    else:
        m_ref, w_ref, c_ref, x_ref, mod_ref, lnw_ref, lnb_ref, o_ref, v_ref = refs
    i = pl.program_id(1)
    d = w_ref.shape[1]

    def run(res_ref):
        m = m_ref[...]
        s1 = jnp.zeros((m.shape[0], 1), F32)
        for c in range(d // COL_TILE):
            cols = slice(c * COL_TILE, (c + 1) * COL_TILE)
            y = jnp.dot(m, w_ref[:, cols], preferred_element_type=F32)
            v = alpha * res_ref[:, cols] + mod_ref[2:3, cols] * y
            v_ref[:, cols] = v
            s1 += jnp.sum(v, axis=-1, keepdims=True)
        mu = s1 * (1.0 / d)
        vc = v_ref[...] - mu
        var = jnp.mean(vc * vc, axis=-1, keepdims=True)
        out = vc * lax.rsqrt(var + LN_EPS) * lnw_ref[...] + lnb_ref[...]
        o_ref[...] = out
        if emit_u:
            u_ref[...] = (out * (1.0 + nmod_ref[1:2, :]) + nmod_ref[0:1, :]).astype(BF16)

    if first_tile == 0:
        @pl.when(i == 0)
        def _():
            run(c_ref)

        @pl.when(i > 0)
        def _():
            run(x_ref)
    else:
        run(x_ref)


def _out_proj(m, w_out, ctx_src, lat_src, modsel, ln_w, ln_b, alpha, skip_ctx, next_modsel=None):
    b, t, d = m.shape
    off = 1 if skip_ctx else 0
    nt = t // ROW_TILE - off
    emit_u = next_modsel is not None
    kern = functools.partial(_out_kernel, alpha=alpha, first_tile=off, emit_u=emit_u)
    vec = lambda: pl.BlockSpec((1, d), lambda bi, i: (0, 0))
    mod = lambda: pl.BlockSpec((None, None, 3, d), lambda bi, i: (bi, jnp.minimum(i + off, 1), 0, 0))
    row_out = lambda: pl.BlockSpec((None, ROW_TILE, d), lambda bi, i: (bi, i, 0))
    x_shape = jax.ShapeDtypeStruct((b, nt * ROW_TILE, d), F32)
    return pl.pallas_call(
        kern,
        out_shape=(x_shape, jax.ShapeDtypeStruct(x_shape.shape, BF16)) if emit_u else x_shape,
        grid=(b, nt),
        in_specs=[pl.BlockSpec((None, ROW_TILE, d), lambda bi, i: (bi, i + off, 0)),
                  pl.BlockSpec((d, d), lambda bi, i: (0, 0)),
                  *_token_specs(ctx_src, lat_src, off),
                  mod(), vec(), vec(), *([mod()] if emit_u else [])],
        out_specs=(row_out(), row_out()) if emit_u else row_out(),
        scratch_shapes=[pltpu.VMEM((ROW_TILE, d), F32)],
        compiler_params=_params("parallel", "arbitrary"),
        name="out_proj_norm",
    )(m, w_out, ctx_src, lat_src, modsel, ln_w.reshape(1, d), ln_b.reshape(1, d),
      *([next_modsel] if emit_u else []))


def _rope_tables(n, n_ctx):
    rows = n // GRID_W
    row = jnp.repeat(jnp.arange(rows, dtype=F32), GRID_W)
    col = jnp.tile(jnp.arange(GRID_W, dtype=F32), rows)

    def table(head_dim):
        per_axis = head_dim // 4
        inv = ROPE_THETA ** (-jnp.arange(per_axis, dtype=F32) / per_axis)
        ang = jnp.concatenate([row[:, None] * inv, col[:, None] * inv], axis=-1)
        cos = jnp.concatenate([jnp.ones((n_ctx, head_dim // 2), F32), jnp.cos(ang)], axis=0)
        sin = jnp.concatenate([jnp.zeros((n_ctx, head_dim // 2), F32), jnp.sin(ang)], axis=0)
        return cos, sin

    cos_r, sin_r = table(RET_DK)
    cos_a, sin_a = table(ATT_HD)
    return (cos_r, sin_r, jnp.concatenate([cos_a, cos_a], -1), jnp.concatenate([-sin_a, sin_a], -1))


def kernel(x, c, ctx, c_ctx, ada_w, ada_b, w_in, ret_log_decay, ret_gn_w, att_q_norm, att_k_norm, s5_a_re, s5_a_im, s5_log_dt, s5_b_re, s5_b_im, s5_c_re, s5_c_im, s5_d, s5_glu_w, s5_glu_b, w_br_ret, w_br_att, w_br_s5, w_out, ln_w, ln_b):
    b, n, d = x.shape
    n_ctx = ctx.shape[1]
    depth = w_in.shape[0]
    assert n_ctx == ROW_TILE and n % ROW_TILE == 0 and n % GRID_W == 0
    assert w_in.shape[2] == 4 * RET_W + 2 * ATT_W + 2 * ATT_KV_W + 2 * S5_W + N_BRANCH * d
    alpha = (2.0 * depth) ** 0.25

    t = n_ctx + n
    ctx_src, lat_src = ctx, x
    cos_r, sin_r, cos_a, sin_a = _rope_tables(n, n_ctx)
    crows = 16
    cvec = jnp.concatenate([c, c_ctx[None, :], jnp.zeros((crows - b - 1, d), F32)], axis=0)
    wb = w_in.astype(BF16)
    s5w = jax.vmap(_s5_weights)(s5_a_re, s5_a_im, s5_log_dt, s5_b_re, s5_b_im, s5_c_re, s5_c_im, s5_d)

    modsels = []
    for l in range(depth):
        mod = _modulation(cvec, ada_w, ada_b, l).reshape(crows, 3, d)
        modsels.append(jnp.stack([jnp.broadcast_to(mod[b][None], (b, 3, d)), mod[:b]], axis=1))

    u = _modulate(ctx_src, lat_src, t, modsels[0])
    for l in range(depth):
        last = l == depth - 1
        modsel = modsels[l]
        mix, s5_stream = _mix_proj(u, wb, l, cos_r, sin_r, cos_a, sin_a,
                                   att_q_norm[l].reshape(1, ATT_HD), att_k_norm[l].reshape(1, ATT_HD))
        r = _retention(mix, ret_log_decay[l], n_ctx)
        a = _attention(mix, n_ctx)
        s_pre = _s5(s5_stream, s5w, l, n_ctx)
        z_s = _s5_glu_gate(s_pre, s5_stream, s5_glu_w[l].astype(BF16), s5_glu_b[l].reshape(1, S5_W))
        z_ra = _branch_gates(u, wb, l, r, a, ret_gn_w[l].reshape(1, RET_W))
        m = _merge(u, z_ra, z_s, wb, l, w_br_ret[l].astype(BF16), w_br_att[l].astype(BF16),
                   w_br_s5[l].astype(BF16))
        res = _out_proj(m, w_out[l].astype(BF16), ctx_src, lat_src, modsel, ln_w[l], ln_b[l], alpha, last,
                        None if last else modsels[l + 1])
        if last:
            return res
        xa, u = res
        ctx_src = lat_src = xa
```

```python
import functools
import math

import jax
import jax.numpy as jnp
from jax import lax
from jax.experimental import pallas as pl
from jax.experimental.pallas import tpu as pltpu

F32 = jnp.float32
BF16 = jnp.bfloat16

GRID_W = 64
RET_HEADS = 4
RET_DK = 256
RET_DV = 256
RET_W = RET_HEADS * RET_DV
ATT_HEADS = 8
ATT_KV_HEADS = 2
ATT_GROUP = ATT_HEADS // ATT_KV_HEADS
ATT_HD = 128
ATT_W = ATT_HEADS * ATT_HD
ATT_KV_W = ATT_KV_HEADS * ATT_HD
ROPE_THETA = 10000.0
S5_GROUP = 16
S5_W = 768
S5_GROUPS = S5_W // S5_GROUP
S5_STATE = 64
N_BRANCH = 3
LN_EPS = 1e-6
RMS_EPS = 1e-6

LANES = 128
VMEM_LIMIT_BYTES = 56 * 1024 * 1024

CHUNK = 128
S5_CHUNK = 16
ROW_TILE = 256
COL_TILE = 512
MIX_W = 4608
ATT_SUB = 2048
LOG2E = math.log2(math.e)


def _params(*sem):
    return pltpu.CompilerParams(dimension_semantics=sem, vmem_limit_bytes=VMEM_LIMIT_BYTES)


def _big_row_tile(t, largest=1056):
    for cand in (2112, 1056, 1024, 768, 640, 512, 256):
        if cand <= largest and t % cand == 0:
            return cand
    raise ValueError(f"unsupported token count {t}")


def _mod_kernel(c_ref, w_ref, b_ref, o_ref):
    c = c_ref[...]
    s = c * jax.nn.sigmoid(c)
    s_hi = s.astype(BF16)
    s_lo = (s - s_hi.astype(F32)).astype(BF16)
    w = w_ref[...]
    w_hi = w.astype(BF16)
    w_lo = (w - w_hi.astype(F32)).astype(BF16)
    acc = jnp.dot(s_hi, w_hi, preferred_element_type=F32)
    acc += jnp.dot(s_lo, w_hi, preferred_element_type=F32)
    acc += jnp.dot(s_hi, w_lo, preferred_element_type=F32)
    o_ref[...] = acc + b_ref[...]


def _modulation(cvec, ada_w, ada_b, layer):
    rows, d = cvec.shape
    n = ada_w.shape[2]
    tn = 768
    return pl.pallas_call(
        _mod_kernel,
        out_shape=jax.ShapeDtypeStruct((rows, n), F32),
        grid=(n // tn,),
        in_specs=[pl.BlockSpec((rows, d), lambda j: (0, 0)),
                  pl.BlockSpec((None, d, tn), lambda j: (layer, 0, j)),
                  pl.BlockSpec((None, 1, tn), lambda j: (layer, 0, j))],
        out_specs=pl.BlockSpec((rows, tn), lambda j: (0, j)),
        compiler_params=_params("arbitrary"),
        name="adaln_modulation",
    )(cvec, ada_w, ada_b.reshape(ada_b.shape[0], 1, n))


def _modulate_kernel(c_ref, x_ref, mod_ref, o_ref):
    i = pl.program_id(1)
    m = mod_ref[...]

    @pl.when(i == 0)
    def _():
        o_ref[...] = (c_ref[...] * (1.0 + m[1:2]) + m[0:1]).astype(BF16)

    @pl.when(i > 0)
    def _():
        o_ref[...] = (x_ref[...] * (1.0 + m[1:2]) + m[0:1]).astype(BF16)


def _token_specs(ctx_src, lat_src, first_tile=0):
    d = ctx_src.shape[2]
    shift = first_tile - (0 if lat_src is ctx_src else 1)
    return (pl.BlockSpec((None, ROW_TILE, d), lambda bi, i: (bi, 0, 0)),
            pl.BlockSpec((None, ROW_TILE, d), lambda bi, i: (bi, jnp.maximum(i + shift, 0), 0)))


def _modulate(ctx_src, lat_src, t, modsel):
    b, _, d = ctx_src.shape
    return pl.pallas_call(
        _modulate_kernel,
        out_shape=jax.ShapeDtypeStruct((b, t, d), BF16),
        grid=(b, t // ROW_TILE),
        in_specs=[*_token_specs(ctx_src, lat_src),
                  pl.BlockSpec((None, None, 3, d), lambda bi, i: (bi, jnp.minimum(i, 1), 0, 0))],
        out_specs=pl.BlockSpec((None, ROW_TILE, d), lambda bi, i: (bi, i, 0)),
        compiler_params=_params("parallel", "arbitrary"),
        name="modulate",
    )(ctx_src, lat_src, modsel)


def _mix_kernel(u_ref, w_ref, cr_ref, sr_ref, ca_ref, sa_ref, qn_ref, kn_ref, o_ref):
    j = pl.program_id(2)
    tm = u_ref.shape[0]
    nsplit = 2 if tm % 32 == 0 else 1
    hr = tm // nsplit

    def halves(epilogue):
        for r in range(nsplit):
            rows = slice(r * hr, (r + 1) * hr)
            epilogue(rows, jnp.dot(u_ref[rows, :], w_ref[...], preferred_element_type=F32))

    def rms_rope(x, rows, w, scale):
        ss = jnp.dot((x * x).astype(BF16), jnp.ones((ATT_HD, ATT_HD), BF16), preferred_element_type=F32)
        y = x * lax.rsqrt(ss * (1.0 / ATT_HD) + RMS_EPS) * w
        y = y * ca_ref[rows, :] + pltpu.roll(y, ATT_HD // 2, 1) * sa_ref[rows, :]
        return (y * scale).astype(BF16)

    @pl.when(j < 4)
    def _():
        scale = jnp.where(j >= 2, RET_DK ** -0.5, 1.0).astype(F32)

        def epilogue(rows, acc):
            cos = cr_ref[rows, :]
            sin = sr_ref[rows, :]
            for h in range(COL_TILE // RET_DK):
                x1 = acc[:, h * RET_DK: h * RET_DK + LANES]
                x2 = acc[:, h * RET_DK + LANES: (h + 1) * RET_DK]
                o_ref[rows, h * RET_DK: h * RET_DK + LANES] = ((x1 * cos - x2 * sin) * scale).astype(BF16)
                o_ref[rows, h * RET_DK + LANES: (h + 1) * RET_DK] = ((x2 * cos + x1 * sin) * scale).astype(BF16)

        halves(epilogue)

    @pl.when((j == 4) | (j == 5))
    def _():
        def epilogue(rows, acc):
            o_ref[rows, :] = acc.astype(BF16)

        halves(epilogue)

    @pl.when((j == 6) | (j == 7))
    def _():
        def epilogue(rows, acc):
            for h in range(COL_TILE // ATT_HD):
                o_ref[rows, h * ATT_HD:(h + 1) * ATT_HD] = rms_rope(
                    acc[:, h * ATT_HD:(h + 1) * ATT_HD], rows, qn_ref[...], ATT_HD ** -0.5 * LOG2E)

        halves(epilogue)

    @pl.when(j == 8)
    def _():
        def epilogue(rows, acc):
            for h in range(ATT_KV_HEADS):
                o_ref[rows, h * ATT_HD:(h + 1) * ATT_HD] = rms_rope(
                    acc[:, h * ATT_HD:(h + 1) * ATT_HD], rows, kn_ref[...], 1.0)
            o_ref[rows, ATT_KV_W:] = acc[:, ATT_KV_W:].astype(BF16)

        halves(epilogue)


def _mix_proj(u, wb, layer, cos_r, sin_r, cos_a, sin_a, qn, kn):
    b, t, d = u.shape
    tm = _big_row_tile(t)
    tab = lambda: pl.BlockSpec((tm, LANES), lambda bi, i, j: (i, 0))
    vec = lambda: pl.BlockSpec((1, LANES), lambda bi, i, j: (0, 0))
    gap = RET_W // COL_TILE
    return pl.pallas_call(
        _mix_kernel,
        out_shape=jax.ShapeDtypeStruct((b, t, MIX_W), BF16),
        grid=(b, t // tm, MIX_W // COL_TILE),
        in_specs=[pl.BlockSpec((None, tm, d), lambda bi, i, j: (bi, i, 0)),
                  pl.BlockSpec((None, d, COL_TILE), lambda bi, i, j: (layer, 0, jnp.where(j >= 6, j + gap, j))),
                  tab(), tab(), tab(), tab(), vec(), vec()],
        out_specs=pl.BlockSpec((None, tm, COL_TILE), lambda bi, i, j: (bi, i, j)),
        compiler_params=_params("parallel", "parallel", "arbitrary"),
        name="mixer_in_proj",
    )(u, wb, cos_r, sin_r, cos_a, sin_a, qn, kn)


def _ret_kernel(ld_ref, q_ref, k_ref, v_ref, o_ref, acc_ref, sf_ref, sb_ref, *, n_ctx_chunks, n_chunks):
    h = pl.program_id(1)
    lgf = ld_ref[0, h]
    lgb = ld_ref[1, h]
    c = CHUNK
    ri = lax.broadcasted_iota(jnp.int32, (c, c), 0).astype(F32)
    ci = lax.broadcasted_iota(jnp.int32, (c, c), 1).astype(F32)
    rel = ri - ci
    mask = jnp.where(rel >= 0, jnp.exp(lgf * jnp.maximum(rel, 0.0)), jnp.exp(lgb * jnp.maximum(-rel, 0.0)))
    row = lax.broadcasted_iota(jnp.int32, (c, RET_DV), 0).astype(F32)
    qdec_f = jnp.exp(lgf * (row + 1.0))
    qdec_b = jnp.exp(lgb * (c - row))
    kdec_f = jnp.exp(lgf * (c - 1.0 - row))
    kdec_b = jnp.exp(lgb * row)
    cdec_f = jnp.exp(jnp.full((1, RET_DV), lgf * c, F32))
    cdec_b = jnp.exp(jnp.full((1, RET_DV), lgb * c, F32))
    contract0 = (((0,), (0,)), ((), ()))
    contract1 = (((1,), (1,)), ((), ()))

    def load(ci_):
        r0 = pl.multiple_of(ci_ * c, c)
        rows = pl.ds(r0, c)
        return rows, q_ref[rows, :], k_ref[rows, :], v_ref[rows, :]

    def bwd(ci_):
        rows, q, k, v = load(ci_)
        s = sb_ref[...]
        acc_ref[rows, :] += jnp.dot(q, s.astype(BF16), preferred_element_type=F32) * qdec_b
        kd = (k.astype(F32) * kdec_b).astype(BF16)
        sb_ref[...] = s * cdec_b + lax.dot_general(kd, v, contract0, preferred_element_type=F32)

    def fwd(ci_):
        rows, q, k, v = load(ci_)
        s = sf_ref[...]
        sc = lax.dot_general(q, k, contract1, preferred_element_type=F32) * mask
        o = jnp.dot(sc.astype(BF16), v, preferred_element_type=F32)
        o += jnp.dot(q, s.astype(BF16), preferred_element_type=F32) * qdec_f
        acc_ref[rows, :] += o
        kd = (k.astype(F32) * kdec_f).astype(BF16)
        sf_ref[...] = s * cdec_f + lax.dot_general(kd, v, contract0, preferred_element_type=F32)

    sf_ref[...] = jnp.zeros_like(sf_ref)
    sb_ref[...] = jnp.zeros_like(sb_ref)
    acc_ref[...] = jnp.zeros_like(acc_ref)

    def ctx_part(t, carry):
        fwd(t)
        bwd(n_ctx_chunks - 1 - t)
        return carry

    def lat_part(t, carry):
        fwd(t)
        bwd(n_chunks - 1 - (t - n_ctx_chunks))
        return carry

    lax.fori_loop(0, n_ctx_chunks, ctx_part, 0, unroll=2)
    lax.fori_loop(n_ctx_chunks, n_chunks, lat_part, 0, unroll=4)

    def norm(t, carry):
        rows = pl.ds(pl.multiple_of(t * c, c), c)
        o = acc_ref[rows, :]
        mu = jnp.mean(o, axis=-1, keepdims=True)
        oc = o - mu
        var = jnp.mean(oc * oc, axis=-1, keepdims=True)
        o_ref[rows, :] = (oc * lax.rsqrt(var + LN_EPS)).astype(BF16)
        return carry

    lax.fori_loop(0, n_chunks, norm, 0, unroll=6)


def _retention(mix, log_decay, n_ctx):
    b, t, _ = mix.shape
    kern = functools.partial(_ret_kernel, n_ctx_chunks=n_ctx // CHUNK, n_chunks=t // CHUNK)
    blk = lambda off: pl.BlockSpec((None, t, RET_DK), lambda bi, h: (bi, 0, off + h))
    return pl.pallas_call(
        kern,
        out_shape=jax.ShapeDtypeStruct((b, t, RET_W), BF16),
        grid=(b, RET_HEADS),
        in_specs=[pl.BlockSpec(memory_space=pltpu.SMEM), blk(0), blk(RET_HEADS), blk(2 * RET_HEADS)],
        out_specs=pl.BlockSpec((None, t, RET_DV), lambda bi, h: (bi, 0, h)),
        scratch_shapes=[pltpu.VMEM((t, RET_DV), F32), pltpu.VMEM((RET_DK, RET_DV), F32),
                        pltpu.VMEM((RET_DK, RET_DV), F32)],
        compiler_params=_params("parallel", "arbitrary"),
        name="retention",
    )(log_decay, mix, mix, mix)


def _att_kernel(q_ref, k_ref, v_ref, o_ref, v1_ref, q4_ref, s_ref, mx_ref, m_ref, acc_ref, *, chunks, n_ctx, t):
    i = pl.program_id(2)
    contract1 = (((1,), (1,)), ((), ()))
    tq = q_ref.shape[0]
    sb = ATT_SUB

    @pl.when(i == 0)
    def _():
        v1_ref[:, :ATT_HD] = v_ref[...]
        v1_ref[:, ATT_HD:] = jnp.ones((t, ATT_HD), BF16)

    for g in range(ATT_GROUP):
        q4_ref[g * tq:(g + 1) * tq, :] = q_ref[:, g * ATT_HD:(g + 1) * ATT_HD]
    neg = jnp.full(mx_ref.shape, -jnp.inf, F32)
    mx_ref[...] = neg
    m_ref[...] = neg
    acc_ref[...] = jnp.zeros(acc_ref.shape, F32)

    def a_step(buf, j, r0, size):
        k = k_ref[pl.ds(r0, size), :]
        s = lax.dot_general(q4_ref[...], k, contract1, preferred_element_type=F32)
        s_ref[buf, j, :, 0:size] = s
        mx = mx_ref[...]
        for tt in range(size // LANES):
            mx = jnp.maximum(mx, s[:, tt * LANES:(tt + 1) * LANES])
        mx_ref[...] = mx

    def settle():
        mx = mx_ref[...]
        mc = jnp.broadcast_to(jnp.max(mx, axis=-1, keepdims=True), mx.shape)
        m_old = m_ref[...]
        m_new = jnp.maximum(m_old, mc)
        alpha = jnp.exp2(m_old - m_new)
        m_ref[...] = m_new
        acc_ref[:, :ATT_HD] = acc_ref[:, :ATT_HD] * alpha
        acc_ref[:, ATT_HD:] = acc_ref[:, ATT_HD:] * alpha
        mx_ref[...] = neg

    def b_step(buf, j, r0, size):
        mb = m_ref[...]
        p = jnp.concatenate([jnp.exp2(s_ref[buf, j, :, tt * LANES:(tt + 1) * LANES] - mb)
                             for tt in range(size // LANES)], axis=1).astype(BF16)
        acc_ref[...] += jnp.dot(p, v1_ref[pl.ds(r0, size), :], preferred_element_type=F32)

    def stage(ca, cb):
        sides = [(a_step, ca), (b_step, cb)]
        sides = [(fn, ch) for fn, ch in sides if ch is not None]
        common = min(ch[2] // sb for _, ch in sides)

        def body(j, carry):
            off = pl.multiple_of(j * sb, sb)
            for fn, (buf, start, _) in sides:
                fn(buf, j, start + off, sb)
            return carry

        if common:
            lax.fori_loop(0, common, body, 0)
        for fn, (buf, start, size) in sides:
            for j in range(common, -(-size // sb)):
                fn(buf, j, start + j * sb, min(sb, size - j * sb))
        if ca is not None:
            settle()

    @pl.when(i == 0)
    def _():
        stage((0, 0, n_ctx), None)
        stage(None, (0, 0, n_ctx))

    @pl.when(i > 0)
    def _():
        bufs = [(c % 2, start, size) for c, (start, size) in enumerate(chunks)]
        for c in range(len(bufs) + 1):
            stage(bufs[c] if c < len(bufs) else None, bufs[c - 1] if c > 0 else None)

    for g in range(ATT_GROUP):
        rows = slice(g * tq, (g + 1) * tq)
        o_ref[:, g * ATT_HD:(g + 1) * ATT_HD] = (acc_ref[rows, :ATT_HD] / acc_ref[rows, ATT_HD:]).astype(BF16)


def _attention(mix, n_ctx):
    b, t, _ = mix.shape
    tq = ROW_TILE
    n_lat = t - n_ctx
    nch = 4 if n_lat % (4 * COL_TILE) == 0 else 1
    chunks = tuple((0, n_ctx + n_lat // nch) if c == 0 else (n_ctx + c * (n_lat // nch), n_lat // nch)
                   for c in range(nch))
    mrows = ATT_GROUP * tq
    q_blk0 = (3 * RET_W) // (ATT_GROUP * ATT_HD)
    k_blk0 = (3 * RET_W + ATT_W) // ATT_HD
    v_blk0 = k_blk0 + ATT_KV_HEADS
    kern = functools.partial(_att_kernel, chunks=chunks, n_ctx=n_ctx, t=t)
    return pl.pallas_call(
        kern,
        out_shape=jax.ShapeDtypeStruct((b, t, ATT_W), BF16),
        grid=(b, ATT_KV_HEADS, t // tq),
        in_specs=[pl.BlockSpec((None, tq, ATT_GROUP * ATT_HD), lambda bi, hk, i: (bi, i, q_blk0 + hk)),
                  pl.BlockSpec((None, t, ATT_HD), lambda bi, hk, i: (bi, 0, k_blk0 + hk)),
                  pl.BlockSpec((None, t, ATT_HD), lambda bi, hk, i: (bi, 0, v_blk0 + hk))],
        out_specs=pl.BlockSpec((None, tq, ATT_GROUP * ATT_HD), lambda bi, hk, i: (bi, i, hk)),
        scratch_shapes=[pltpu.VMEM((t, 2 * ATT_HD), BF16),
                        pltpu.VMEM((mrows, ATT_HD), BF16),
                        pltpu.VMEM((2, -(-chunks[0][1] // ATT_SUB), mrows, ATT_SUB), F32),
                        pltpu.VMEM((mrows, LANES), F32),
                        pltpu.VMEM((mrows, LANES), F32),
                        pltpu.VMEM((mrows, 2 * ATT_HD), F32)],
        compiler_params=_params("parallel", "parallel", "arbitrary"),
        name="gqa_attention",
    )(mix, mix, mix)


def _s5_kernel(u_ref, win_ref, tsum_ref, wout_ref, lam_ref, y_ref, uc_ref, d_ref, xin_ref, yw_ref, *, nc, ncc):
    p2 = 2 * S5_STATE
    sub = 8
    ln = S5_CHUNK
    gpb = LANES // S5_GROUP
    us = [pltpu.bitcast(u_ref[pl.ds(s, nc, stride=ln), :].astype(BF16), jnp.uint32) for s in range(ln)]
    for g in range(gpb):
        uc = pltpu.bitcast(
            jnp.concatenate([us[s][:, g * S5_GROUP:(g + 1) * S5_GROUP] for s in range(ln)], axis=1), BF16)
        uc_ref[g] = uc
        d_ref[g] = jnp.dot(uc, win_ref[g], preferred_element_type=F32)
    rowid = lax.broadcasted_iota(jnp.int32, (sub, p2), 0)

    def tile(g, reverse, rows, c, cs):
        base = 32 if reverse else 0
        col = 2 * p2 if reverse else 0

        def mult(i):
            return tuple(jnp.broadcast_to(lam_ref[g, base + 3 * i + r:base + 3 * i + r + 1, :], (sub, p2))
                         for r in range(3))

        def shifted(v, n):
            if reverse:
                return jnp.where(rowid < sub - n, pltpu.roll(v, sub - n, 0), 0.0)
            return jnp.where(rowid >= n, pltpu.roll(v, n, 0), 0.0)

        x = d_ref[g, rows, col:col + p2]
        xs = d_ref[g, rows, col + p2:col + 2 * p2]
        for i, n in enumerate((1, 2, 4)):
            a, b, bs = mult(i)
            xr, xsr = shifted(x, n), shifted(xs, n)
            x, xs = x + a * xr + b * xsr, xs + a * xsr + bs * xr
        pa = lam_ref[g, base + 16:base + 24, :]
        pb = lam_ref[g, base + 24:base + 32, :]
        xin_ref[g, rows, col // 2:col // 2 + p2] = shifted(x, 1) + pa * c + pb * cs
        edge = 0 if reverse else sub - 1
        last = jnp.broadcast_to(x[edge:edge + 1, :], (sub, p2))
        lasts = jnp.broadcast_to(xs[edge:edge + 1, :], (sub, p2))
        a, b, bs = mult(3)
        return a * c + b * cs + last, a * cs + bs * c + lasts

    nt = nc // sub
    nct = ncc // sub

    def step(mf, mb, carry):
        rf = pl.ds(pl.multiple_of(mf * sub, sub), sub)
        rb = pl.ds(pl.multiple_of(mb * sub, sub), sub)
        return tuple(tile(g, False, rf, carry[g][0], carry[g][1]) + tile(g, True, rb, carry[g][2], carry[g][3])
                     for g in range(gpb))

    zero = jnp.zeros((sub, p2), F32)
    init = tuple((zero, zero, zero, zero) for _ in range(gpb))
    carry = lax.fori_loop(0, nct, lambda t, cr: step(t, nct - 1 - t, cr), init)
    lax.fori_loop(nct, nt, lambda t, cr: step(t, nt - 1 - (t - nct), cr), carry)

    for g in range(gpb):
        y = jnp.dot(uc_ref[g], tsum_ref[g], preferred_element_type=F32)
        y += jnp.dot(xin_ref[g].astype(BF16), wout_ref[g], preferred_element_type=F32)
        yw_ref[g] = pltpu.bitcast(y.astype(BF16), jnp.uint32)
    for s in range(ln):
        piece = jnp.concatenate([yw_ref[g, :, s * S5_GROUP:(s + 1) * S5_GROUP] for g in range(gpb)], axis=1)
        y_ref[pl.ds(s, nc, stride=ln), :] = pltpu.bitcast(piece, BF16).astype(F32)


def _s5_weights(a_re, a_im, log_dt, b_re, b_im, c_re, c_im, d_skip):
    hi = lax.Precision.HIGHEST
    ln = S5_CHUNK
    dt = jnp.exp(log_dt)[..., None]
    mag = jnp.exp(a_re * dt)
    abr, abi = mag * jnp.cos(a_im * dt), mag * jnp.sin(a_im * dt)
    den = a_re * a_re + a_im * a_im
    fr = ((abr - 1.0) * a_re + abi * a_im) / den
    fi = (abi * a_re - (abr - 1.0) * a_im) / den
    bbr = fr[..., None] * b_re - fi[..., None] * b_im
    bbi = fr[..., None] * b_im + fi[..., None] * b_re
    n = jnp.arange(ln + 1, dtype=F32)[:, None, None, None]
    pmag = jnp.exp(a_re * dt * n)
    pr, pi = pmag * jnp.cos(a_im * dt * n), pmag * jnp.sin(a_im * dt * n)
    cr = c_re[None] * pr[:, :, :, None, :] - c_im[None] * pi[:, :, :, None, :]
    ci = c_re[None] * pi[:, :, :, None, :] + c_im[None] * pr[:, :, :, None, :]
    g_ = a_re.shape[1]
    lhs = jnp.concatenate([cr[:ln], -ci[:ln]], axis=-1).transpose(1, 2, 0, 3, 4)
    lhs = lhs.reshape(2, g_, ln * S5_GROUP, 2 * S5_STATE)
    rhs = jnp.concatenate([bbr, bbi], axis=2)
    m = jnp.einsum('dgmk,dgke->dgme', lhs, rhs, precision=hi)
    m = m.reshape(2, g_, ln, S5_GROUP, S5_GROUP).transpose(2, 0, 1, 3, 4)
    lag = jnp.arange(ln)[None, :] - jnp.arange(ln)[:, None]
    n_idx = jnp.arange(ln)[:, None, None]
    oh_f = (lag[None] == n_idx).astype(F32)
    oh_b = (-lag[None] == n_idx).astype(F32)
    tsum = (jnp.einsum('nsi,ngce->gseic', oh_f, m[:, 0], precision=hi)
            + jnp.einsum('nsi,ngce->gseic', oh_b, m[:, 1], precision=hi))
    g = a_re.shape[1]
    eye_s = jnp.eye(ln, dtype=F32)[None, :, None, :, None]
    eye_c = jnp.eye(S5_GROUP, dtype=F32)[None, None, :, None, :]
    tsum = tsum + eye_s * eye_c * d_skip.reshape(g, 1, S5_GROUP, 1, 1)
    tsum = tsum.reshape(g, ln * S5_GROUP, ln * S5_GROUP)

    def to_g(p):
        return p.transpose(2, 0, 1, 3)[:, :, :, None, :]

    pw_r = to_g(jnp.stack([pr[:ln, 0][::-1], pr[:ln, 1]]))
    pw_i = to_g(jnp.stack([pi[:ln, 0][::-1], pi[:ln, 1]]))
    bt_r = bbr.transpose(1, 0, 3, 2)[:, :, None]
    bt_i = bbi.transpose(1, 0, 3, 2)[:, :, None]
    w_re = pw_r * bt_r - pw_i * bt_i
    w_im = pw_r * bt_i + pw_i * bt_r
    win = jnp.stack([w_re, w_im, w_im, w_re], axis=-2)
    win = win.transpose(0, 2, 3, 1, 4, 5).reshape(g, ln * S5_GROUP, 8 * S5_STATE)

    cw_r = jnp.stack([cr[1:ln + 1, 0], cr[1:ln + 1, 1][::-1]])
    cw_i = jnp.stack([ci[1:ln + 1, 0], ci[1:ln + 1, 1][::-1]])
    wout = jnp.stack([cw_r, -cw_i], axis=1)
    wout = wout.transpose(3, 0, 1, 5, 2, 4).reshape(g, 4 * S5_STATE, ln * S5_GROUP)

    def cpow(k):
        e = (jnp.asarray(k, F32) * ln)[..., None, None, None]
        mg = jnp.exp(a_re * dt * e)
        return mg * jnp.cos(a_im * dt * e), mg * jnp.sin(a_im * dt * e)

    def dup(a, b):
        return jnp.concatenate([a, b], axis=-1)

    mr, mi = cpow(jnp.array([1, 2, 4, 8]))
    mult = jnp.stack([dup(mr, mr), dup(-mi, mi), dup(mi, -mi)], axis=1).reshape(12, 2, g, 2 * S5_STATE)
    qr, qi = cpow(jnp.arange(8))
    qr = jnp.stack([qr[:, 0], qr[::-1, 1]], axis=1)
    qi = jnp.stack([qi[:, 0], qi[::-1, 1]], axis=1)
    lam = jnp.concatenate([mult, jnp.zeros((4, 2, g, 2 * S5_STATE), F32), dup(qr, qr), dup(-qi, qi)], axis=0)
    lam = lam.transpose(2, 1, 0, 3).reshape(g, 64, 2 * S5_STATE)
    return win.astype(BF16), tsum.astype(BF16), wout.astype(BF16), lam


def _s5_in_kernel(u_ref, w_ref, o_ref):
    o_ref[...] = jnp.dot(u_ref[...], w_ref[...], preferred_element_type=F32)


def _s5_in_proj(u, w_s5):
    b, t, d = u.shape
    tm = _big_row_tile(t)
    return pl.pallas_call(
        _s5_in_kernel,
        out_shape=jax.ShapeDtypeStruct((b, t, S5_W), F32),
        grid=(b, t // tm),
        in_specs=[pl.BlockSpec((None, tm, d), lambda bi, i: (bi, i, 0)),
                  pl.BlockSpec((d, S5_W), lambda bi, i: (0, 0))],
        out_specs=pl.BlockSpec((None, tm, S5_W), lambda bi, i: (bi, i, 0)),
        compiler_params=_params("parallel", "arbitrary"),
        name="s5_in_proj",
    )(u, w_s5)


def _s5(u5, s5w, layer, n_ctx):
    b, t, _ = u5.shape
    win, tsum, wout, lam = s5w
    gpb = LANES // S5_GROUP
    nc = t // S5_CHUNK
    kw = S5_CHUNK * S5_GROUP
    assert (n_ctx // S5_CHUNK) % 8 == 0 and nc % 8 == 0
    kern = functools.partial(_s5_kernel, nc=nc, ncc=n_ctx // S5_CHUNK)
    wspec = lambda r, c: pl.BlockSpec((None, gpb, r, c), lambda bi, gb: (layer, gb, 0, 0))
    return pl.pallas_call(
        kern,
        out_shape=jax.ShapeDtypeStruct((b, t, S5_W), F32),
        grid=(b, S5_W // LANES),
        in_specs=[pl.BlockSpec((None, t, LANES), lambda bi, gb: (bi, 0, gb)),
                  wspec(kw, 8 * S5_STATE), wspec(kw, kw), wspec(4 * S5_STATE, kw), wspec(64, 2 * S5_STATE)],
        out_specs=pl.BlockSpec((None, t, LANES), lambda bi, gb: (bi, 0, gb)),
        scratch_shapes=[pltpu.VMEM((gpb, nc, kw), BF16),
                        pltpu.VMEM((gpb, nc, 8 * S5_STATE), F32),
                        pltpu.VMEM((gpb, nc, 4 * S5_STATE), F32),
                        pltpu.VMEM((gpb, nc // 2, kw), jnp.uint32)],
        compiler_params=_params("parallel", "parallel"),
        name="s5_scan",
    )(u5, win, tsum, wout, lam)


def _glu_kernel(s_ref, u_ref, wg_ref, w_ref, b_ref, o_ref):
    s = jax.nn.gelu(s_ref[...])
    z = jnp.dot(s.astype(BF16), w_ref[...], preferred_element_type=F32) + b_ref[...]
    g = jnp.dot(u_ref[...], wg_ref[...], preferred_element_type=F32)
    o_ref[...] = (s * jax.nn.sigmoid(z) * (g * jax.nn.sigmoid(g))).astype(BF16)


def _s5_glu_gate(s_pre, u, w_s5, glu_w, glu_b):
    b, t, d = u.shape
    tm = _big_row_tile(t)
    return pl.pallas_call(
        _glu_kernel,
        out_shape=jax.ShapeDtypeStruct((b, t, S5_W), BF16),
        grid=(b, t // tm),
        in_specs=[pl.BlockSpec((None, tm, S5_W), lambda bi, i: (bi, i, 0)),
                  pl.BlockSpec((None, tm, d), lambda bi, i: (bi, i, 0)),
                  pl.BlockSpec((d, S5_W), lambda bi, i: (0, 1)),
                  pl.BlockSpec((S5_W, S5_W), lambda bi, i: (0, 0)),
                  pl.BlockSpec((1, S5_W), lambda bi, i: (0, 0))],
        out_specs=pl.BlockSpec((None, tm, S5_W), lambda bi, i: (bi, i, 0)),
        compiler_params=_params("parallel", "arbitrary"),
        name="s5_glu_gate",
    )(s_pre, u, w_s5, glu_w, glu_b)


def _gate_kernel(u_ref, w_ref, r_ref, a_ref, gn_ref, o_ref):
    j = pl.program_id(2)
    tm = u_ref.shape[0]
    nsplit = 2 if tm % 32 == 0 else 1
    hr = tm // nsplit

    def halves(act_ref, scale):
        for r in range(nsplit):
            rows = slice(r * hr, (r + 1) * hr)
            g = jnp.dot(u_ref[rows, :], w_ref[...], preferred_element_type=F32)
            act = act_ref[rows, :].astype(F32)
            if scale is not None:
                act = act * scale
            o_ref[rows, :] = (act * (g * jax.nn.sigmoid(g))).astype(BF16)

    @pl.when(j < 2)
    def _():
        halves(r_ref, gn_ref[...])

    @pl.when(j >= 2)
    def _():
        halves(a_ref, None)


def _branch_gates(u, wb, layer, r, a, gn_w):
    b, t, d = u.shape
    tm = _big_row_tile(t, 2112)
    ret_g0 = 3 * RET_W // COL_TILE
    att_g0 = (4 * RET_W + ATT_W + 2 * ATT_KV_W) // COL_TILE
    act = lambda off: pl.BlockSpec((None, tm, COL_TILE), lambda bi, i, j: (bi, i, jnp.clip(j - off, 0, 1)))
    return pl.pallas_call(
        _gate_kernel,
        out_shape=jax.ShapeDtypeStruct((b, t, RET_W + ATT_W), BF16),
        grid=(b, t // tm, (RET_W + ATT_W) // COL_TILE),
        in_specs=[pl.BlockSpec((None, tm, d), lambda bi, i, j: (bi, i, 0)),
                  pl.BlockSpec((None, d, COL_TILE),
                               lambda bi, i, j: (layer, 0, jnp.where(j < 2, ret_g0 + j, att_g0 + j - 2))),
                  act(0), act(2),
                  pl.BlockSpec((1, COL_TILE), lambda bi, i, j: (0, jnp.clip(j, 0, 1)))],
        out_specs=pl.BlockSpec((None, tm, COL_TILE), lambda bi, i, j: (bi, i, j)),
        compiler_params=_params("parallel", "parallel", "arbitrary"),
        name="branch_gates",
    )(u, wb, r, a, gn_w)


def _merge_kernel(u_ref, zra_ref, zs_ref, wm0_ref, wm1_ref, wm2_ref, wr_ref, wa_ref, ws_ref, o_ref):
    u = u_ref[...]
    acts = (zra_ref[:, :RET_W], zra_ref[:, RET_W:], zs_ref[...])
    acc = None
    for act, wm_ref, wbr_ref in zip(acts, (wm0_ref, wm1_ref, wm2_ref), (wr_ref, wa_ref, ws_ref)):
        gate = jax.nn.sigmoid(jnp.dot(u, wm_ref[...], preferred_element_type=F32))
        proj = jnp.dot(act, wbr_ref[...], preferred_element_type=F32)
        acc = gate * proj if acc is None else acc + gate * proj
    o_ref[...] = acc.astype(BF16)


def _merge(u, z_ra, z_s, wb, layer, w_ret, w_att, w_s5):
    b, t, d = u.shape
    tm = 528 if t % 528 == 0 else _big_row_tile(t)
    m0 = (wb.shape[2] - N_BRANCH * d) // COL_TILE
    per = d // COL_TILE
    wm = lambda br: pl.BlockSpec((None, d, COL_TILE), lambda bi, i, j: (layer, 0, m0 + br * per + j))
    wbr = lambda k: pl.BlockSpec((k, COL_TILE), lambda bi, i, j: (0, j))
    return pl.pallas_call(
        _merge_kernel,
        out_shape=jax.ShapeDtypeStruct((b, t, d), BF16),
        grid=(b, t // tm, d // COL_TILE),
        in_specs=[pl.BlockSpec((None, tm, d), lambda bi, i, j: (bi, i, 0)),
                  pl.BlockSpec((None, tm, RET_W + ATT_W), lambda bi, i, j: (bi, i, 0)),
                  pl.BlockSpec((None, tm, S5_W), lambda bi, i, j: (bi, i, 0)),
                  wm(0), wm(1), wm(2), wbr(RET_W), wbr(ATT_W), wbr(S5_W)],
        out_specs=pl.BlockSpec((None, tm, COL_TILE), lambda bi, i, j: (bi, i, j)),
        compiler_params=_params("parallel", "parallel", "arbitrary"),
        name="gated_merge",
    )(u, z_ra, z_s, wb, wb, wb, w_ret, w_att, w_s5)


def _out_kernel(*refs, alpha, first_tile, emit_u):
    if emit_u:
        m_ref, w_ref, c_ref, x_ref, mod_ref, lnw_ref, lnb_ref, nmod_ref, o_ref, u_ref, v_ref = refs
    else:
        m_ref, w_ref, c_ref, x_ref, mod_ref, lnw_ref, lnb_ref, o_ref, v_ref = refs
    i = pl.program_id(1)
    d = w_ref.shape[1]

    def run(res_ref):
        m = m_ref[...]
        s1 = jnp.zeros((m.shape[0], 1), F32)
        for c in range(d // COL_TILE):
            cols = slice(c * COL_TILE, (c + 1) * COL_TILE)
            y = jnp.dot(m, w_ref[:, cols], preferred_element_type=F32)
            v = alpha * res_ref[:, cols] + mod_ref[2:3, cols] * y
            v_ref[:, cols] = v
            s1 += jnp.sum(v, axis=-1, keepdims=True)
        mu = s1 * (1.0 / d)
        vc = v_ref[...] - mu
        var = jnp.mean(vc * vc, axis=-1, keepdims=True)
        out = vc * lax.rsqrt(var + LN_EPS) * lnw_ref[...] + lnb_ref[...]
        o_ref[...] = out
        if emit_u:
            u_ref[...] = (out * (1.0 + nmod_ref[1:2, :]) + nmod_ref[0:1, :]).astype(BF16)

    if first_tile == 0:
        @pl.when(i == 0)
        def _():
            run(c_ref)

        @pl.when(i > 0)
        def _():
            run(x_ref)
    else:
        run(x_ref)


def _out_proj(m, w_out, ctx_src, lat_src, modsel, ln_w, ln_b, alpha, skip_ctx, next_modsel=None):
    b, t, d = m.shape
    off = 1 if skip_ctx else 0
    nt = t // ROW_TILE - off
    emit_u = next_modsel is not None
    kern = functools.partial(_out_kernel, alpha=alpha, first_tile=off, emit_u=emit_u)
    vec = lambda: pl.BlockSpec((1, d), lambda bi, i: (0, 0))
    mod = lambda: pl.BlockSpec((None, None, 3, d), lambda bi, i: (bi, jnp.minimum(i + off, 1), 0, 0))
    row_out = lambda: pl.BlockSpec((None, ROW_TILE, d), lambda bi, i: (bi, i, 0))
    x_shape = jax.ShapeDtypeStruct((b, nt * ROW_TILE, d), F32)
    return pl.pallas_call(
        kern,
        out_shape=(x_shape, jax.ShapeDtypeStruct(x_shape.shape, BF16)) if emit_u else x_shape,
        grid=(b, nt),
        in_specs=[pl.BlockSpec((None, ROW_TILE, d), lambda bi, i: (bi, i + off, 0)),
                  pl.BlockSpec((d, d), lambda bi, i: (0, 0)),
                  *_token_specs(ctx_src, lat_src, off),
                  mod(), vec(), vec(), *([mod()] if emit_u else [])],
        out_specs=(row_out(), row_out()) if emit_u else row_out(),
        scratch_shapes=[pltpu.VMEM((ROW_TILE, d), F32)],
        compiler_params=_params("parallel", "arbitrary"),
        name="out_proj_norm",
    )(m, w_out, ctx_src, lat_src, modsel, ln_w.reshape(1, d), ln_b.reshape(1, d),
      *([next_modsel] if emit_u else []))


def _rope_tables(n, n_ctx):
    rows = n // GRID_W
    row = jnp.repeat(jnp.arange(rows, dtype=F32), GRID_W)
    col = jnp.tile(jnp.arange(GRID_W, dtype=F32), rows)

    def table(head_dim):
        per_axis = head_dim // 4
        inv = ROPE_THETA ** (-jnp.arange(per_axis, dtype=F32) / per_axis)
        ang = jnp.concatenate([row[:, None] * inv, col[:, None] * inv], axis=-1)
        cos = jnp.concatenate([jnp.ones((n_ctx, head_dim // 2), F32), jnp.cos(ang)], axis=0)
        sin = jnp.concatenate([jnp.zeros((n_ctx, head_dim // 2), F32), jnp.sin(ang)], axis=0)
        return cos, sin

    cos_r, sin_r = table(RET_DK)
    cos_a, sin_a = table(ATT_HD)
    return (cos_r, sin_r, jnp.concatenate([cos_a, cos_a], -1), jnp.concatenate([-sin_a, sin_a], -1))


def kernel(x, c, ctx, c_ctx, ada_w, ada_b, w_in, ret_log_decay, ret_gn_w, att_q_norm, att_k_norm, s5_a_re, s5_a_im, s5_log_dt, s5_b_re, s5_b_im, s5_c_re, s5_c_im, s5_d, s5_glu_w, s5_glu_b, w_br_ret, w_br_att, w_br_s5, w_out, ln_w, ln_b):
    b, n, d = x.shape
    n_ctx = ctx.shape[1]
    depth = w_in.shape[0]
    assert n_ctx == ROW_TILE and n % ROW_TILE == 0 and n % GRID_W == 0
    assert w_in.shape[2] == 4 * RET_W + 2 * ATT_W + 2 * ATT_KV_W + 2 * S5_W + N_BRANCH * d
    alpha = (2.0 * depth) ** 0.25
    s5_col0 = 4 * RET_W + 2 * ATT_W + 2 * ATT_KV_W

    t = n_ctx + n
    ctx_src, lat_src = ctx, x
    cos_r, sin_r, cos_a, sin_a = _rope_tables(n, n_ctx)
    crows = 16
    cvec = jnp.concatenate([c, c_ctx[None, :], jnp.zeros((crows - b - 1, d), F32)], axis=0)
    wb = w_in.astype(BF16)
    s5w = jax.vmap(_s5_weights)(s5_a_re, s5_a_im, s5_log_dt, s5_b_re, s5_b_im, s5_c_re, s5_c_im, s5_d)

    modsels = []
    for l in range(depth):
        mod = _modulation(cvec, ada_w, ada_b, l).reshape(crows, 3, d)
        modsels.append(jnp.stack([jnp.broadcast_to(mod[b][None], (b, 3, d)), mod[:b]], axis=1))

    u = _modulate(ctx_src, lat_src, t, modsels[0])
    for l in range(depth):
        last = l == depth - 1
        modsel = modsels[l]
        w_s5 = wb[l, :, s5_col0:s5_col0 + 2 * S5_W]
        mix = _mix_proj(u, wb, l, cos_r, sin_r, cos_a, sin_a,
                        att_q_norm[l].reshape(1, ATT_HD), att_k_norm[l].reshape(1, ATT_HD))
        r = _retention(mix, ret_log_decay[l], n_ctx)
        a = _attention(mix, n_ctx)
        s_pre = _s5(_s5_in_proj(u, w_s5), s5w, l, n_ctx)
        z_s = _s5_glu_gate(s_pre, u, w_s5, s5_glu_w[l].astype(BF16), s5_glu_b[l].reshape(1, S5_W))
        z_ra = _branch_gates(u, wb, l, r, a, ret_gn_w[l].reshape(1, RET_W))
        m = _merge(u, z_ra, z_s, wb, l, w_br_ret[l].astype(BF16), w_br_att[l].astype(BF16),
                   w_br_s5[l].astype(BF16))
        res = _out_proj(m, w_out[l].astype(BF16), ctx_src, lat_src, modsel, ln_w[l], ln_b[l], alpha, last,
                        None if last else modsels[l + 1])
        if last:
            return res
        xa, u = res
        ctx_src = lat_src = xa
```

```python
import functools
import math

import jax
import jax.numpy as jnp
from jax import lax
from jax.experimental import pallas as pl
from jax.experimental.pallas import tpu as pltpu

F32 = jnp.float32
BF16 = jnp.bfloat16

GRID_W = 64
RET_HEADS = 4
RET_DK = 256
RET_DV = 256
RET_W = RET_HEADS * RET_DV
ATT_HEADS = 8
ATT_KV_HEADS = 2
ATT_GROUP = ATT_HEADS // ATT_KV_HEADS
ATT_HD = 128
ATT_W = ATT_HEADS * ATT_HD
ATT_KV_W = ATT_KV_HEADS * ATT_HD
ROPE_THETA = 10000.0
S5_GROUP = 16
S5_W = 768
S5_GROUPS = S5_W // S5_GROUP
S5_STATE = 64
N_BRANCH = 3
LN_EPS = 1e-6
RMS_EPS = 1e-6

LANES = 128
VMEM_LIMIT_BYTES = 56 * 1024 * 1024

CHUNK = 128
S5_CHUNK = 16
ROW_TILE = 256
COL_TILE = 512
MIX_W = 4608
ATT_SUB = 2048
LOG2E = math.log2(math.e)


def _params(*sem):
    return pltpu.CompilerParams(dimension_semantics=sem, vmem_limit_bytes=VMEM_LIMIT_BYTES)


def _big_row_tile(t, largest=1056):
    for cand in (2112, 1056, 1024, 768, 640, 512, 256):
        if cand <= largest and t % cand == 0:
            return cand
    raise ValueError(f"unsupported token count {t}")


def _mod_kernel(c_ref, w_ref, b_ref, o_ref):
    c = c_ref[...]
    s = c * jax.nn.sigmoid(c)
    s_hi = s.astype(BF16)
    s_lo = (s - s_hi.astype(F32)).astype(BF16)
    w = w_ref[...]
    w_hi = w.astype(BF16)
    w_lo = (w - w_hi.astype(F32)).astype(BF16)
    acc = jnp.dot(s_hi, w_hi, preferred_element_type=F32)
    acc += jnp.dot(s_lo, w_hi, preferred_element_type=F32)
    acc += jnp.dot(s_hi, w_lo, preferred_element_type=F32)
    o_ref[...] = acc + b_ref[...]


def _modulation(cvec, ada_w, ada_b, layer):
    rows, d = cvec.shape
    n = ada_w.shape[2]
    tn = 768
    return pl.pallas_call(
        _mod_kernel,
        out_shape=jax.ShapeDtypeStruct((rows, n), F32),
        grid=(n // tn,),
        in_specs=[pl.BlockSpec((rows, d), lambda j: (0, 0)),
                  pl.BlockSpec((None, d, tn), lambda j: (layer, 0, j)),
                  pl.BlockSpec((None, 1, tn), lambda j: (layer, 0, j))],
        out_specs=pl.BlockSpec((rows, tn), lambda j: (0, j)),
        compiler_params=_params("arbitrary"),
        name="adaln_modulation",
    )(cvec, ada_w, ada_b.reshape(ada_b.shape[0], 1, n))


def _modulate_kernel(c_ref, x_ref, mod_ref, o_ref):
    i = pl.program_id(1)
    m = mod_ref[...]

    @pl.when(i == 0)
    def _():
        o_ref[...] = (c_ref[...] * (1.0 + m[1:2]) + m[0:1]).astype(BF16)

    @pl.when(i > 0)
    def _():
        o_ref[...] = (x_ref[...] * (1.0 + m[1:2]) + m[0:1]).astype(BF16)


def _token_specs(ctx_src, lat_src, first_tile=0):
    d = ctx_src.shape[2]
    shift = first_tile - (0 if lat_src is ctx_src else 1)
    return (pl.BlockSpec((None, ROW_TILE, d), lambda bi, i: (bi, 0, 0)),
            pl.BlockSpec((None, ROW_TILE, d), lambda bi, i: (bi, jnp.maximum(i + shift, 0), 0)))


def _modulate(ctx_src, lat_src, t, modsel):
    b, _, d = ctx_src.shape
    return pl.pallas_call(
        _modulate_kernel,
        out_shape=jax.ShapeDtypeStruct((b, t, d), BF16),
        grid=(b, t // ROW_TILE),
        in_specs=[*_token_specs(ctx_src, lat_src),
                  pl.BlockSpec((None, None, 3, d), lambda bi, i: (bi, jnp.minimum(i, 1), 0, 0))],
        out_specs=pl.BlockSpec((None, ROW_TILE, d), lambda bi, i: (bi, i, 0)),
        compiler_params=_params("parallel", "arbitrary"),
        name="modulate",
    )(ctx_src, lat_src, modsel)


def _mix_kernel(u_ref, w_ref, cr_ref, sr_ref, ca_ref, sa_ref, qn_ref, kn_ref, o_ref):
    j = pl.program_id(2)
    tm = u_ref.shape[0]
    nsplit = 2 if tm % 32 == 0 else 1
    hr = tm // nsplit

    def halves(epilogue):
        for r in range(nsplit):
            rows = slice(r * hr, (r + 1) * hr)
            epilogue(rows, jnp.dot(u_ref[rows, :], w_ref[...], preferred_element_type=F32))

    def rms_rope(x, rows, w, scale):
        ss = jnp.dot((x * x).astype(BF16), jnp.ones((ATT_HD, ATT_HD), BF16), preferred_element_type=F32)
        y = x * lax.rsqrt(ss * (1.0 / ATT_HD) + RMS_EPS) * w
        y = y * ca_ref[rows, :] + pltpu.roll(y, ATT_HD // 2, 1) * sa_ref[rows, :]
        return (y * scale).astype(BF16)

    @pl.when(j < 4)
    def _():
        scale = jnp.where(j >= 2, RET_DK ** -0.5, 1.0).astype(F32)

        def epilogue(rows, acc):
            cos = cr_ref[rows, :]
            sin = sr_ref[rows, :]
            for h in range(COL_TILE // RET_DK):
                x1 = acc[:, h * RET_DK: h * RET_DK + LANES]
                x2 = acc[:, h * RET_DK + LANES: (h + 1) * RET_DK]
                o_ref[rows, h * RET_DK: h * RET_DK + LANES] = ((x1 * cos - x2 * sin) * scale).astype(BF16)
                o_ref[rows, h * RET_DK + LANES: (h + 1) * RET_DK] = ((x2 * cos + x1 * sin) * scale).astype(BF16)

        halves(epilogue)

    @pl.when((j == 4) | (j == 5))
    def _():
        def epilogue(rows, acc):
            o_ref[rows, :] = acc.astype(BF16)

        halves(epilogue)

    @pl.when((j == 6) | (j == 7))
    def _():
        def epilogue(rows, acc):
            for h in range(COL_TILE // ATT_HD):
                o_ref[rows, h * ATT_HD:(h + 1) * ATT_HD] = rms_rope(
                    acc[:, h * ATT_HD:(h + 1) * ATT_HD], rows, qn_ref[...], ATT_HD ** -0.5 * LOG2E)

        halves(epilogue)

    @pl.when(j == 8)
    def _():
        def epilogue(rows, acc):
            for h in range(ATT_KV_HEADS):
                o_ref[rows, h * ATT_HD:(h + 1) * ATT_HD] = rms_rope(
                    acc[:, h * ATT_HD:(h + 1) * ATT_HD], rows, kn_ref[...], 1.0)
            o_ref[rows, ATT_KV_W:] = acc[:, ATT_KV_W:].astype(BF16)

        halves(epilogue)


def _mix_proj(u, wb, layer, cos_r, sin_r, cos_a, sin_a, qn, kn):
    b, t, d = u.shape
    tm = _big_row_tile(t)
    tab = lambda: pl.BlockSpec((tm, LANES), lambda bi, i, j: (i, 0))
    vec = lambda: pl.BlockSpec((1, LANES), lambda bi, i, j: (0, 0))
    gap = RET_W // COL_TILE
    return pl.pallas_call(
        _mix_kernel,
        out_shape=jax.ShapeDtypeStruct((b, t, MIX_W), BF16),
        grid=(b, t // tm, MIX_W // COL_TILE),
        in_specs=[pl.BlockSpec((None, tm, d), lambda bi, i, j: (bi, i, 0)),
                  pl.BlockSpec((None, d, COL_TILE), lambda bi, i, j: (layer, 0, jnp.where(j >= 6, j + gap, j))),
                  tab(), tab(), tab(), tab(), vec(), vec()],
        out_specs=pl.BlockSpec((None, tm, COL_TILE), lambda bi, i, j: (bi, i, j)),
        compiler_params=_params("parallel", "parallel", "arbitrary"),
        name="mixer_in_proj",
    )(u, wb, cos_r, sin_r, cos_a, sin_a, qn, kn)


def _ret_kernel(ld_ref, q_ref, k_ref, v_ref, o_ref, acc_ref, sf_ref, sb_ref, *, n_ctx_chunks, n_chunks):
    h = pl.program_id(1)
    lgf = ld_ref[0, h]
    lgb = ld_ref[1, h]
    c = CHUNK
    ri = lax.broadcasted_iota(jnp.int32, (c, c), 0).astype(F32)
    ci = lax.broadcasted_iota(jnp.int32, (c, c), 1).astype(F32)
    rel = ri - ci
    mask = jnp.where(rel >= 0, jnp.exp(lgf * jnp.maximum(rel, 0.0)), jnp.exp(lgb * jnp.maximum(-rel, 0.0)))
    row = lax.broadcasted_iota(jnp.int32, (c, RET_DV), 0).astype(F32)
    qdec_f = jnp.exp(lgf * (row + 1.0))
    qdec_b = jnp.exp(lgb * (c - row))
    kdec_f = jnp.exp(lgf * (c - 1.0 - row))
    kdec_b = jnp.exp(lgb * row)
    cdec_f = jnp.exp(jnp.full((1, RET_DV), lgf * c, F32))
    cdec_b = jnp.exp(jnp.full((1, RET_DV), lgb * c, F32))
    contract0 = (((0,), (0,)), ((), ()))
    contract1 = (((1,), (1,)), ((), ()))

    def load(ci_):
        r0 = pl.multiple_of(ci_ * c, c)
        rows = pl.ds(r0, c)
        return rows, q_ref[rows, :], k_ref[rows, :], v_ref[rows, :]

    def bwd(ci_):
        rows, q, k, v = load(ci_)
        s = sb_ref[...]
        acc_ref[rows, :] += jnp.dot(q, s.astype(BF16), preferred_element_type=F32) * qdec_b
        kd = (k.astype(F32) * kdec_b).astype(BF16)
        sb_ref[...] = s * cdec_b + lax.dot_general(kd, v, contract0, preferred_element_type=F32)

    def fwd(ci_):
        rows, q, k, v = load(ci_)
        s = sf_ref[...]
        sc = lax.dot_general(q, k, contract1, preferred_element_type=F32) * mask
        o = jnp.dot(sc.astype(BF16), v, preferred_element_type=F32)
        o += jnp.dot(q, s.astype(BF16), preferred_element_type=F32) * qdec_f
        acc_ref[rows, :] += o
        kd = (k.astype(F32) * kdec_f).astype(BF16)
        sf_ref[...] = s * cdec_f + lax.dot_general(kd, v, contract0, preferred_element_type=F32)

    sf_ref[...] = jnp.zeros_like(sf_ref)
    sb_ref[...] = jnp.zeros_like(sb_ref)
    acc_ref[...] = jnp.zeros_like(acc_ref)

    def ctx_part(t, carry):
        fwd(t)
        bwd(n_ctx_chunks - 1 - t)
        return carry

    def lat_part(t, carry):
        fwd(t)
        bwd(n_chunks - 1 - (t - n_ctx_chunks))
        return carry

    lax.fori_loop(0, n_ctx_chunks, ctx_part, 0, unroll=2)
    lax.fori_loop(n_ctx_chunks, n_chunks, lat_part, 0, unroll=4)

    def norm(t, carry):
        rows = pl.ds(pl.multiple_of(t * c, c), c)
        o = acc_ref[rows, :]
        mu = jnp.mean(o, axis=-1, keepdims=True)
        oc = o - mu
        var = jnp.mean(oc * oc, axis=-1, keepdims=True)
        o_ref[rows, :] = (oc * lax.rsqrt(var + LN_EPS)).astype(BF16)
        return carry

    lax.fori_loop(0, n_chunks, norm, 0, unroll=6)


def _retention(mix, log_decay, n_ctx):
    b, t, _ = mix.shape
    kern = functools.partial(_ret_kernel, n_ctx_chunks=n_ctx // CHUNK, n_chunks=t // CHUNK)
    blk = lambda off: pl.BlockSpec((None, t, RET_DK), lambda bi, h: (bi, 0, off + h))
    return pl.pallas_call(
        kern,
        out_shape=jax.ShapeDtypeStruct((b, t, RET_W), BF16),
        grid=(b, RET_HEADS),
        in_specs=[pl.BlockSpec(memory_space=pltpu.SMEM), blk(0), blk(RET_HEADS), blk(2 * RET_HEADS)],
        out_specs=pl.BlockSpec((None, t, RET_DV), lambda bi, h: (bi, 0, h)),
        scratch_shapes=[pltpu.VMEM((t, RET_DV), F32), pltpu.VMEM((RET_DK, RET_DV), F32),
                        pltpu.VMEM((RET_DK, RET_DV), F32)],
        compiler_params=_params("parallel", "arbitrary"),
        name="retention",
    )(log_decay, mix, mix, mix)


def _att_kernel(q_ref, k_ref, v_ref, o_ref, v1_ref, q4_ref, s_ref, mx_ref, m_ref, acc_ref, *, chunks, n_ctx, t):
    i = pl.program_id(2)
    contract1 = (((1,), (1,)), ((), ()))
    tq = q_ref.shape[0]
    sb = ATT_SUB

    @pl.when(i == 0)
    def _():
        v1_ref[:, :ATT_HD] = v_ref[...]
        v1_ref[:, ATT_HD:] = jnp.ones((t, ATT_HD), BF16)

    for g in range(ATT_GROUP):
        q4_ref[g * tq:(g + 1) * tq, :] = q_ref[:, g * ATT_HD:(g + 1) * ATT_HD]
    neg = jnp.full(mx_ref.shape, -jnp.inf, F32)
    mx_ref[...] = neg
    m_ref[...] = neg
    acc_ref[...] = jnp.zeros(acc_ref.shape, F32)

    def a_step(buf, j, r0, size):
        k = k_ref[pl.ds(r0, size), :]
        s = lax.dot_general(q4_ref[...], k, contract1, preferred_element_type=F32)
        s_ref[buf, j, :, 0:size] = s
        mx = mx_ref[...]
        for tt in range(size // LANES):
            mx = jnp.maximum(mx, s[:, tt * LANES:(tt + 1) * LANES])
        mx_ref[...] = mx

    def settle():
        mx = mx_ref[...]
        mc = jnp.broadcast_to(jnp.max(mx, axis=-1, keepdims=True), mx.shape)
        m_old = m_ref[...]
        m_new = jnp.maximum(m_old, mc)
        alpha = jnp.exp2(m_old - m_new)
        m_ref[...] = m_new
        acc_ref[:, :ATT_HD] = acc_ref[:, :ATT_HD] * alpha
        acc_ref[:, ATT_HD:] = acc_ref[:, ATT_HD:] * alpha
        mx_ref[...] = neg

    def b_step(buf, j, r0, size):
        mb = m_ref[...]
        p = jnp.concatenate([jnp.exp2(s_ref[buf, j, :, tt * LANES:(tt + 1) * LANES] - mb)
                             for tt in range(size // LANES)], axis=1).astype(BF16)
        acc_ref[...] += jnp.dot(p, v1_ref[pl.ds(r0, size), :], preferred_element_type=F32)

    def stage(ca, cb):
        sides = [(a_step, ca), (b_step, cb)]
        sides = [(fn, ch) for fn, ch in sides if ch is not None]
        common = min(ch[2] // sb for _, ch in sides)

        def body(j, carry):
            off = pl.multiple_of(j * sb, sb)
            for fn, (buf, start, _) in sides:
                fn(buf, j, start + off, sb)
            return carry

        if common:
            lax.fori_loop(0, common, body, 0)
        for fn, (buf, start, size) in sides:
            for j in range(common, -(-size // sb)):
                fn(buf, j, start + j * sb, min(sb, size - j * sb))
        if ca is not None:
            settle()

    @pl.when(i == 0)
    def _():
        stage((0, 0, n_ctx), None)
        stage(None, (0, 0, n_ctx))

    @pl.when(i > 0)
    def _():
        bufs = [(c % 2, start, size) for c, (start, size) in enumerate(chunks)]
        for c in range(len(bufs) + 1):
            stage(bufs[c] if c < len(bufs) else None, bufs[c - 1] if c > 0 else None)

    for g in range(ATT_GROUP):
        rows = slice(g * tq, (g + 1) * tq)
        o_ref[:, g * ATT_HD:(g + 1) * ATT_HD] = (acc_ref[rows, :ATT_HD] / acc_ref[rows, ATT_HD:]).astype(BF16)


def _attention(mix, n_ctx):
    b, t, _ = mix.shape
    tq = ROW_TILE
    n_lat = t - n_ctx
    nch = 4 if n_lat % (4 * COL_TILE) == 0 else 1
    chunks = tuple((0, n_ctx + n_lat // nch) if c == 0 else (n_ctx + c * (n_lat // nch), n_lat // nch)
                   for c in range(nch))
    mrows = ATT_GROUP * tq
    q_blk0 = (3 * RET_W) // (ATT_GROUP * ATT_HD)
    k_blk0 = (3 * RET_W + ATT_W) // ATT_HD
    v_blk0 = k_blk0 + ATT_KV_HEADS
    kern = functools.partial(_att_kernel, chunks=chunks, n_ctx=n_ctx, t=t)
    return pl.pallas_call(
        kern,
        out_shape=jax.ShapeDtypeStruct((b, t, ATT_W), BF16),
        grid=(b, ATT_KV_HEADS, t // tq),
        in_specs=[pl.BlockSpec((None, tq, ATT_GROUP * ATT_HD), lambda bi, hk, i: (bi, i, q_blk0 + hk)),
                  pl.BlockSpec((None, t, ATT_HD), lambda bi, hk, i: (bi, 0, k_blk0 + hk)),
                  pl.BlockSpec((None, t, ATT_HD), lambda bi, hk, i: (bi, 0, v_blk0 + hk))],
        out_specs=pl.BlockSpec((None, tq, ATT_GROUP * ATT_HD), lambda bi, hk, i: (bi, i, hk)),
        scratch_shapes=[pltpu.VMEM((t, 2 * ATT_HD), BF16),
                        pltpu.VMEM((mrows, ATT_HD), BF16),
                        pltpu.VMEM((2, -(-chunks[0][1] // ATT_SUB), mrows, ATT_SUB), F32),
                        pltpu.VMEM((mrows, LANES), F32),
                        pltpu.VMEM((mrows, LANES), F32),
                        pltpu.VMEM((mrows, 2 * ATT_HD), F32)],
        compiler_params=_params("parallel", "parallel", "arbitrary"),
        name="gqa_attention",
    )(mix, mix, mix)


def _s5_kernel(u_ref, win_ref, tsum_ref, wout_ref, lam_ref, y_ref, uc_ref, d_ref, xin_ref, yw_ref, *, nc, ncc):
    p2 = 2 * S5_STATE
    sub = 8
    ln = S5_CHUNK
    gpb = LANES // S5_GROUP
    us = [pltpu.bitcast(u_ref[pl.ds(s, nc, stride=ln), :].astype(BF16), jnp.uint32) for s in range(ln)]
    for g in range(gpb):
        uc = pltpu.bitcast(
            jnp.concatenate([us[s][:, g * S5_GROUP:(g + 1) * S5_GROUP] for s in range(ln)], axis=1), BF16)
        uc_ref[g] = uc
        d_ref[g] = jnp.dot(uc, win_ref[g], preferred_element_type=F32)
    rowid = lax.broadcasted_iota(jnp.int32, (sub, p2), 0)

    def tile(g, reverse, rows, c, cs):
        base = 32 if reverse else 0
        col = 2 * p2 if reverse else 0

        def mult(i):
            return tuple(jnp.broadcast_to(lam_ref[g, base + 3 * i + r:base + 3 * i + r + 1, :], (sub, p2))
                         for r in range(3))

        def shifted(v, n):
            if reverse:
                return jnp.where(rowid < sub - n, pltpu.roll(v, sub - n, 0), 0.0)
            return jnp.where(rowid >= n, pltpu.roll(v, n, 0), 0.0)

        x = d_ref[g, rows, col:col + p2]
        xs = d_ref[g, rows, col + p2:col + 2 * p2]
        for i, n in enumerate((1, 2, 4)):
            a, b, bs = mult(i)
            xr, xsr = shifted(x, n), shifted(xs, n)
            x, xs = x + a * xr + b * xsr, xs + a * xsr + bs * xr
        pa = lam_ref[g, base + 16:base + 24, :]
        pb = lam_ref[g, base + 24:base + 32, :]
        xin_ref[g, rows, col // 2:col // 2 + p2] = shifted(x, 1) + pa * c + pb * cs
        edge = 0 if reverse else sub - 1
        last = jnp.broadcast_to(x[edge:edge + 1, :], (sub, p2))
        lasts = jnp.broadcast_to(xs[edge:edge + 1, :], (sub, p2))
        a, b, bs = mult(3)
        return a * c + b * cs + last, a * cs + bs * c + lasts

    nt = nc // sub
    nct = ncc // sub

    def step(mf, mb, carry):
        rf = pl.ds(pl.multiple_of(mf * sub, sub), sub)
        rb = pl.ds(pl.multiple_of(mb * sub, sub), sub)
        return tuple(tile(g, False, rf, carry[g][0], carry[g][1]) + tile(g, True, rb, carry[g][2], carry[g][3])
                     for g in range(gpb))

    zero = jnp.zeros((sub, p2), F32)
    init = tuple((zero, zero, zero, zero) for _ in range(gpb))
    carry = lax.fori_loop(0, nct, lambda t, cr: step(t, nct - 1 - t, cr), init)
    lax.fori_loop(nct, nt, lambda t, cr: step(t, nt - 1 - (t - nct), cr), carry)

    for g in range(gpb):
        y = jnp.dot(uc_ref[g], tsum_ref[g], preferred_element_type=F32)
        y += jnp.dot(xin_ref[g].astype(BF16), wout_ref[g], preferred_element_type=F32)
        yw_ref[g] = pltpu.bitcast(y.astype(BF16), jnp.uint32)
    for s in range(ln):
        piece = jnp.concatenate([yw_ref[g, :, s * S5_GROUP:(s + 1) * S5_GROUP] for g in range(gpb)], axis=1)
        y_ref[pl.ds(s, nc, stride=ln), :] = pltpu.bitcast(piece, BF16).astype(F32)


def _s5_weights(a_re, a_im, log_dt, b_re, b_im, c_re, c_im, d_skip):
    hi = lax.Precision.HIGHEST
    ln = S5_CHUNK
    dt = jnp.exp(log_dt)[..., None]
    mag = jnp.exp(a_re * dt)
    abr, abi = mag * jnp.cos(a_im * dt), mag * jnp.sin(a_im * dt)
    den = a_re * a_re + a_im * a_im
    fr = ((abr - 1.0) * a_re + abi * a_im) / den
    fi = (abi * a_re - (abr - 1.0) * a_im) / den
    bbr = fr[..., None] * b_re - fi[..., None] * b_im
    bbi = fr[..., None] * b_im + fi[..., None] * b_re
    n = jnp.arange(ln + 1, dtype=F32)[:, None, None, None]
    pmag = jnp.exp(a_re * dt * n)
    pr, pi = pmag * jnp.cos(a_im * dt * n), pmag * jnp.sin(a_im * dt * n)
    cr = c_re[None] * pr[:, :, :, None, :] - c_im[None] * pi[:, :, :, None, :]
    ci = c_re[None] * pi[:, :, :, None, :] + c_im[None] * pr[:, :, :, None, :]
    g_ = a_re.shape[1]
    lhs = jnp.concatenate([cr[:ln], -ci[:ln]], axis=-1).transpose(1, 2, 0, 3, 4)
    lhs = lhs.reshape(2, g_, ln * S5_GROUP, 2 * S5_STATE)
    rhs = jnp.concatenate([bbr, bbi], axis=2)
    m = jnp.einsum('dgmk,dgke->dgme', lhs, rhs, precision=hi)
    m = m.reshape(2, g_, ln, S5_GROUP, S5_GROUP).transpose(2, 0, 1, 3, 4)
    lag = jnp.arange(ln)[None, :] - jnp.arange(ln)[:, None]
    n_idx = jnp.arange(ln)[:, None, None]
    oh_f = (lag[None] == n_idx).astype(F32)
    oh_b = (-lag[None] == n_idx).astype(F32)
    tsum = (jnp.einsum('nsi,ngce->gseic', oh_f, m[:, 0], precision=hi)
            + jnp.einsum('nsi,ngce->gseic', oh_b, m[:, 1], precision=hi))
    g = a_re.shape[1]
    eye_s = jnp.eye(ln, dtype=F32)[None, :, None, :, None]
    eye_c = jnp.eye(S5_GROUP, dtype=F32)[None, None, :, None, :]
    tsum = tsum + eye_s * eye_c * d_skip.reshape(g, 1, S5_GROUP, 1, 1)
    tsum = tsum.reshape(g, ln * S5_GROUP, ln * S5_GROUP)

    def w_in(pw_r, pw_i, d):
        re = pw_r[..., None] * bbr[d][None] - pw_i[..., None] * bbi[d][None]
        im = pw_r[..., None] * bbi[d][None] + pw_i[..., None] * bbr[d][None]
        re = re.transpose(1, 0, 3, 2).reshape(g, ln * S5_GROUP, S5_STATE)
        im = im.transpose(1, 0, 3, 2).reshape(g, ln * S5_GROUP, S5_STATE)
        return jnp.concatenate([re, im, im, re], axis=-1)

    win = jnp.concatenate([w_in(pr[:ln, 0][::-1], pi[:ln, 0][::-1], 0),
                           w_in(pr[:ln, 1], pi[:ln, 1], 1)], axis=-1)

    def w_out(cr_d, ci_d):
        re = cr_d.transpose(1, 3, 0, 2).reshape(g, S5_STATE, ln * S5_GROUP)
        im = (-ci_d).transpose(1, 3, 0, 2).reshape(g, S5_STATE, ln * S5_GROUP)
        return jnp.concatenate([re, im], axis=1)

    wout = jnp.concatenate([w_out(cr[1:ln + 1, 0], ci[1:ln + 1, 0]),
                            w_out(cr[1:ln + 1, 1][::-1], ci[1:ln + 1, 1][::-1])], axis=1)

    def cpow(k):
        e = (jnp.asarray(k, F32) * ln)[..., None, None, None]
        mg = jnp.exp(a_re * dt * e)
        return mg * jnp.cos(a_im * dt * e), mg * jnp.sin(a_im * dt * e)

    mr, mi = cpow(jnp.array([1, 2, 4, 8]))
    rows = []
    for d in range(2):
        for i in range(4):
            r_, i_ = mr[i, d], mi[i, d]
            rows += [jnp.concatenate([r_, r_], -1), jnp.concatenate([-i_, i_], -1), jnp.concatenate([i_, -i_], -1)]
        rows += [jnp.zeros_like(rows[0])] * 4
        j = jnp.arange(8)
        qr, qi = cpow(j if d == 0 else 7 - j)
        rows += [jnp.concatenate([qr[k, d], qr[k, d]], -1) for k in range(8)]
        rows += [jnp.concatenate([-qi[k, d], qi[k, d]], -1) for k in range(8)]
    lam = jnp.stack(rows, axis=1)
    return win.astype(BF16), tsum.astype(BF16), wout.astype(BF16), lam


def _s5_in_kernel(u_ref, w_ref, o_ref):
    o_ref[...] = jnp.dot(u_ref[...], w_ref[...], preferred_element_type=F32)


def _s5_in_proj(u, w_s5):
    b, t, d = u.shape
    tm = _big_row_tile(t)
    return pl.pallas_call(
        _s5_in_kernel,
        out_shape=jax.ShapeDtypeStruct((b, t, S5_W), F32),
        grid=(b, t // tm),
        in_specs=[pl.BlockSpec((None, tm, d), lambda bi, i: (bi, i, 0)),
                  pl.BlockSpec((d, S5_W), lambda bi, i: (0, 0))],
        out_specs=pl.BlockSpec((None, tm, S5_W), lambda bi, i: (bi, i, 0)),
        compiler_params=_params("parallel", "arbitrary"),
        name="s5_in_proj",
    )(u, w_s5)


def _s5(u5, s5w, layer, n_ctx):
    b, t, _ = u5.shape
    win, tsum, wout, lam = s5w
    gpb = LANES // S5_GROUP
    nc = t // S5_CHUNK
    kw = S5_CHUNK * S5_GROUP
    assert (n_ctx // S5_CHUNK) % 8 == 0 and nc % 8 == 0
    kern = functools.partial(_s5_kernel, nc=nc, ncc=n_ctx // S5_CHUNK)
    wspec = lambda r, c: pl.BlockSpec((None, gpb, r, c), lambda bi, gb: (layer, gb, 0, 0))
    return pl.pallas_call(
        kern,
        out_shape=jax.ShapeDtypeStruct((b, t, S5_W), F32),
        grid=(b, S5_W // LANES),
        in_specs=[pl.BlockSpec((None, t, LANES), lambda bi, gb: (bi, 0, gb)),
                  wspec(kw, 8 * S5_STATE), wspec(kw, kw), wspec(4 * S5_STATE, kw), wspec(64, 2 * S5_STATE)],
        out_specs=pl.BlockSpec((None, t, LANES), lambda bi, gb: (bi, 0, gb)),
        scratch_shapes=[pltpu.VMEM((gpb, nc, kw), BF16),
                        pltpu.VMEM((gpb, nc, 8 * S5_STATE), F32),
                        pltpu.VMEM((gpb, nc, 4 * S5_STATE), F32),
                        pltpu.VMEM((gpb, nc // 2, kw), jnp.uint32)],
        compiler_params=_params("parallel", "parallel"),
        name="s5_scan",
    )(u5, win, tsum, wout, lam)


def _glu_kernel(s_ref, u_ref, wg_ref, w_ref, b_ref, o_ref):
    s = jax.nn.gelu(s_ref[...])
    z = jnp.dot(s.astype(BF16), w_ref[...], preferred_element_type=F32) + b_ref[...]
    g = jnp.dot(u_ref[...], wg_ref[...], preferred_element_type=F32)
    o_ref[...] = (s * jax.nn.sigmoid(z) * (g * jax.nn.sigmoid(g))).astype(BF16)


def _s5_glu_gate(s_pre, u, w_s5, glu_w, glu_b):
    b, t, d = u.shape
    tm = _big_row_tile(t)
    return pl.pallas_call(
        _glu_kernel,
        out_shape=jax.ShapeDtypeStruct((b, t, S5_W), BF16),
        grid=(b, t // tm),
        in_specs=[pl.BlockSpec((None, tm, S5_W), lambda bi, i: (bi, i, 0)),
                  pl.BlockSpec((None, tm, d), lambda bi, i: (bi, i, 0)),
                  pl.BlockSpec((d, S5_W), lambda bi, i: (0, 1)),
                  pl.BlockSpec((S5_W, S5_W), lambda bi, i: (0, 0)),
                  pl.BlockSpec((1, S5_W), lambda bi, i: (0, 0))],
        out_specs=pl.BlockSpec((None, tm, S5_W), lambda bi, i: (bi, i, 0)),
        compiler_params=_params("parallel", "arbitrary"),
        name="s5_glu_gate",
    )(s_pre, u, w_s5, glu_w, glu_b)


def _gate_kernel(u_ref, w_ref, r_ref, a_ref, gn_ref, o_ref):
    j = pl.program_id(2)
    tm = u_ref.shape[0]
    nsplit = 2 if tm % 32 == 0 else 1
    hr = tm // nsplit

    def halves(act_ref, scale):
        for r in range(nsplit):
            rows = slice(r * hr, (r + 1) * hr)
            g = jnp.dot(u_ref[rows, :], w_ref[...], preferred_element_type=F32)
            act = act_ref[rows, :].astype(F32)
            if scale is not None:
                act = act * scale
            o_ref[rows, :] = (act * (g * jax.nn.sigmoid(g))).astype(BF16)

    @pl.when(j < 2)
    def _():
        halves(r_ref, gn_ref[...])

    @pl.when(j >= 2)
    def _():
        halves(a_ref, None)


def _branch_gates(u, wb, layer, r, a, gn_w):
    b, t, d = u.shape
    tm = _big_row_tile(t, 2112)
    ret_g0 = 3 * RET_W // COL_TILE
    att_g0 = (4 * RET_W + ATT_W + 2 * ATT_KV_W) // COL_TILE
    act = lambda off: pl.BlockSpec((None, tm, COL_TILE), lambda bi, i, j: (bi, i, jnp.clip(j - off, 0, 1)))
    return pl.pallas_call(
        _gate_kernel,
        out_shape=jax.ShapeDtypeStruct((b, t, RET_W + ATT_W), BF16),
        grid=(b, t // tm, (RET_W + ATT_W) // COL_TILE),
        in_specs=[pl.BlockSpec((None, tm, d), lambda bi, i, j: (bi, i, 0)),
                  pl.BlockSpec((None, d, COL_TILE),
                               lambda bi, i, j: (layer, 0, jnp.where(j < 2, ret_g0 + j, att_g0 + j - 2))),
                  act(0), act(2),
                  pl.BlockSpec((1, COL_TILE), lambda bi, i, j: (0, jnp.clip(j, 0, 1)))],
        out_specs=pl.BlockSpec((None, tm, COL_TILE), lambda bi, i, j: (bi, i, j)),
        compiler_params=_params("parallel", "parallel", "arbitrary"),
        name="branch_gates",
    )(u, wb, r, a, gn_w)


def _merge_kernel(u_ref, zra_ref, zs_ref, wm0_ref, wm1_ref, wm2_ref, wr_ref, wa_ref, ws_ref, o_ref):
    tm = u_ref.shape[0]
    nsplit = 2 if tm % 32 == 0 else 1
    hr = tm // nsplit
    for r in range(nsplit):
        rows = slice(r * hr, (r + 1) * hr)
        u = u_ref[rows, :]
        acts = (zra_ref[rows, :RET_W], zra_ref[rows, RET_W:], zs_ref[rows, :])
        acc = None
        for act, wm_ref, wbr_ref in zip(acts, (wm0_ref, wm1_ref, wm2_ref), (wr_ref, wa_ref, ws_ref)):
            gate = jax.nn.sigmoid(jnp.dot(u, wm_ref[...], preferred_element_type=F32))
            proj = jnp.dot(act, wbr_ref[...], preferred_element_type=F32)
            acc = gate * proj if acc is None else acc + gate * proj
        o_ref[rows, :] = acc.astype(BF16)


def _merge(u, z_ra, z_s, wb, layer, w_ret, w_att, w_s5):
    b, t, d = u.shape
    tm = _big_row_tile(t)
    m0 = (wb.shape[2] - N_BRANCH * d) // COL_TILE
    per = d // COL_TILE
    wm = lambda br: pl.BlockSpec((None, d, COL_TILE), lambda bi, i, j: (layer, 0, m0 + br * per + j))
    wbr = lambda k: pl.BlockSpec((k, COL_TILE), lambda bi, i, j: (0, j))
    return pl.pallas_call(
        _merge_kernel,
        out_shape=jax.ShapeDtypeStruct((b, t, d), BF16),
        grid=(b, t // tm, d // COL_TILE),
        in_specs=[pl.BlockSpec((None, tm, d), lambda bi, i, j: (bi, i, 0)),
                  pl.BlockSpec((None, tm, RET_W + ATT_W), lambda bi, i, j: (bi, i, 0)),
                  pl.BlockSpec((None, tm, S5_W), lambda bi, i, j: (bi, i, 0)),
                  wm(0), wm(1), wm(2), wbr(RET_W), wbr(ATT_W), wbr(S5_W)],
        out_specs=pl.BlockSpec((None, tm, COL_TILE), lambda bi, i, j: (bi, i, j)),
        compiler_params=_params("parallel", "parallel", "arbitrary"),
        name="gated_merge",
    )(u, z_ra, z_s, wb, wb, wb, w_ret, w_att, w_s5)


def _out_kernel(*refs, alpha, first_tile, emit_u):
    if emit_u:
        m_ref, w_ref, c_ref, x_ref, mod_ref, lnw_ref, lnb_ref, nmod_ref, o_ref, u_ref, v_ref = refs
    else:
        m_ref, w_ref, c_ref, x_ref, mod_ref, lnw_ref, lnb_ref, o_ref, v_ref = refs
    i = pl.program_id(1)
    d = w_ref.shape[1]

    def run(res_ref):
        m = m_ref[...]
        s1 = jnp.zeros((m.shape[0], 1), F32)
        for c in range(d // COL_TILE):
            cols = slice(c * COL_TILE, (c + 1) * COL_TILE)
            y = jnp.dot(m, w_ref[:, cols], preferred_element_type=F32)
            v = alpha * res_ref[:, cols] + mod_ref[2:3, cols] * y
            v_ref[:, cols] = v
            s1 += jnp.sum(v, axis=-1, keepdims=True)
        mu = s1 * (1.0 / d)
        vc = v_ref[...] - mu
        var = jnp.mean(vc * vc, axis=-1, keepdims=True)
        out = vc * lax.rsqrt(var + LN_EPS) * lnw_ref[...] + lnb_ref[...]
        o_ref[...] = out
        if emit_u:
            u_ref[...] = (out * (1.0 + nmod_ref[1:2, :]) + nmod_ref[0:1, :]).astype(BF16)

    if first_tile == 0:
        @pl.when(i == 0)
        def _():
            run(c_ref)

        @pl.when(i > 0)
        def _():
            run(x_ref)
    else:
        run(x_ref)


def _out_proj(m, w_out, ctx_src, lat_src, modsel, ln_w, ln_b, alpha, skip_ctx, next_modsel=None):
    b, t, d = m.shape
    off = 1 if skip_ctx else 0
    nt = t // ROW_TILE - off
    emit_u = next_modsel is not None
    kern = functools.partial(_out_kernel, alpha=alpha, first_tile=off, emit_u=emit_u)
    vec = lambda: pl.BlockSpec((1, d), lambda bi, i: (0, 0))
    mod = lambda: pl.BlockSpec((None, None, 3, d), lambda bi, i: (bi, jnp.minimum(i + off, 1), 0, 0))
    row_out = lambda: pl.BlockSpec((None, ROW_TILE, d), lambda bi, i: (bi, i, 0))
    x_shape = jax.ShapeDtypeStruct((b, nt * ROW_TILE, d), F32)
    return pl.pallas_call(
        kern,
        out_shape=(x_shape, jax.ShapeDtypeStruct(x_shape.shape, BF16)) if emit_u else x_shape,
        grid=(b, nt),
        in_specs=[pl.BlockSpec((None, ROW_TILE, d), lambda bi, i: (bi, i + off, 0)),
                  pl.BlockSpec((d, d), lambda bi, i: (0, 0)),
                  *_token_specs(ctx_src, lat_src, off),
                  mod(), vec(), vec(), *([mod()] if emit_u else [])],
        out_specs=(row_out(), row_out()) if emit_u else row_out(),
        scratch_shapes=[pltpu.VMEM((ROW_TILE, d), F32)],
        compiler_params=_params("parallel", "arbitrary"),
        name="out_proj_norm",
    )(m, w_out, ctx_src, lat_src, modsel, ln_w.reshape(1, d), ln_b.reshape(1, d),
      *([next_modsel] if emit_u else []))


def _rope_tables(n, n_ctx):
    rows = n // GRID_W
    row = jnp.repeat(jnp.arange(rows, dtype=F32), GRID_W)
    col = jnp.tile(jnp.arange(GRID_W, dtype=F32), rows)

    def table(head_dim):
        per_axis = head_dim // 4
        inv = ROPE_THETA ** (-jnp.arange(per_axis, dtype=F32) / per_axis)
        ang = jnp.concatenate([row[:, None] * inv, col[:, None] * inv], axis=-1)
        cos = jnp.concatenate([jnp.ones((n_ctx, head_dim // 2), F32), jnp.cos(ang)], axis=0)
        sin = jnp.concatenate([jnp.zeros((n_ctx, head_dim // 2), F32), jnp.sin(ang)], axis=0)
        return cos, sin

    cos_r, sin_r = table(RET_DK)
    cos_a, sin_a = table(ATT_HD)
    return (cos_r, sin_r, jnp.concatenate([cos_a, cos_a], -1), jnp.concatenate([-sin_a, sin_a], -1))


def kernel(x, c, ctx, c_ctx, ada_w, ada_b, w_in, ret_log_decay, ret_gn_w, att_q_norm, att_k_norm, s5_a_re, s5_a_im, s5_log_dt, s5_b_re, s5_b_im, s5_c_re, s5_c_im, s5_d, s5_glu_w, s5_glu_b, w_br_ret, w_br_att, w_br_s5, w_out, ln_w, ln_b):
    b, n, d = x.shape
    n_ctx = ctx.shape[1]
    depth = w_in.shape[0]
    assert n_ctx == ROW_TILE and n % ROW_TILE == 0 and n % GRID_W == 0
    assert w_in.shape[2] == 4 * RET_W + 2 * ATT_W + 2 * ATT_KV_W + 2 * S5_W + N_BRANCH * d
    alpha = (2.0 * depth) ** 0.25
    s5_col0 = 4 * RET_W + 2 * ATT_W + 2 * ATT_KV_W

    t = n_ctx + n
    ctx_src, lat_src = ctx, x
    cos_r, sin_r, cos_a, sin_a = _rope_tables(n, n_ctx)
    crows = 16
    cvec = jnp.concatenate([c, c_ctx[None, :], jnp.zeros((crows - b - 1, d), F32)], axis=0)
    wb = w_in.astype(BF16)
    s5w = jax.vmap(_s5_weights)(s5_a_re, s5_a_im, s5_log_dt, s5_b_re, s5_b_im, s5_c_re, s5_c_im, s5_d)

    modsels = []
    for l in range(depth):
        mod = _modulation(cvec, ada_w, ada_b, l).reshape(crows, 3, d)
        modsels.append(jnp.stack([jnp.broadcast_to(mod[b][None], (b, 3, d)), mod[:b]], axis=1))

    u = _modulate(ctx_src, lat_src, t, modsels[0])
    for l in range(depth):
        last = l == depth - 1
        modsel = modsels[l]
        w_s5 = wb[l, :, s5_col0:s5_col0 + 2 * S5_W]
        mix = _mix_proj(u, wb, l, cos_r, sin_r, cos_a, sin_a,
                        att_q_norm[l].reshape(1, ATT_HD), att_k_norm[l].reshape(1, ATT_HD))
        r = _retention(mix, ret_log_decay[l], n_ctx)
        a = _attention(mix, n_ctx)
        s_pre = _s5(_s5_in_proj(u, w_s5), s5w, l, n_ctx)
        z_s = _s5_glu_gate(s_pre, u, w_s5, s5_glu_w[l].astype(BF16), s5_glu_b[l].reshape(1, S5_W))
        z_ra = _branch_gates(u, wb, l, r, a, ret_gn_w[l].reshape(1, RET_W))
        m = _merge(u, z_ra, z_s, wb, l, w_br_ret[l].astype(BF16), w_br_att[l].astype(BF16),
                   w_br_s5[l].astype(BF16))
        res = _out_proj(m, w_out[l].astype(BF16), ctx_src, lat_src, modsel, ln_w[l], ln_b[l], alpha, last,
                        None if last else modsels[l + 1])
        if last:
            return res
        xa, u = res
        ctx_src = lat_src = xa
```

```python
import functools
import math

import jax
import jax.numpy as jnp
from jax import lax
from jax.experimental import pallas as pl
from jax.experimental.pallas import tpu as pltpu

F32 = jnp.float32
BF16 = jnp.bfloat16

GRID_W = 64
RET_HEADS = 4
RET_DK = 256
RET_DV = 256
RET_W = RET_HEADS * RET_DV
ATT_HEADS = 8
ATT_KV_HEADS = 2
ATT_GROUP = ATT_HEADS // ATT_KV_HEADS
ATT_HD = 128
ATT_W = ATT_HEADS * ATT_HD
ATT_KV_W = ATT_KV_HEADS * ATT_HD
ROPE_THETA = 10000.0
S5_GROUP = 16
S5_W = 768
S5_GROUPS = S5_W // S5_GROUP
S5_STATE = 64
N_BRANCH = 3
LN_EPS = 1e-6
RMS_EPS = 1e-6

LANES = 128
VMEM_LIMIT_BYTES = 56 * 1024 * 1024

CHUNK = 128
S5_CHUNK = 16
ROW_TILE = 256
COL_TILE = 512
MIX_W = 4608
ATT_SUB = 2048
LOG2E = math.log2(math.e)


def _params(*sem):
    return pltpu.CompilerParams(dimension_semantics=sem, vmem_limit_bytes=VMEM_LIMIT_BYTES)


def _big_row_tile(t, largest=1056):
    for cand in (2112, 1056, 1024, 768, 640, 512, 256):
        if cand <= largest and t % cand == 0:
            return cand
    raise ValueError(f"unsupported token count {t}")


def _mod_kernel(c_ref, w_ref, b_ref, o_ref):
    c = c_ref[...]
    s = c * jax.nn.sigmoid(c)
    s_hi = s.astype(BF16)
    s_lo = (s - s_hi.astype(F32)).astype(BF16)
    w = w_ref[...]
    w_hi = w.astype(BF16)
    w_lo = (w - w_hi.astype(F32)).astype(BF16)
    acc = jnp.dot(s_hi, w_hi, preferred_element_type=F32)
    acc += jnp.dot(s_lo, w_hi, preferred_element_type=F32)
    acc += jnp.dot(s_hi, w_lo, preferred_element_type=F32)
    o_ref[...] = acc + b_ref[...]


def _modulation(cvec, ada_w, ada_b, layer):
    rows, d = cvec.shape
    n = ada_w.shape[2]
    tn = 768
    return pl.pallas_call(
        _mod_kernel,
        out_shape=jax.ShapeDtypeStruct((rows, n), F32),
        grid=(n // tn,),
        in_specs=[pl.BlockSpec((rows, d), lambda j: (0, 0)),
                  pl.BlockSpec((None, d, tn), lambda j: (layer, 0, j)),
                  pl.BlockSpec((None, 1, tn), lambda j: (layer, 0, j))],
        out_specs=pl.BlockSpec((rows, tn), lambda j: (0, j)),
        compiler_params=_params("arbitrary"),
        name="adaln_modulation",
    )(cvec, ada_w, ada_b.reshape(ada_b.shape[0], 1, n))


def _modulate_kernel(c_ref, x_ref, mod_ref, o_ref):
    i = pl.program_id(1)
    m = mod_ref[...]

    @pl.when(i == 0)
    def _():
        o_ref[...] = (c_ref[...] * (1.0 + m[1:2]) + m[0:1]).astype(BF16)

    @pl.when(i > 0)
    def _():
        o_ref[...] = (x_ref[...] * (1.0 + m[1:2]) + m[0:1]).astype(BF16)


def _token_specs(ctx_src, lat_src, first_tile=0):
    d = ctx_src.shape[2]
    shift = first_tile - (0 if lat_src is ctx_src else 1)
    return (pl.BlockSpec((None, ROW_TILE, d), lambda bi, i: (bi, 0, 0)),
            pl.BlockSpec((None, ROW_TILE, d), lambda bi, i: (bi, jnp.maximum(i + shift, 0), 0)))


def _modulate(ctx_src, lat_src, t, modsel):
    b, _, d = ctx_src.shape
    return pl.pallas_call(
        _modulate_kernel,
        out_shape=jax.ShapeDtypeStruct((b, t, d), BF16),
        grid=(b, t // ROW_TILE),
        in_specs=[*_token_specs(ctx_src, lat_src),
                  pl.BlockSpec((None, None, 3, d), lambda bi, i: (bi, jnp.minimum(i, 1), 0, 0))],
        out_specs=pl.BlockSpec((None, ROW_TILE, d), lambda bi, i: (bi, i, 0)),
        compiler_params=_params("parallel", "arbitrary"),
        name="modulate",
    )(ctx_src, lat_src, modsel)


def _mix_kernel(u_ref, wa_ref, wb_ref, cr_ref, sr_ref, ca_ref, sa_ref, qn_ref, kn_ref, o_ref):
    u = u_ref[...]

    def proj(w_ref, tile):
        return jnp.dot(u, w_ref[:, tile * COL_TILE:(tile + 1) * COL_TILE], preferred_element_type=F32)

    def rms_rope(x, w, scale):
        ss = jnp.dot((x * x).astype(BF16), jnp.ones((ATT_HD, ATT_HD), BF16), preferred_element_type=F32)
        y = x * lax.rsqrt(ss * (1.0 / ATT_HD) + RMS_EPS) * w
        y = y * ca_ref[...] + pltpu.roll(y, ATT_HD // 2, 1) * sa_ref[...]
        return (y * scale).astype(BF16)

    cos = cr_ref[...]
    sin = sr_ref[...]
    for tile in range(4):
        acc = proj(wa_ref, tile)
        scale = RET_DK ** -0.5 if tile >= 2 else 1.0
        for h in range(COL_TILE // RET_DK):
            c0 = tile * COL_TILE + h * RET_DK
            x1 = acc[:, h * RET_DK: h * RET_DK + LANES]
            x2 = acc[:, h * RET_DK + LANES: (h + 1) * RET_DK]
            o_ref[:, c0:c0 + LANES] = ((x1 * cos - x2 * sin) * scale).astype(BF16)
            o_ref[:, c0 + LANES:c0 + RET_DK] = ((x2 * cos + x1 * sin) * scale).astype(BF16)
    for tile in range(4, 6):
        o_ref[:, tile * COL_TILE:(tile + 1) * COL_TILE] = proj(wa_ref, tile).astype(BF16)
    for tile in range(2):
        acc = proj(wb_ref, tile)
        for h in range(COL_TILE // ATT_HD):
            c0 = (6 + tile) * COL_TILE + h * ATT_HD
            o_ref[:, c0:c0 + ATT_HD] = rms_rope(acc[:, h * ATT_HD:(h + 1) * ATT_HD], qn_ref[...],
                                                ATT_HD ** -0.5 * LOG2E)
    acc = proj(wb_ref, 2)
    c0 = 8 * COL_TILE
    for h in range(ATT_KV_HEADS):
        o_ref[:, c0 + h * ATT_HD:c0 + (h + 1) * ATT_HD] = rms_rope(acc[:, h * ATT_HD:(h + 1) * ATT_HD],
                                                                   kn_ref[...], 1.0)
    o_ref[:, c0 + ATT_KV_W:] = acc[:, ATT_KV_W:].astype(BF16)


def _mix_proj(u, wb, layer, cos_r, sin_r, cos_a, sin_a, qn, kn):
    b, t, d = u.shape
    tm = 528 if t % 528 == 0 else ROW_TILE
    tab = lambda: pl.BlockSpec((tm, LANES), lambda bi, i: (i, 0))
    vec = lambda: pl.BlockSpec((1, LANES), lambda bi, i: (0, 0))
    ret_w = 3 * RET_W
    att_w = 2 * ATT_W
    assert (4 * RET_W) % att_w == 0
    resident = lambda width, blk: pl.BlockSpec((None, d, width), lambda bi, i: (layer, 0, blk),
                                               pipeline_mode=pl.Buffered(1))
    return pl.pallas_call(
        _mix_kernel,
        out_shape=jax.ShapeDtypeStruct((b, t, MIX_W), BF16),
        grid=(b, t // tm),
        in_specs=[pl.BlockSpec((None, tm, d), lambda bi, i: (bi, i, 0)),
                  resident(ret_w, 0), resident(att_w, 4 * RET_W // att_w),
                  tab(), tab(), tab(), tab(), vec(), vec()],
        out_specs=pl.BlockSpec((None, tm, MIX_W), lambda bi, i: (bi, i, 0)),
        compiler_params=_params("parallel", "arbitrary"),
        name="mixer_in_proj",
    )(u, wb, wb, cos_r, sin_r, cos_a, sin_a, qn, kn)


def _ret_kernel(ld_ref, q_ref, k_ref, v_ref, o_ref, acc_ref, sf_ref, sb_ref, *, n_ctx_chunks, n_chunks):
    h = pl.program_id(1)
    lgf = ld_ref[0, h]
    lgb = ld_ref[1, h]
    c = CHUNK
    ri = lax.broadcasted_iota(jnp.int32, (c, c), 0).astype(F32)
    ci = lax.broadcasted_iota(jnp.int32, (c, c), 1).astype(F32)
    rel = ri - ci
    mask = jnp.where(rel >= 0, jnp.exp(lgf * jnp.maximum(rel, 0.0)), jnp.exp(lgb * jnp.maximum(-rel, 0.0)))
    row = lax.broadcasted_iota(jnp.int32, (c, RET_DV), 0).astype(F32)
    qdec_f = jnp.exp(lgf * (row + 1.0))
    qdec_b = jnp.exp(lgb * (c - row))
    kdec_f = jnp.exp(lgf * (c - 1.0 - row))
    kdec_b = jnp.exp(lgb * row)
    cdec_f = jnp.exp(jnp.full((1, RET_DV), lgf * c, F32))
    cdec_b = jnp.exp(jnp.full((1, RET_DV), lgb * c, F32))
    contract0 = (((0,), (0,)), ((), ()))
    contract1 = (((1,), (1,)), ((), ()))

    def load(ci_):
        r0 = pl.multiple_of(ci_ * c, c)
        rows = pl.ds(r0, c)
        return rows, q_ref[rows, :], k_ref[rows, :], v_ref[rows, :]

    def bwd(ci_):
        rows, q, k, v = load(ci_)
        s = sb_ref[...]
        acc_ref[rows, :] += jnp.dot(q, s.astype(BF16), preferred_element_type=F32) * qdec_b
        kd = (k.astype(F32) * kdec_b).astype(BF16)
        sb_ref[...] = s * cdec_b + lax.dot_general(kd, v, contract0, preferred_element_type=F32)

    def fwd(ci_):
        rows, q, k, v = load(ci_)
        s = sf_ref[...]
        sc = lax.dot_general(q, k, contract1, preferred_element_type=F32) * mask
        o = jnp.dot(sc.astype(BF16), v, preferred_element_type=F32)
        o += jnp.dot(q, s.astype(BF16), preferred_element_type=F32) * qdec_f
        acc_ref[rows, :] += o
        kd = (k.astype(F32) * kdec_f).astype(BF16)
        sf_ref[...] = s * cdec_f + lax.dot_general(kd, v, contract0, preferred_element_type=F32)

    sf_ref[...] = jnp.zeros_like(sf_ref)
    sb_ref[...] = jnp.zeros_like(sb_ref)
    acc_ref[...] = jnp.zeros_like(acc_ref)

    def ctx_part(t, carry):
        fwd(t)
        bwd(n_ctx_chunks - 1 - t)
        return carry

    def lat_part(t, carry):
        fwd(t)
        bwd(n_chunks - 1 - (t - n_ctx_chunks))
        return carry

    lax.fori_loop(0, n_ctx_chunks, ctx_part, 0, unroll=2)
    lax.fori_loop(n_ctx_chunks, n_chunks, lat_part, 0, unroll=4)

    def norm(t, carry):
        rows = pl.ds(pl.multiple_of(t * c, c), c)
        o = acc_ref[rows, :]
        mu = jnp.mean(o, axis=-1, keepdims=True)
        oc = o - mu
        var = jnp.mean(oc * oc, axis=-1, keepdims=True)
        o_ref[rows, :] = (oc * lax.rsqrt(var + LN_EPS)).astype(BF16)
        return carry

    lax.fori_loop(0, n_chunks, norm, 0, unroll=6)


def _retention(mix, log_decay, n_ctx):
    b, t, _ = mix.shape
    kern = functools.partial(_ret_kernel, n_ctx_chunks=n_ctx // CHUNK, n_chunks=t // CHUNK)
    blk = lambda off: pl.BlockSpec((None, t, RET_DK), lambda bi, h: (bi, 0, off + h))
    return pl.pallas_call(
        kern,
        out_shape=jax.ShapeDtypeStruct((b, t, RET_W), BF16),
        grid=(b, RET_HEADS),
        in_specs=[pl.BlockSpec(memory_space=pltpu.SMEM), blk(0), blk(RET_HEADS), blk(2 * RET_HEADS)],
        out_specs=pl.BlockSpec((None, t, RET_DV), lambda bi, h: (bi, 0, h)),
        scratch_shapes=[pltpu.VMEM((t, RET_DV), F32), pltpu.VMEM((RET_DK, RET_DV), F32),
                        pltpu.VMEM((RET_DK, RET_DV), F32)],
        compiler_params=_params("parallel", "arbitrary"),
        name="retention",
    )(log_decay, mix, mix, mix)


def _att_kernel(q_ref, k_ref, v_ref, o_ref, v1_ref, q4_ref, s_ref, mx_ref, m_ref, acc_ref, *, chunks, n_ctx, t):
    i = pl.program_id(2)
    contract1 = (((1,), (1,)), ((), ()))
    tq = q_ref.shape[0]
    sb = ATT_SUB

    @pl.when(i == 0)
    def _():
        v1_ref[:, :ATT_HD] = v_ref[...]
        v1_ref[:, ATT_HD:] = jnp.ones((t, ATT_HD), BF16)

    for g in range(ATT_GROUP):
        q4_ref[g * tq:(g + 1) * tq, :] = q_ref[:, g * ATT_HD:(g + 1) * ATT_HD]
    neg = jnp.full(mx_ref.shape, -jnp.inf, F32)
    mx_ref[...] = neg
    m_ref[...] = neg
    acc_ref[...] = jnp.zeros(acc_ref.shape, F32)

    def a_step(buf, j, r0, size):
        k = k_ref[pl.ds(r0, size), :]
        s = lax.dot_general(q4_ref[...], k, contract1, preferred_element_type=F32)
        s_ref[buf, j, :, 0:size] = s
        mx = mx_ref[...]
        for tt in range(size // LANES):
            mx = jnp.maximum(mx, s[:, tt * LANES:(tt + 1) * LANES])
        mx_ref[...] = mx

    def settle():
        mx = mx_ref[...]
        mc = jnp.broadcast_to(jnp.max(mx, axis=-1, keepdims=True), mx.shape)
        m_old = m_ref[...]
        m_new = jnp.maximum(m_old, mc)
        alpha = jnp.exp2(m_old - m_new)
        m_ref[...] = m_new
        acc_ref[:, :ATT_HD] = acc_ref[:, :ATT_HD] * alpha
        acc_ref[:, ATT_HD:] = acc_ref[:, ATT_HD:] * alpha
        mx_ref[...] = neg

    def b_step(buf, j, r0, size):
        mb = m_ref[...]
        p = jnp.concatenate([jnp.exp2(s_ref[buf, j, :, tt * LANES:(tt + 1) * LANES] - mb)
                             for tt in range(size // LANES)], axis=1).astype(BF16)
        acc_ref[...] += jnp.dot(p, v1_ref[pl.ds(r0, size), :], preferred_element_type=F32)

    def stage(ca, cb):
        sides = [(a_step, ca), (b_step, cb)]
        sides = [(fn, ch) for fn, ch in sides if ch is not None]
        common = min(ch[2] // sb for _, ch in sides)

        def body(j, carry):
            off = pl.multiple_of(j * sb, sb)
            for fn, (buf, start, _) in sides:
                fn(buf, j, start + off, sb)
            return carry

        if common:
            lax.fori_loop(0, common, body, 0)
        for fn, (buf, start, size) in sides:
            for j in range(common, -(-size // sb)):
                fn(buf, j, start + j * sb, min(sb, size - j * sb))
        if ca is not None:
            settle()

    @pl.when(i == 0)
    def _():
        stage((0, 0, n_ctx), None)
        stage(None, (0, 0, n_ctx))

    @pl.when(i > 0)
    def _():
        bufs = [(c % 2, start, size) for c, (start, size) in enumerate(chunks)]
        for c in range(len(bufs) + 1):
            stage(bufs[c] if c < len(bufs) else None, bufs[c - 1] if c > 0 else None)

    for g in range(ATT_GROUP):
        rows = slice(g * tq, (g + 1) * tq)
        o_ref[:, g * ATT_HD:(g + 1) * ATT_HD] = (acc_ref[rows, :ATT_HD] / acc_ref[rows, ATT_HD:]).astype(BF16)


def _attention(mix, n_ctx):
    b, t, _ = mix.shape
    tq = ROW_TILE
    n_lat = t - n_ctx
    nch = 4 if n_lat % (4 * COL_TILE) == 0 else 1
    chunks = tuple((0, n_ctx + n_lat // nch) if c == 0 else (n_ctx + c * (n_lat // nch), n_lat // nch)
                   for c in range(nch))
    mrows = ATT_GROUP * tq
    q_blk0 = (3 * RET_W) // (ATT_GROUP * ATT_HD)
    k_blk0 = (3 * RET_W + ATT_W) // ATT_HD
    v_blk0 = k_blk0 + ATT_KV_HEADS
    kern = functools.partial(_att_kernel, chunks=chunks, n_ctx=n_ctx, t=t)
    return pl.pallas_call(
        kern,
        out_shape=jax.ShapeDtypeStruct((b, t, ATT_W), BF16),
        grid=(b, ATT_KV_HEADS, t // tq),
        in_specs=[pl.BlockSpec((None, tq, ATT_GROUP * ATT_HD), lambda bi, hk, i: (bi, i, q_blk0 + hk)),
                  pl.BlockSpec((None, t, ATT_HD), lambda bi, hk, i: (bi, 0, k_blk0 + hk)),
                  pl.BlockSpec((None, t, ATT_HD), lambda bi, hk, i: (bi, 0, v_blk0 + hk))],
        out_specs=pl.BlockSpec((None, tq, ATT_GROUP * ATT_HD), lambda bi, hk, i: (bi, i, hk)),
        scratch_shapes=[pltpu.VMEM((t, 2 * ATT_HD), BF16),
                        pltpu.VMEM((mrows, ATT_HD), BF16),
                        pltpu.VMEM((2, -(-chunks[0][1] // ATT_SUB), mrows, ATT_SUB), F32),
                        pltpu.VMEM((mrows, LANES), F32),
                        pltpu.VMEM((mrows, LANES), F32),
                        pltpu.VMEM((mrows, 2 * ATT_HD), F32)],
        compiler_params=_params("parallel", "parallel", "arbitrary"),
        name="gqa_attention",
    )(mix, mix, mix)


def _s5_kernel(u_ref, win_ref, tsum_ref, wout_ref, lam_ref, y_ref, uc_ref, d_ref, xin_ref, yw_ref, *, nc, ncc):
    p2 = 2 * S5_STATE
    sub = 8
    ln = S5_CHUNK
    gpb = LANES // S5_GROUP
    us = [pltpu.bitcast(u_ref[pl.ds(s, nc, stride=ln), :].astype(BF16), jnp.uint32) for s in range(ln)]
    for g in range(gpb):
        uc = pltpu.bitcast(
            jnp.concatenate([us[s][:, g * S5_GROUP:(g + 1) * S5_GROUP] for s in range(ln)], axis=1), BF16)
        uc_ref[g] = uc
        d_ref[g] = jnp.dot(uc, win_ref[g], preferred_element_type=F32)
    rowid = lax.broadcasted_iota(jnp.int32, (sub, p2), 0)

    def tile(g, reverse, rows, c, cs):
        base = 32 if reverse else 0
        col = 2 * p2 if reverse else 0

        def mult(i):
            return tuple(jnp.broadcast_to(lam_ref[g, base + 3 * i + r:base + 3 * i + r + 1, :], (sub, p2))
                         for r in range(3))

        def shifted(v, n):
            if reverse:
                return jnp.where(rowid < sub - n, pltpu.roll(v, sub - n, 0), 0.0)
            return jnp.where(rowid >= n, pltpu.roll(v, n, 0), 0.0)

        x = d_ref[g, rows, col:col + p2]
        xs = d_ref[g, rows, col + p2:col + 2 * p2]
        for i, n in enumerate((1, 2, 4)):
            a, b, bs = mult(i)
            xr, xsr = shifted(x, n), shifted(xs, n)
            x, xs = x + a * xr + b * xsr, xs + a * xsr + bs * xr
        pa = lam_ref[g, base + 16:base + 24, :]
        pb = lam_ref[g, base + 24:base + 32, :]
        xin_ref[g, rows, col // 2:col // 2 + p2] = shifted(x, 1) + pa * c + pb * cs
        edge = 0 if reverse else sub - 1
        last = jnp.broadcast_to(x[edge:edge + 1, :], (sub, p2))
        lasts = jnp.broadcast_to(xs[edge:edge + 1, :], (sub, p2))
        a, b, bs = mult(3)
        return a * c + b * cs + last, a * cs + bs * c + lasts

    nt = nc // sub
    nct = ncc // sub

    def step(mf, mb, carry):
        rf = pl.ds(pl.multiple_of(mf * sub, sub), sub)
        rb = pl.ds(pl.multiple_of(mb * sub, sub), sub)
        return tuple(tile(g, False, rf, carry[g][0], carry[g][1]) + tile(g, True, rb, carry[g][2], carry[g][3])
                     for g in range(gpb))

    zero = jnp.zeros((sub, p2), F32)
    init = tuple((zero, zero, zero, zero) for _ in range(gpb))
    carry = lax.fori_loop(0, nct, lambda t, cr: step(t, nct - 1 - t, cr), init)
    lax.fori_loop(nct, nt, lambda t, cr: step(t, nt - 1 - (t - nct), cr), carry)

    for g in range(gpb):
        y = jnp.dot(uc_ref[g], tsum_ref[g], preferred_element_type=F32)
        y += jnp.dot(xin_ref[g].astype(BF16), wout_ref[g], preferred_element_type=F32)
        yw_ref[g] = pltpu.bitcast(y.astype(BF16), jnp.uint32)
    for s in range(ln):
        piece = jnp.concatenate([yw_ref[g, :, s * S5_GROUP:(s + 1) * S5_GROUP] for g in range(gpb)], axis=1)
        y_ref[pl.ds(s, nc, stride=ln), :] = pltpu.bitcast(piece, BF16).astype(F32)


def _s5_weights(a_re, a_im, log_dt, b_re, b_im, c_re, c_im, d_skip):
    hi = lax.Precision.HIGHEST
    ln = S5_CHUNK
    dt = jnp.exp(log_dt)[..., None]
    mag = jnp.exp(a_re * dt)
    abr, abi = mag * jnp.cos(a_im * dt), mag * jnp.sin(a_im * dt)
    den = a_re * a_re + a_im * a_im
    fr = ((abr - 1.0) * a_re + abi * a_im) / den
    fi = (abi * a_re - (abr - 1.0) * a_im) / den
    bbr = fr[..., None] * b_re - fi[..., None] * b_im
    bbi = fr[..., None] * b_im + fi[..., None] * b_re
    n = jnp.arange(ln + 1, dtype=F32)[:, None, None, None]
    pmag = jnp.exp(a_re * dt * n)
    pr, pi = pmag * jnp.cos(a_im * dt * n), pmag * jnp.sin(a_im * dt * n)
    cr = c_re[None] * pr[:, :, :, None, :] - c_im[None] * pi[:, :, :, None, :]
    ci = c_re[None] * pi[:, :, :, None, :] + c_im[None] * pr[:, :, :, None, :]
    g_ = a_re.shape[1]
    lhs = jnp.concatenate([cr[:ln], -ci[:ln]], axis=-1).transpose(1, 2, 0, 3, 4)
    lhs = lhs.reshape(2, g_, ln * S5_GROUP, 2 * S5_STATE)
    rhs = jnp.concatenate([bbr, bbi], axis=2)
    m = jnp.einsum('dgmk,dgke->dgme', lhs, rhs, precision=hi)
    m = m.reshape(2, g_, ln, S5_GROUP, S5_GROUP).transpose(2, 0, 1, 3, 4)
    lag = jnp.arange(ln)[None, :] - jnp.arange(ln)[:, None]
    n_idx = jnp.arange(ln)[:, None, None]
    oh_f = (lag[None] == n_idx).astype(F32)
    oh_b = (-lag[None] == n_idx).astype(F32)
    tsum = (jnp.einsum('nsi,ngce->gseic', oh_f, m[:, 0], precision=hi)
            + jnp.einsum('nsi,ngce->gseic', oh_b, m[:, 1], precision=hi))
    g = a_re.shape[1]
    eye_s = jnp.eye(ln, dtype=F32)[None, :, None, :, None]
    eye_c = jnp.eye(S5_GROUP, dtype=F32)[None, None, :, None, :]
    tsum = tsum + eye_s * eye_c * d_skip.reshape(g, 1, S5_GROUP, 1, 1)
    tsum = tsum.reshape(g, ln * S5_GROUP, ln * S5_GROUP)

    def w_in(pw_r, pw_i, d):
        re = pw_r[..., None] * bbr[d][None] - pw_i[..., None] * bbi[d][None]
        im = pw_r[..., None] * bbi[d][None] + pw_i[..., None] * bbr[d][None]
        re = re.transpose(1, 0, 3, 2).reshape(g, ln * S5_GROUP, S5_STATE)
        im = im.transpose(1, 0, 3, 2).reshape(g, ln * S5_GROUP, S5_STATE)
        return jnp.concatenate([re, im, im, re], axis=-1)

    win = jnp.concatenate([w_in(pr[:ln, 0][::-1], pi[:ln, 0][::-1], 0),
                           w_in(pr[:ln, 1], pi[:ln, 1], 1)], axis=-1)

    def w_out(cr_d, ci_d):
        re = cr_d.transpose(1, 3, 0, 2).reshape(g, S5_STATE, ln * S5_GROUP)
        im = (-ci_d).transpose(1, 3, 0, 2).reshape(g, S5_STATE, ln * S5_GROUP)
        return jnp.concatenate([re, im], axis=1)

    wout = jnp.concatenate([w_out(cr[1:ln + 1, 0], ci[1:ln + 1, 0]),
                            w_out(cr[1:ln + 1, 1][::-1], ci[1:ln + 1, 1][::-1])], axis=1)

    def cpow(k):
        e = (jnp.asarray(k, F32) * ln)[..., None, None, None]
        mg = jnp.exp(a_re * dt * e)
        return mg * jnp.cos(a_im * dt * e), mg * jnp.sin(a_im * dt * e)

    def dup(a, b):
        return jnp.concatenate([a, b], axis=-1)

    mr, mi = cpow(jnp.array([1, 2, 4, 8]))
    mult = jnp.stack([dup(mr, mr), dup(-mi, mi), dup(mi, -mi)], axis=1).reshape(12, 2, g, 2 * S5_STATE)
    qr, qi = cpow(jnp.arange(8))
    qr = jnp.stack([qr[:, 0], qr[::-1, 1]], axis=1)
    qi = jnp.stack([qi[:, 0], qi[::-1, 1]], axis=1)
    lam = jnp.concatenate([mult, jnp.zeros((4, 2, g, 2 * S5_STATE), F32), dup(qr, qr), dup(-qi, qi)], axis=0)
    lam = lam.transpose(2, 1, 0, 3).reshape(g, 64, 2 * S5_STATE)
    return win.astype(BF16), tsum.astype(BF16), wout.astype(BF16), lam


def _s5_in_kernel(u_ref, w_ref, o_ref):
    o_ref[...] = jnp.dot(u_ref[...], w_ref[...], preferred_element_type=F32)


def _s5_in_proj(u, w_s5):
    b, t, d = u.shape
    tm = _big_row_tile(t)
    return pl.pallas_call(
        _s5_in_kernel,
        out_shape=jax.ShapeDtypeStruct((b, t, S5_W), F32),
        grid=(b, t // tm),
        in_specs=[pl.BlockSpec((None, tm, d), lambda bi, i: (bi, i, 0)),
                  pl.BlockSpec((d, S5_W), lambda bi, i: (0, 0))],
        out_specs=pl.BlockSpec((None, tm, S5_W), lambda bi, i: (bi, i, 0)),
        compiler_params=_params("parallel", "arbitrary"),
        name="s5_in_proj",
    )(u, w_s5)


def _s5(u5, s5w, layer, n_ctx):
    b, t, _ = u5.shape
    win, tsum, wout, lam = s5w
    gpb = LANES // S5_GROUP
    nc = t // S5_CHUNK
    kw = S5_CHUNK * S5_GROUP
    assert (n_ctx // S5_CHUNK) % 8 == 0 and nc % 8 == 0
    kern = functools.partial(_s5_kernel, nc=nc, ncc=n_ctx // S5_CHUNK)
    wspec = lambda r, c: pl.BlockSpec((None, gpb, r, c), lambda bi, gb: (layer, gb, 0, 0))
    return pl.pallas_call(
        kern,
        out_shape=jax.ShapeDtypeStruct((b, t, S5_W), F32),
        grid=(b, S5_W // LANES),
        in_specs=[pl.BlockSpec((None, t, LANES), lambda bi, gb: (bi, 0, gb)),
                  wspec(kw, 8 * S5_STATE), wspec(kw, kw), wspec(4 * S5_STATE, kw), wspec(64, 2 * S5_STATE)],
        out_specs=pl.BlockSpec((None, t, LANES), lambda bi, gb: (bi, 0, gb)),
        scratch_shapes=[pltpu.VMEM((gpb, nc, kw), BF16),
                        pltpu.VMEM((gpb, nc, 8 * S5_STATE), F32),
                        pltpu.VMEM((gpb, nc, 4 * S5_STATE), F32),
                        pltpu.VMEM((gpb, nc // 2, kw), jnp.uint32)],
        compiler_params=_params("parallel", "parallel"),
        name="s5_scan",
    )(u5, win, tsum, wout, lam)


def _glu_kernel(s_ref, u_ref, wg_ref, w_ref, b_ref, o_ref):
    s = jax.nn.gelu(s_ref[...])
    z = jnp.dot(s.astype(BF16), w_ref[...], preferred_element_type=F32) + b_ref[...]
    g = jnp.dot(u_ref[...], wg_ref[...], preferred_element_type=F32)
    o_ref[...] = (s * jax.nn.sigmoid(z) * (g * jax.nn.sigmoid(g))).astype(BF16)


def _s5_glu_gate(s_pre, u, w_s5, glu_w, glu_b):
    b, t, d = u.shape
    tm = _big_row_tile(t)
    return pl.pallas_call(
        _glu_kernel,
        out_shape=jax.ShapeDtypeStruct((b, t, S5_W), BF16),
        grid=(b, t // tm),
        in_specs=[pl.BlockSpec((None, tm, S5_W), lambda bi, i: (bi, i, 0)),
                  pl.BlockSpec((None, tm, d), lambda bi, i: (bi, i, 0)),
                  pl.BlockSpec((d, S5_W), lambda bi, i: (0, 1)),
                  pl.BlockSpec((S5_W, S5_W), lambda bi, i: (0, 0)),
                  pl.BlockSpec((1, S5_W), lambda bi, i: (0, 0))],
        out_specs=pl.BlockSpec((None, tm, S5_W), lambda bi, i: (bi, i, 0)),
        compiler_params=_params("parallel", "arbitrary"),
        name="s5_glu_gate",
    )(s_pre, u, w_s5, glu_w, glu_b)


def _gate_kernel(u_ref, wr_ref, wa0_ref, wa1_ref, r_ref, a_ref, gn_ref, o_ref):
    u = u_ref[...]

    def gated(w, act, cols):
        g = jnp.dot(u, w, preferred_element_type=F32)
        o_ref[:, cols] = (act * (g * jax.nn.sigmoid(g))).astype(BF16)

    for tile in range(RET_W // COL_TILE):
        cols = slice(tile * COL_TILE, (tile + 1) * COL_TILE)
        gated(wr_ref[:, cols], r_ref[:, cols].astype(F32) * gn_ref[:, cols], cols)
    for tile, w_ref in enumerate((wa0_ref, wa1_ref)):
        cols = slice(tile * COL_TILE, (tile + 1) * COL_TILE)
        gated(w_ref[...], a_ref[:, cols].astype(F32), slice(RET_W + cols.start, RET_W + cols.stop))


def _branch_gates(u, wb, layer, r, a, gn_w):
    b, t, d = u.shape
    tm = 528 if t % 528 == 0 else ROW_TILE
    assert ATT_W == 2 * COL_TILE
    att_g0 = (4 * RET_W + ATT_W + 2 * ATT_KV_W) // COL_TILE
    resident = lambda width, blk: pl.BlockSpec((None, d, width), lambda bi, i: (layer, 0, blk),
                                               pipeline_mode=pl.Buffered(1))
    return pl.pallas_call(
        _gate_kernel,
        out_shape=jax.ShapeDtypeStruct((b, t, RET_W + ATT_W), BF16),
        grid=(b, t // tm),
        in_specs=[pl.BlockSpec((None, tm, d), lambda bi, i: (bi, i, 0)),
                  resident(RET_W, 3), resident(COL_TILE, att_g0), resident(COL_TILE, att_g0 + 1),
                  pl.BlockSpec((None, tm, RET_W), lambda bi, i: (bi, i, 0)),
                  pl.BlockSpec((None, tm, ATT_W), lambda bi, i: (bi, i, 0)),
                  pl.BlockSpec((1, RET_W), lambda bi, i: (0, 0))],
        out_specs=pl.BlockSpec((None, tm, RET_W + ATT_W), lambda bi, i: (bi, i, 0)),
        compiler_params=_params("parallel", "arbitrary"),
        name="branch_gates",
    )(u, wb, wb, wb, r, a, gn_w)


def _merge_kernel(u_ref, zra_ref, zs_ref, wm0_ref, wm1_ref, wm2_ref, wr_ref, wa_ref, ws_ref, o_ref):
    tm = u_ref.shape[0]
    nsplit = 2 if tm % 32 == 0 else 1
    hr = tm // nsplit
    for r in range(nsplit):
        rows = slice(r * hr, (r + 1) * hr)
        u = u_ref[rows, :]
        acts = (zra_ref[rows, :RET_W], zra_ref[rows, RET_W:], zs_ref[rows, :])
        acc = None
        for act, wm_ref, wbr_ref in zip(acts, (wm0_ref, wm1_ref, wm2_ref), (wr_ref, wa_ref, ws_ref)):
            gate = jax.nn.sigmoid(jnp.dot(u, wm_ref[...], preferred_element_type=F32))
            proj = jnp.dot(act, wbr_ref[...], preferred_element_type=F32)
            acc = gate * proj if acc is None else acc + gate * proj
        o_ref[rows, :] = acc.astype(BF16)


def _merge(u, z_ra, z_s, wb, layer, w_ret, w_att, w_s5):
    b, t, d = u.shape
    tm = _big_row_tile(t)
    m0 = (wb.shape[2] - N_BRANCH * d) // COL_TILE
    per = d // COL_TILE
    wm = lambda br: pl.BlockSpec((None, d, COL_TILE), lambda bi, i, j: (layer, 0, m0 + br * per + j))
    wbr = lambda k: pl.BlockSpec((k, COL_TILE), lambda bi, i, j: (0, j))
    return pl.pallas_call(
        _merge_kernel,
        out_shape=jax.ShapeDtypeStruct((b, t, d), BF16),
        grid=(b, t // tm, d // COL_TILE),
        in_specs=[pl.BlockSpec((None, tm, d), lambda bi, i, j: (bi, i, 0)),
                  pl.BlockSpec((None, tm, RET_W + ATT_W), lambda bi, i, j: (bi, i, 0)),
                  pl.BlockSpec((None, tm, S5_W), lambda bi, i, j: (bi, i, 0)),
                  wm(0), wm(1), wm(2), wbr(RET_W), wbr(ATT_W), wbr(S5_W)],
        out_specs=pl.BlockSpec((None, tm, COL_TILE), lambda bi, i, j: (bi, i, j)),
        compiler_params=_params("parallel", "parallel", "arbitrary"),
        name="gated_merge",
    )(u, z_ra, z_s, wb, wb, wb, w_ret, w_att, w_s5)


def _out_kernel(*refs, alpha, first_tile, emit_u):
    if emit_u:
        m_ref, w_ref, c_ref, x_ref, mod_ref, lnw_ref, lnb_ref, nmod_ref, o_ref, u_ref, v_ref = refs
    else:
        m_ref, w_ref, c_ref, x_ref, mod_ref, lnw_ref, lnb_ref, o_ref, v_ref = refs
    i = pl.program_id(1)
    d = w_ref.shape[1]

    def run(res_ref):
        m = m_ref[...]
        s1 = jnp.zeros((m.shape[0], 1), F32)
        for c in range(d // COL_TILE):
            cols = slice(c * COL_TILE, (c + 1) * COL_TILE)
            y = jnp.dot(m, w_ref[:, cols], preferred_element_type=F32)
            v = alpha * res_ref[:, cols] + mod_ref[2:3, cols] * y
            v_ref[:, cols] = v
            s1 += jnp.sum(v, axis=-1, keepdims=True)
        mu = s1 * (1.0 / d)
        vc = v_ref[...] - mu
        var = jnp.mean(vc * vc, axis=-1, keepdims=True)
        out = vc * lax.rsqrt(var + LN_EPS) * lnw_ref[...] + lnb_ref[...]
        o_ref[...] = out
        if emit_u:
            u_ref[...] = (out * (1.0 + nmod_ref[1:2, :]) + nmod_ref[0:1, :]).astype(BF16)

    if first_tile == 0:
        @pl.when(i == 0)
        def _():
            run(c_ref)

        @pl.when(i > 0)
        def _():
            run(x_ref)
    else:
        run(x_ref)


def _out_proj(m, w_out, ctx_src, lat_src, modsel, ln_w, ln_b, alpha, skip_ctx, next_modsel=None):
    b, t, d = m.shape
    off = 1 if skip_ctx else 0
    nt = t // ROW_TILE - off
    emit_u = next_modsel is not None
    kern = functools.partial(_out_kernel, alpha=alpha, first_tile=off, emit_u=emit_u)
    vec = lambda: pl.BlockSpec((1, d), lambda bi, i: (0, 0))
    mod = lambda: pl.BlockSpec((None, None, 3, d), lambda bi, i: (bi, jnp.minimum(i + off, 1), 0, 0))
    row_out = lambda: pl.BlockSpec((None, ROW_TILE, d), lambda bi, i: (bi, i, 0))
    x_shape = jax.ShapeDtypeStruct((b, nt * ROW_TILE, d), F32)
    return pl.pallas_call(
        kern,
        out_shape=(x_shape, jax.ShapeDtypeStruct(x_shape.shape, BF16)) if emit_u else x_shape,
        grid=(b, nt),
        in_specs=[pl.BlockSpec((None, ROW_TILE, d), lambda bi, i: (bi, i + off, 0)),
                  pl.BlockSpec((d, d), lambda bi, i: (0, 0)),
                  *_token_specs(ctx_src, lat_src, off),
                  mod(), vec(), vec(), *([mod()] if emit_u else [])],
        out_specs=(row_out(), row_out()) if emit_u else row_out(),
        scratch_shapes=[pltpu.VMEM((ROW_TILE, d), F32)],
        compiler_params=_params("parallel", "arbitrary"),
        name="out_proj_norm",
    )(m, w_out, ctx_src, lat_src, modsel, ln_w.reshape(1, d), ln_b.reshape(1, d),
      *([next_modsel] if emit_u else []))


def _rope_tables(n, n_ctx):
    rows = n // GRID_W
    row = jnp.repeat(jnp.arange(rows, dtype=F32), GRID_W)
    col = jnp.tile(jnp.arange(GRID_W, dtype=F32), rows)

    def table(head_dim):
        per_axis = head_dim // 4
        inv = ROPE_THETA ** (-jnp.arange(per_axis, dtype=F32) / per_axis)
        ang = jnp.concatenate([row[:, None] * inv, col[:, None] * inv], axis=-1)
        cos = jnp.concatenate([jnp.ones((n_ctx, head_dim // 2), F32), jnp.cos(ang)], axis=0)
        sin = jnp.concatenate([jnp.zeros((n_ctx, head_dim // 2), F32), jnp.sin(ang)], axis=0)
        return cos, sin

    cos_r, sin_r = table(RET_DK)
    cos_a, sin_a = table(ATT_HD)
    return (cos_r, sin_r, jnp.concatenate([cos_a, cos_a], -1), jnp.concatenate([-sin_a, sin_a], -1))


def kernel(x, c, ctx, c_ctx, ada_w, ada_b, w_in, ret_log_decay, ret_gn_w, att_q_norm, att_k_norm, s5_a_re, s5_a_im, s5_log_dt, s5_b_re, s5_b_im, s5_c_re, s5_c_im, s5_d, s5_glu_w, s5_glu_b, w_br_ret, w_br_att, w_br_s5, w_out, ln_w, ln_b):
    b, n, d = x.shape
    n_ctx = ctx.shape[1]
    depth = w_in.shape[0]
    assert n_ctx == ROW_TILE and n % ROW_TILE == 0 and n % GRID_W == 0
    assert w_in.shape[2] == 4 * RET_W + 2 * ATT_W + 2 * ATT_KV_W + 2 * S5_W + N_BRANCH * d
    alpha = (2.0 * depth) ** 0.25
    s5_col0 = 4 * RET_W + 2 * ATT_W + 2 * ATT_KV_W

    t = n_ctx + n
    ctx_src, lat_src = ctx, x
    cos_r, sin_r, cos_a, sin_a = _rope_tables(n, n_ctx)
    crows = 16
    cvec = jnp.concatenate([c, c_ctx[None, :], jnp.zeros((crows - b - 1, d), F32)], axis=0)
    wb = w_in.astype(BF16)
    s5w = jax.vmap(_s5_weights)(s5_a_re, s5_a_im, s5_log_dt, s5_b_re, s5_b_im, s5_c_re, s5_c_im, s5_d)

    modsels = []
    for l in range(depth):
        mod = _modulation(cvec, ada_w, ada_b, l).reshape(crows, 3, d)
        modsels.append(jnp.stack([jnp.broadcast_to(mod[b][None], (b, 3, d)), mod[:b]], axis=1))

    u = _modulate(ctx_src, lat_src, t, modsels[0])
    for l in range(depth):
        last = l == depth - 1
        modsel = modsels[l]
        w_s5 = wb[l, :, s5_col0:s5_col0 + 2 * S5_W]
        mix = _mix_proj(u, wb, l, cos_r, sin_r, cos_a, sin_a,
                        att_q_norm[l].reshape(1, ATT_HD), att_k_norm[l].reshape(1, ATT_HD))
        r = _retention(mix, ret_log_decay[l], n_ctx)
        a = _attention(mix, n_ctx)
        s_pre = _s5(_s5_in_proj(u, w_s5), s5w, l, n_ctx)
        z_s = _s5_glu_gate(s_pre, u, w_s5, s5_glu_w[l].astype(BF16), s5_glu_b[l].reshape(1, S5_W))
        z_ra = _branch_gates(u, wb, l, r, a, ret_gn_w[l].reshape(1, RET_W))
        m = _merge(u, z_ra, z_s, wb, l, w_br_ret[l].astype(BF16), w_br_att[l].astype(BF16),
                   w_br_s5[l].astype(BF16))
        res = _out_proj(m, w_out[l].astype(BF16), ctx_src, lat_src, modsel, ln_w[l], ln_b[l], alpha, last,
                        None if last else modsels[l + 1])
        if last:
            return res
        xa, u = res
        ctx_src = lat_src = xa
```

```python
import functools
import math

import jax
import jax.numpy as jnp
from jax import lax
from jax.experimental import pallas as pl
from jax.experimental.pallas import tpu as pltpu

F32 = jnp.float32
BF16 = jnp.bfloat16

GRID_W = 64
RET_HEADS = 4
RET_DK = 256
RET_DV = 256
RET_W = RET_HEADS * RET_DV
ATT_HEADS = 8
ATT_KV_HEADS = 2
ATT_GROUP = ATT_HEADS // ATT_KV_HEADS
ATT_HD = 128
ATT_W = ATT_HEADS * ATT_HD
ATT_KV_W = ATT_KV_HEADS * ATT_HD
ROPE_THETA = 10000.0
S5_GROUP = 16
S5_W = 768
S5_GROUPS = S5_W // S5_GROUP
S5_STATE = 64
N_BRANCH = 3
LN_EPS = 1e-6
RMS_EPS = 1e-6

LANES = 128
VMEM_LIMIT_BYTES = 56 * 1024 * 1024

CHUNK = 128
S5_CHUNK = 16
ROW_TILE = 256
COL_TILE = 512
MIX_W = 4608
ATT_SUB = 2048
LOG2E = math.log2(math.e)


def _params(*sem):
    return pltpu.CompilerParams(dimension_semantics=sem, vmem_limit_bytes=VMEM_LIMIT_BYTES)


def _big_row_tile(t):
    for cand in (1056, 1024, 768, 640, 512, 256):
        if t % cand == 0:
            return cand
    raise ValueError(f"unsupported token count {t}")


def _resident_row_tile(t):
    return 528 if t % 528 == 0 else ROW_TILE


def _mod_kernel(c_ref, w_ref, b_ref, o_ref):
    c = c_ref[...]
    s = c * jax.nn.sigmoid(c)
    s_hi = s.astype(BF16)
    s_lo = (s - s_hi.astype(F32)).astype(BF16)
    w = w_ref[...]
    w_hi = w.astype(BF16)
    w_lo = (w - w_hi.astype(F32)).astype(BF16)
    acc = jnp.dot(s_hi, w_hi, preferred_element_type=F32)
    acc += jnp.dot(s_lo, w_hi, preferred_element_type=F32)
    acc += jnp.dot(s_hi, w_lo, preferred_element_type=F32)
    o_ref[...] = acc + b_ref[...]


def _modulation(cvec, ada_w, ada_b, layer):
    rows, d = cvec.shape
    n = ada_w.shape[2]
    tn = 768
    return pl.pallas_call(
        _mod_kernel,
        out_shape=jax.ShapeDtypeStruct((rows, n), F32),
        grid=(n // tn,),
        in_specs=[pl.BlockSpec((rows, d), lambda j: (0, 0)),
                  pl.BlockSpec((None, d, tn), lambda j: (layer, 0, j)),
                  pl.BlockSpec((None, 1, tn), lambda j: (layer, 0, j))],
        out_specs=pl.BlockSpec((rows, tn), lambda j: (0, j)),
        compiler_params=_params("arbitrary"),
        name="adaln_modulation",
    )(cvec, ada_w, ada_b.reshape(ada_b.shape[0], 1, n))


def _modulate_kernel(c_ref, x_ref, mod_ref, o_ref):
    i = pl.program_id(1)
    m = mod_ref[...]

    @pl.when(i == 0)
    def _():
        o_ref[...] = (c_ref[...] * (1.0 + m[1:2]) + m[0:1]).astype(BF16)

    @pl.when(i > 0)
    def _():
        o_ref[...] = (x_ref[...] * (1.0 + m[1:2]) + m[0:1]).astype(BF16)


def _token_specs(ctx_src, lat_src, first_tile=0):
    d = ctx_src.shape[2]
    shift = first_tile - (0 if lat_src is ctx_src else 1)
    return (pl.BlockSpec((None, ROW_TILE, d), lambda bi, i: (bi, 0, 0)),
            pl.BlockSpec((None, ROW_TILE, d), lambda bi, i: (bi, jnp.maximum(i + shift, 0), 0)))


def _modulate(ctx_src, lat_src, t, modsel):
    b, _, d = ctx_src.shape
    return pl.pallas_call(
        _modulate_kernel,
        out_shape=jax.ShapeDtypeStruct((b, t, d), BF16),
        grid=(b, t // ROW_TILE),
        in_specs=[*_token_specs(ctx_src, lat_src),
                  pl.BlockSpec((None, None, 3, d), lambda bi, i: (bi, jnp.minimum(i, 1), 0, 0))],
        out_specs=pl.BlockSpec((None, ROW_TILE, d), lambda bi, i: (bi, i, 0)),
        compiler_params=_params("parallel", "arbitrary"),
        name="modulate",
    )(ctx_src, lat_src, modsel)


def _mix_kernel(u_ref, wa_ref, wb_ref, cr_ref, sr_ref, ca_ref, sa_ref, qn_ref, kn_ref, o_ref):
    u = u_ref[...]

    def proj(w_ref, tile):
        return jnp.dot(u, w_ref[:, tile * COL_TILE:(tile + 1) * COL_TILE], preferred_element_type=F32)

    def rms_rope(x, w, scale):
        ss = jnp.dot((x * x).astype(BF16), jnp.ones((ATT_HD, ATT_HD), BF16), preferred_element_type=F32)
        y = x * lax.rsqrt(ss * (1.0 / ATT_HD) + RMS_EPS) * w
        y = y * ca_ref[...] + pltpu.roll(y, ATT_HD // 2, 1) * sa_ref[...]
        return (y * scale).astype(BF16)

    cos = cr_ref[...]
    sin = sr_ref[...]
    for tile in range(4):
        acc = proj(wa_ref, tile)
        scale = RET_DK ** -0.5 if tile >= 2 else 1.0
        for h in range(COL_TILE // RET_DK):
            c0 = tile * COL_TILE + h * RET_DK
            x1 = acc[:, h * RET_DK: h * RET_DK + LANES]
            x2 = acc[:, h * RET_DK + LANES: (h + 1) * RET_DK]
            o_ref[:, c0:c0 + LANES] = ((x1 * cos - x2 * sin) * scale).astype(BF16)
            o_ref[:, c0 + LANES:c0 + RET_DK] = ((x2 * cos + x1 * sin) * scale).astype(BF16)
    for tile in range(4, 6):
        o_ref[:, tile * COL_TILE:(tile + 1) * COL_TILE] = proj(wa_ref, tile).astype(BF16)
    for tile in range(2):
        acc = proj(wb_ref, tile)
        for h in range(COL_TILE // ATT_HD):
            c0 = (6 + tile) * COL_TILE + h * ATT_HD
            o_ref[:, c0:c0 + ATT_HD] = rms_rope(acc[:, h * ATT_HD:(h + 1) * ATT_HD], qn_ref[...],
                                                ATT_HD ** -0.5 * LOG2E)
    acc = proj(wb_ref, 2)
    c0 = 8 * COL_TILE
    for h in range(ATT_KV_HEADS):
        o_ref[:, c0 + h * ATT_HD:c0 + (h + 1) * ATT_HD] = rms_rope(acc[:, h * ATT_HD:(h + 1) * ATT_HD],
                                                                   kn_ref[...], 1.0)
    o_ref[:, c0 + ATT_KV_W:] = acc[:, ATT_KV_W:].astype(BF16)


def _mix_proj(u, wb, layer, cos_r, sin_r, cos_a, sin_a, qn, kn):
    b, t, d = u.shape
    tm = _resident_row_tile(t)
    tab = lambda: pl.BlockSpec((tm, LANES), lambda bi, i: (i, 0))
    vec = lambda: pl.BlockSpec((1, LANES), lambda bi, i: (0, 0))
    ret_w = 3 * RET_W
    att_w = 2 * ATT_W
    assert (4 * RET_W) % att_w == 0
    resident = lambda width, blk: pl.BlockSpec((None, d, width), lambda bi, i: (layer, 0, blk),
                                               pipeline_mode=pl.Buffered(1))
    return pl.pallas_call(
        _mix_kernel,
        out_shape=jax.ShapeDtypeStruct((b, t, MIX_W), BF16),
        grid=(b, t // tm),
        in_specs=[pl.BlockSpec((None, tm, d), lambda bi, i: (bi, i, 0)),
                  resident(ret_w, 0), resident(att_w, 4 * RET_W // att_w),
                  tab(), tab(), tab(), tab(), vec(), vec()],
        out_specs=pl.BlockSpec((None, tm, MIX_W), lambda bi, i: (bi, i, 0)),
        compiler_params=_params("parallel", "arbitrary"),
        name="mixer_in_proj",
    )(u, wb, wb, cos_r, sin_r, cos_a, sin_a, qn, kn)


def _ret_kernel(ld_ref, q_ref, k_ref, v_ref, o_ref, acc_ref, sf_ref, sb_ref, *, n_ctx_chunks, n_chunks):
    h = pl.program_id(1)
    lgf = ld_ref[0, h]
    lgb = ld_ref[1, h]
    c = CHUNK
    ri = lax.broadcasted_iota(jnp.int32, (c, c), 0).astype(F32)
    ci = lax.broadcasted_iota(jnp.int32, (c, c), 1).astype(F32)
    rel = ri - ci
    mask = jnp.where(rel >= 0, jnp.exp(lgf * jnp.maximum(rel, 0.0)), jnp.exp(lgb * jnp.maximum(-rel, 0.0)))
    row = lax.broadcasted_iota(jnp.int32, (c, RET_DV), 0).astype(F32)
    qdec_f = jnp.exp(lgf * (row + 1.0))
    qdec_b = jnp.exp(lgb * (c - row))
    kdec_f = jnp.exp(lgf * (c - 1.0 - row))
    kdec_b = jnp.exp(lgb * row)
    cdec_f = jnp.exp(jnp.full((1, RET_DV), lgf * c, F32))
    cdec_b = jnp.exp(jnp.full((1, RET_DV), lgb * c, F32))
    contract0 = (((0,), (0,)), ((), ()))
    contract1 = (((1,), (1,)), ((), ()))

    def load(ci_):
        r0 = pl.multiple_of(ci_ * c, c)
        rows = pl.ds(r0, c)
        return rows, q_ref[rows, :], k_ref[rows, :], v_ref[rows, :]

    def bwd(ci_):
        rows, q, k, v = load(ci_)
        s = sb_ref[...]
        acc_ref[rows, :] += jnp.dot(q, s.astype(BF16), preferred_element_type=F32) * qdec_b
        kd = (k.astype(F32) * kdec_b).astype(BF16)
        sb_ref[...] = s * cdec_b + lax.dot_general(kd, v, contract0, preferred_element_type=F32)

    def fwd(ci_):
        rows, q, k, v = load(ci_)
        s = sf_ref[...]
        sc = lax.dot_general(q, k, contract1, preferred_element_type=F32) * mask
        o = jnp.dot(sc.astype(BF16), v, preferred_element_type=F32)
        o += jnp.dot(q, s.astype(BF16), preferred_element_type=F32) * qdec_f
        acc_ref[rows, :] += o
        kd = (k.astype(F32) * kdec_f).astype(BF16)
        sf_ref[...] = s * cdec_f + lax.dot_general(kd, v, contract0, preferred_element_type=F32)

    sf_ref[...] = jnp.zeros_like(sf_ref)
    sb_ref[...] = jnp.zeros_like(sb_ref)
    acc_ref[...] = jnp.zeros_like(acc_ref)

    def ctx_part(t, carry):
        fwd(t)
        bwd(n_ctx_chunks - 1 - t)
        return carry

    def lat_part(t, carry):
        fwd(t)
        bwd(n_chunks - 1 - (t - n_ctx_chunks))
        return carry

    lax.fori_loop(0, n_ctx_chunks, ctx_part, 0, unroll=2)
    lax.fori_loop(n_ctx_chunks, n_chunks, lat_part, 0, unroll=4)

    def emit(t, carry):
        rows = pl.ds(pl.multiple_of(t * c, c), c)
        o_ref[rows, :] = acc_ref[rows, :].astype(BF16)
        return carry

    lax.fori_loop(0, n_chunks, emit, 0, unroll=6)


def _retention(mix, log_decay, n_ctx):
    b, t, _ = mix.shape
    kern = functools.partial(_ret_kernel, n_ctx_chunks=n_ctx // CHUNK, n_chunks=t // CHUNK)
    blk = lambda off: pl.BlockSpec((None, t, RET_DK), lambda bi, h: (bi, 0, off + h))
    return pl.pallas_call(
        kern,
        out_shape=jax.ShapeDtypeStruct((b, t, RET_W), BF16),
        grid=(b, RET_HEADS),
        in_specs=[pl.BlockSpec(memory_space=pltpu.SMEM), blk(0), blk(RET_HEADS), blk(2 * RET_HEADS)],
        out_specs=pl.BlockSpec((None, t, RET_DV), lambda bi, h: (bi, 0, h)),
        scratch_shapes=[pltpu.VMEM((t, RET_DV), F32), pltpu.VMEM((RET_DK, RET_DV), F32),
                        pltpu.VMEM((RET_DK, RET_DV), F32)],
        compiler_params=_params("parallel", "arbitrary"),
        name="retention",
    )(log_decay, mix, mix, mix)


def _att_kernel(q_ref, k_ref, v_ref, o_ref, v1_ref, q4_ref, s_ref, mx_ref, m_ref, acc_ref, *, chunks, n_ctx, t):
    i = pl.program_id(2)
    contract1 = (((1,), (1,)), ((), ()))
    tq = q_ref.shape[0]
    sb = ATT_SUB

    @pl.when(i == 0)
    def _():
        v1_ref[:, :ATT_HD] = v_ref[...]
        v1_ref[:, ATT_HD:] = jnp.ones((t, ATT_HD), BF16)

    for g in range(ATT_GROUP):
        q4_ref[g * tq:(g + 1) * tq, :] = q_ref[:, g * ATT_HD:(g + 1) * ATT_HD]
    neg = jnp.full(mx_ref.shape, -jnp.inf, F32)
    mx_ref[...] = neg
    m_ref[...] = neg
    acc_ref[...] = jnp.zeros(acc_ref.shape, F32)

    def a_step(buf, j, r0, size):
        k = k_ref[pl.ds(r0, size), :]
        s = lax.dot_general(q4_ref[...], k, contract1, preferred_element_type=F32)
        s_ref[buf, j, :, 0:size] = s
        mx = mx_ref[...]
        for tt in range(size // LANES):
            mx = jnp.maximum(mx, s[:, tt * LANES:(tt + 1) * LANES])
        mx_ref[...] = mx

    def settle():
        mx = mx_ref[...]
        mc = jnp.broadcast_to(jnp.max(mx, axis=-1, keepdims=True), mx.shape)
        m_old = m_ref[...]
        m_new = jnp.maximum(m_old, mc)
        alpha = jnp.exp2(m_old - m_new)
        m_ref[...] = m_new
        acc_ref[:, :ATT_HD] = acc_ref[:, :ATT_HD] * alpha
        acc_ref[:, ATT_HD:] = acc_ref[:, ATT_HD:] * alpha
        mx_ref[...] = neg

    def b_step(buf, j, r0, size):
        mb = m_ref[...]
        p = jnp.concatenate([jnp.exp2(s_ref[buf, j, :, tt * LANES:(tt + 1) * LANES] - mb)
                             for tt in range(size // LANES)], axis=1).astype(BF16)
        acc_ref[...] += jnp.dot(p, v1_ref[pl.ds(r0, size), :], preferred_element_type=F32)

    def stage(ca, cb):
        sides = [(a_step, ca), (b_step, cb)]
        sides = [(fn, ch) for fn, ch in sides if ch is not None]
        common = min(ch[2] // sb for _, ch in sides)

        def body(j, carry):
            off = pl.multiple_of(j * sb, sb)
            for fn, (buf, start, _) in sides:
                fn(buf, j, start + off, sb)
            return carry

        if common:
            lax.fori_loop(0, common, body, 0)
        for fn, (buf, start, size) in sides:
            for j in range(common, -(-size // sb)):
                fn(buf, j, start + j * sb, min(sb, size - j * sb))
        if ca is not None:
            settle()

    @pl.when(i == 0)
    def _():
        stage((0, 0, n_ctx), None)
        stage(None, (0, 0, n_ctx))

    @pl.when(i > 0)
    def _():
        bufs = [(c % 2, start, size) for c, (start, size) in enumerate(chunks)]
        for c in range(len(bufs) + 1):
            stage(bufs[c] if c < len(bufs) else None, bufs[c - 1] if c > 0 else None)

    for g in range(ATT_GROUP):
        rows = slice(g * tq, (g + 1) * tq)
        o_ref[:, g * ATT_HD:(g + 1) * ATT_HD] = (acc_ref[rows, :ATT_HD] / acc_ref[rows, ATT_HD:]).astype(BF16)


def _attention(mix, n_ctx):
    b, t, _ = mix.shape
    tq = ROW_TILE
    n_lat = t - n_ctx
    nch = 4 if n_lat % (4 * COL_TILE) == 0 else 1
    chunks = tuple((0, n_ctx + n_lat // nch) if c == 0 else (n_ctx + c * (n_lat // nch), n_lat // nch)
                   for c in range(nch))
    mrows = ATT_GROUP * tq
    q_blk0 = (3 * RET_W) // (ATT_GROUP * ATT_HD)
    k_blk0 = (3 * RET_W + ATT_W) // ATT_HD
    v_blk0 = k_blk0 + ATT_KV_HEADS
    kern = functools.partial(_att_kernel, chunks=chunks, n_ctx=n_ctx, t=t)
    return pl.pallas_call(
        kern,
        out_shape=jax.ShapeDtypeStruct((b, t, ATT_W), BF16),
        grid=(b, ATT_KV_HEADS, t // tq),
        in_specs=[pl.BlockSpec((None, tq, ATT_GROUP * ATT_HD), lambda bi, hk, i: (bi, i, q_blk0 + hk)),
                  pl.BlockSpec((None, t, ATT_HD), lambda bi, hk, i: (bi, 0, k_blk0 + hk)),
                  pl.BlockSpec((None, t, ATT_HD), lambda bi, hk, i: (bi, 0, v_blk0 + hk))],
        out_specs=pl.BlockSpec((None, tq, ATT_GROUP * ATT_HD), lambda bi, hk, i: (bi, i, hk)),
        scratch_shapes=[pltpu.VMEM((t, 2 * ATT_HD), BF16),
                        pltpu.VMEM((mrows, ATT_HD), BF16),
                        pltpu.VMEM((2, -(-chunks[0][1] // ATT_SUB), mrows, ATT_SUB), F32),
                        pltpu.VMEM((mrows, LANES), F32),
                        pltpu.VMEM((mrows, LANES), F32),
                        pltpu.VMEM((mrows, 2 * ATT_HD), F32)],
        compiler_params=_params("parallel", "parallel", "arbitrary"),
        name="gqa_attention",
    )(mix, mix, mix)


def _s5_kernel(u_ref, win_ref, tsum_ref, wout_ref, lam_ref, y_ref, uc_ref, d_ref, xin_ref, yw_ref, *, nc, ncc):
    p2 = 2 * S5_STATE
    sub = 8
    ln = S5_CHUNK
    gpb = LANES // S5_GROUP
    us = [pltpu.bitcast(u_ref[pl.ds(s, nc, stride=ln), :].astype(BF16), jnp.uint32) for s in range(ln)]
    for g in range(gpb):
        uc = pltpu.bitcast(
            jnp.concatenate([us[s][:, g * S5_GROUP:(g + 1) * S5_GROUP] for s in range(ln)], axis=1), BF16)
        uc_ref[g] = uc
        d_ref[g] = jnp.dot(uc, win_ref[g], preferred_element_type=F32)
    rowid = lax.broadcasted_iota(jnp.int32, (sub, p2), 0)

    def tile(g, reverse, rows, c, cs):
        base = 32 if reverse else 0
        col = 2 * p2 if reverse else 0

        def mult(i):
            return tuple(jnp.broadcast_to(lam_ref[g, base + 3 * i + r:base + 3 * i + r + 1, :], (sub, p2))
                         for r in range(3))

        def shifted(v, n):
            if reverse:
                return jnp.where(rowid < sub - n, pltpu.roll(v, sub - n, 0), 0.0)
            return jnp.where(rowid >= n, pltpu.roll(v, n, 0), 0.0)

        x = d_ref[g, rows, col:col + p2]
        xs = d_ref[g, rows, col + p2:col + 2 * p2]
        for i, n in enumerate((1, 2, 4)):
            a, b, bs = mult(i)
            xr, xsr = shifted(x, n), shifted(xs, n)
            x, xs = x + a * xr + b * xsr, xs + a * xsr + bs * xr
        pa = lam_ref[g, base + 16:base + 24, :]
        pb = lam_ref[g, base + 24:base + 32, :]
        xin_ref[g, rows, col // 2:col // 2 + p2] = shifted(x, 1) + pa * c + pb * cs
        edge = 0 if reverse else sub - 1
        last = jnp.broadcast_to(x[edge:edge + 1, :], (sub, p2))
        lasts = jnp.broadcast_to(xs[edge:edge + 1, :], (sub, p2))
        a, b, bs = mult(3)
        return a * c + b * cs + last, a * cs + bs * c + lasts

    nt = nc // sub
    nct = ncc // sub

    def step(mf, mb, carry):
        rf = pl.ds(pl.multiple_of(mf * sub, sub), sub)
        rb = pl.ds(pl.multiple_of(mb * sub, sub), sub)
        return tuple(tile(g, False, rf, carry[g][0], carry[g][1]) + tile(g, True, rb, carry[g][2], carry[g][3])
                     for g in range(gpb))

    zero = jnp.zeros((sub, p2), F32)
    init = tuple((zero, zero, zero, zero) for _ in range(gpb))
    carry = lax.fori_loop(0, nct, lambda t, cr: step(t, nct - 1 - t, cr), init)
    lax.fori_loop(nct, nt, lambda t, cr: step(t, nt - 1 - (t - nct), cr), carry)

    for g in range(gpb):
        y = jnp.dot(uc_ref[g], tsum_ref[g], preferred_element_type=F32)
        y += jnp.dot(xin_ref[g].astype(BF16), wout_ref[g], preferred_element_type=F32)
        yw_ref[g] = pltpu.bitcast(y.astype(BF16), jnp.uint32)
    for s in range(ln):
        piece = jnp.concatenate([yw_ref[g, :, s * S5_GROUP:(s + 1) * S5_GROUP] for g in range(gpb)], axis=1)
        y_ref[pl.ds(s, nc, stride=ln), :] = pltpu.bitcast(piece, BF16).astype(F32)


def _s5_weights(a_re, a_im, log_dt, b_re, b_im, c_re, c_im, d_skip):
    hi = lax.Precision.HIGHEST
    ln = S5_CHUNK
    dt = jnp.exp(log_dt)[..., None]
    mag = jnp.exp(a_re * dt)
    abr, abi = mag * jnp.cos(a_im * dt), mag * jnp.sin(a_im * dt)
    den = a_re * a_re + a_im * a_im
    fr = ((abr - 1.0) * a_re + abi * a_im) / den
    fi = (abi * a_re - (abr - 1.0) * a_im) / den
    bbr = fr[..., None] * b_re - fi[..., None] * b_im
    bbi = fr[..., None] * b_im + fi[..., None] * b_re
    n = jnp.arange(ln + 1, dtype=F32)[:, None, None, None]
    pmag = jnp.exp(a_re * dt * n)
    pr, pi = pmag * jnp.cos(a_im * dt * n), pmag * jnp.sin(a_im * dt * n)
    cr = c_re[None] * pr[:, :, :, None, :] - c_im[None] * pi[:, :, :, None, :]
    ci = c_re[None] * pi[:, :, :, None, :] + c_im[None] * pr[:, :, :, None, :]
    g_ = a_re.shape[1]
    lhs = jnp.concatenate([cr[:ln], -ci[:ln]], axis=-1).transpose(1, 2, 0, 3, 4)
    lhs = lhs.reshape(2, g_, ln * S5_GROUP, 2 * S5_STATE)
    rhs = jnp.concatenate([bbr, bbi], axis=2)
    m = jnp.einsum('dgmk,dgke->dgme', lhs, rhs, precision=hi)
    m = m.reshape(2, g_, ln, S5_GROUP, S5_GROUP).transpose(2, 0, 1, 3, 4)
    lag = jnp.arange(ln)[None, :] - jnp.arange(ln)[:, None]
    n_idx = jnp.arange(ln)[:, None, None]
    oh_f = (lag[None] == n_idx).astype(F32)
    oh_b = (-lag[None] == n_idx).astype(F32)
    tsum = (jnp.einsum('nsi,ngce->gseic', oh_f, m[:, 0], precision=hi)
            + jnp.einsum('nsi,ngce->gseic', oh_b, m[:, 1], precision=hi))
    g = a_re.shape[1]
    eye_s = jnp.eye(ln, dtype=F32)[None, :, None, :, None]
    eye_c = jnp.eye(S5_GROUP, dtype=F32)[None, None, :, None, :]
    tsum = tsum + eye_s * eye_c * d_skip.reshape(g, 1, S5_GROUP, 1, 1)
    tsum = tsum.reshape(g, ln * S5_GROUP, ln * S5_GROUP)

    def w_in(pw_r, pw_i, d):
        re = pw_r[..., None] * bbr[d][None] - pw_i[..., None] * bbi[d][None]
        im = pw_r[..., None] * bbi[d][None] + pw_i[..., None] * bbr[d][None]
        re = re.transpose(1, 0, 3, 2).reshape(g, ln * S5_GROUP, S5_STATE)
        im = im.transpose(1, 0, 3, 2).reshape(g, ln * S5_GROUP, S5_STATE)
        return jnp.concatenate([re, im, im, re], axis=-1)

    win = jnp.concatenate([w_in(pr[:ln, 0][::-1], pi[:ln, 0][::-1], 0),
                           w_in(pr[:ln, 1], pi[:ln, 1], 1)], axis=-1)

    def w_out(cr_d, ci_d):
        re = cr_d.transpose(1, 3, 0, 2).reshape(g, S5_STATE, ln * S5_GROUP)
        im = (-ci_d).transpose(1, 3, 0, 2).reshape(g, S5_STATE, ln * S5_GROUP)
        return jnp.concatenate([re, im], axis=1)

    wout = jnp.concatenate([w_out(cr[1:ln + 1, 0], ci[1:ln + 1, 0]),
                            w_out(cr[1:ln + 1, 1][::-1], ci[1:ln + 1, 1][::-1])], axis=1)

    def cpow(k):
        e = (jnp.asarray(k, F32) * ln)[..., None, None, None]
        mg = jnp.exp(a_re * dt * e)
        return mg * jnp.cos(a_im * dt * e), mg * jnp.sin(a_im * dt * e)

    def dup(a, b):
        return jnp.concatenate([a, b], axis=-1)

    mr, mi = cpow(jnp.array([1, 2, 4, 8]))
    mult = jnp.stack([dup(mr, mr), dup(-mi, mi), dup(mi, -mi)], axis=1).reshape(12, 2, g, 2 * S5_STATE)
    qr, qi = cpow(jnp.arange(8))
    qr = jnp.stack([qr[:, 0], qr[::-1, 1]], axis=1)
    qi = jnp.stack([qi[:, 0], qi[::-1, 1]], axis=1)
    lam = jnp.concatenate([mult, jnp.zeros((4, 2, g, 2 * S5_STATE), F32), dup(qr, qr), dup(-qi, qi)], axis=0)
    lam = lam.transpose(2, 1, 0, 3).reshape(g, 64, 2 * S5_STATE)
    return win.astype(BF16), tsum.astype(BF16), wout.astype(BF16), lam


def _s5_in_kernel(u_ref, w_ref, o_ref):
    o_ref[...] = jnp.dot(u_ref[...], w_ref[...], preferred_element_type=F32)


def _s5_in_proj(u, w_s5):
    b, t, d = u.shape
    tm = _big_row_tile(t)
    return pl.pallas_call(
        _s5_in_kernel,
        out_shape=jax.ShapeDtypeStruct((b, t, S5_W), F32),
        grid=(b, t // tm),
        in_specs=[pl.BlockSpec((None, tm, d), lambda bi, i: (bi, i, 0)),
                  pl.BlockSpec((d, S5_W), lambda bi, i: (0, 0))],
        out_specs=pl.BlockSpec((None, tm, S5_W), lambda bi, i: (bi, i, 0)),
        compiler_params=_params("parallel", "arbitrary"),
        name="s5_in_proj",
    )(u, w_s5)


def _s5(u5, s5w, layer, n_ctx):
    b, t, _ = u5.shape
    win, tsum, wout, lam = s5w
    gpb = LANES // S5_GROUP
    nc = t // S5_CHUNK
    kw = S5_CHUNK * S5_GROUP
    assert (n_ctx // S5_CHUNK) % 8 == 0 and nc % 8 == 0
    kern = functools.partial(_s5_kernel, nc=nc, ncc=n_ctx // S5_CHUNK)
    wspec = lambda r, c: pl.BlockSpec((None, gpb, r, c), lambda bi, gb: (layer, gb, 0, 0))
    return pl.pallas_call(
        kern,
        out_shape=jax.ShapeDtypeStruct((b, t, S5_W), F32),
        grid=(b, S5_W // LANES),
        in_specs=[pl.BlockSpec((None, t, LANES), lambda bi, gb: (bi, 0, gb)),
                  wspec(kw, 8 * S5_STATE), wspec(kw, kw), wspec(4 * S5_STATE, kw), wspec(64, 2 * S5_STATE)],
        out_specs=pl.BlockSpec((None, t, LANES), lambda bi, gb: (bi, 0, gb)),
        scratch_shapes=[pltpu.VMEM((gpb, nc, kw), BF16),
                        pltpu.VMEM((gpb, nc, 8 * S5_STATE), F32),
                        pltpu.VMEM((gpb, nc, 4 * S5_STATE), F32),
                        pltpu.VMEM((gpb, nc // 2, kw), jnp.uint32)],
        compiler_params=_params("parallel", "parallel"),
        name="s5_scan",
    )(u5, win, tsum, wout, lam)


def _glu_kernel(s_ref, u_ref, wg_ref, w_ref, b_ref, o_ref):
    s = jax.nn.gelu(s_ref[...])
    z = jnp.dot(s.astype(BF16), w_ref[...], preferred_element_type=F32) + b_ref[...]
    g = jnp.dot(u_ref[...], wg_ref[...], preferred_element_type=F32)
    o_ref[...] = (s * jax.nn.sigmoid(z) * (g * jax.nn.sigmoid(g))).astype(BF16)


def _s5_glu_gate(s_pre, u, w_s5, glu_w, glu_b):
    b, t, d = u.shape
    tm = _big_row_tile(t)
    return pl.pallas_call(
        _glu_kernel,
        out_shape=jax.ShapeDtypeStruct((b, t, S5_W), BF16),
        grid=(b, t // tm),
        in_specs=[pl.BlockSpec((None, tm, S5_W), lambda bi, i: (bi, i, 0)),
                  pl.BlockSpec((None, tm, d), lambda bi, i: (bi, i, 0)),
                  pl.BlockSpec((d, S5_W), lambda bi, i: (0, 1)),
                  pl.BlockSpec((S5_W, S5_W), lambda bi, i: (0, 0)),
                  pl.BlockSpec((1, S5_W), lambda bi, i: (0, 0))],
        out_specs=pl.BlockSpec((None, tm, S5_W), lambda bi, i: (bi, i, 0)),
        compiler_params=_params("parallel", "arbitrary"),
        name="s5_glu_gate",
    )(s_pre, u, w_s5, glu_w, glu_b)


def _gate_kernel(u_ref, wr_ref, wa0_ref, wa1_ref, r_ref, a_ref, gn_ref, o_ref):
    u = u_ref[...]

    def gated(w, act, cols):
        g = jnp.dot(u, w, preferred_element_type=F32)
        o_ref[:, cols] = (act * (g * jax.nn.sigmoid(g))).astype(BF16)

    def group_norm(x):
        mu = jnp.mean(x, axis=-1, keepdims=True)
        xc = x - mu
        var = jnp.mean(xc * xc, axis=-1, keepdims=True)
        return xc * lax.rsqrt(var + LN_EPS)

    for tile in range(RET_W // COL_TILE):
        cols = slice(tile * COL_TILE, (tile + 1) * COL_TILE)
        heads = [group_norm(r_ref[:, c0:c0 + RET_DV].astype(F32))
                 for c0 in range(cols.start, cols.stop, RET_DV)]
        gated(wr_ref[:, cols], jnp.concatenate(heads, axis=1) * gn_ref[:, cols], cols)
    for tile, w_ref in enumerate((wa0_ref, wa1_ref)):
        cols = slice(tile * COL_TILE, (tile + 1) * COL_TILE)
        gated(w_ref[...], a_ref[:, cols].astype(F32), slice(RET_W + cols.start, RET_W + cols.stop))


def _branch_gates(u, wb, layer, r, a, gn_w):
    b, t, d = u.shape
    tm = _resident_row_tile(t)
    assert ATT_W == 2 * COL_TILE
    ret_g_blk = 3
    att_g0 = (4 * RET_W + ATT_W + 2 * ATT_KV_W) // COL_TILE
    resident = lambda width, blk: pl.BlockSpec((None, d, width), lambda bi, i: (layer, 0, blk),
                                               pipeline_mode=pl.Buffered(1))
    return pl.pallas_call(
        _gate_kernel,
        out_shape=jax.ShapeDtypeStruct((b, t, RET_W + ATT_W), BF16),
        grid=(b, t // tm),
        in_specs=[pl.BlockSpec((None, tm, d), lambda bi, i: (bi, i, 0)),
                  resident(RET_W, ret_g_blk), resident(COL_TILE, att_g0), resident(COL_TILE, att_g0 + 1),
                  pl.BlockSpec((None, tm, RET_W), lambda bi, i: (bi, i, 0)),
                  pl.BlockSpec((None, tm, ATT_W), lambda bi, i: (bi, i, 0)),
                  pl.BlockSpec((1, RET_W), lambda bi, i: (0, 0))],
        out_specs=pl.BlockSpec((None, tm, RET_W + ATT_W), lambda bi, i: (bi, i, 0)),
        compiler_params=_params("parallel", "arbitrary"),
        name="branch_gates",
    )(u, wb, wb, wb, r, a, gn_w)


def _merge_kernel(u_ref, zra_ref, zs_ref, wm0_ref, wm1_ref, wm2_ref, wr_ref, wa_ref, ws_ref, o_ref):
    tm = u_ref.shape[0]
    nsplit = 2 if tm % 32 == 0 else 1
    hr = tm // nsplit
    for r in range(nsplit):
        rows = slice(r * hr, (r + 1) * hr)
        u = u_ref[rows, :]
        acts = (zra_ref[rows, :RET_W], zra_ref[rows, RET_W:], zs_ref[rows, :])
        acc = None
        for act, wm_ref, wbr_ref in zip(acts, (wm0_ref, wm1_ref, wm2_ref), (wr_ref, wa_ref, ws_ref)):
            gate = jax.nn.sigmoid(jnp.dot(u, wm_ref[...], preferred_element_type=F32))
            proj = jnp.dot(act, wbr_ref[...], preferred_element_type=F32)
            acc = gate * proj if acc is None else acc + gate * proj
        o_ref[rows, :] = acc.astype(BF16)


def _merge(u, z_ra, z_s, wb, layer, w_ret, w_att, w_s5):
    b, t, d = u.shape
    tm = _big_row_tile(t)
    m0 = (wb.shape[2] - N_BRANCH * d) // COL_TILE
    per = d // COL_TILE
    wm = lambda br: pl.BlockSpec((None, d, COL_TILE), lambda bi, i, j: (layer, 0, m0 + br * per + j))
    wbr = lambda k: pl.BlockSpec((k, COL_TILE), lambda bi, i, j: (0, j))
    return pl.pallas_call(
        _merge_kernel,
        out_shape=jax.ShapeDtypeStruct((b, t, d), BF16),
        grid=(b, t // tm, d // COL_TILE),
        in_specs=[pl.BlockSpec((None, tm, d), lambda bi, i, j: (bi, i, 0)),
                  pl.BlockSpec((None, tm, RET_W + ATT_W), lambda bi, i, j: (bi, i, 0)),
                  pl.BlockSpec((None, tm, S5_W), lambda bi, i, j: (bi, i, 0)),
                  wm(0), wm(1), wm(2), wbr(RET_W), wbr(ATT_W), wbr(S5_W)],
        out_specs=pl.BlockSpec((None, tm, COL_TILE), lambda bi, i, j: (bi, i, j)),
        compiler_params=_params("parallel", "parallel", "arbitrary"),
        name="gated_merge",
    )(u, z_ra, z_s, wb, wb, wb, w_ret, w_att, w_s5)


def _out_kernel(*refs, alpha, first_tile, emit_u):
    if emit_u:
        m_ref, w_ref, c_ref, x_ref, mod_ref, lnw_ref, lnb_ref, nmod_ref, o_ref, u_ref, v_ref = refs
    else:
        m_ref, w_ref, c_ref, x_ref, mod_ref, lnw_ref, lnb_ref, o_ref, v_ref = refs
    i = pl.program_id(1)
    d = w_ref.shape[1]

    def run(res_ref):
        m = m_ref[...]
        s1 = jnp.zeros((m.shape[0], 1), F32)
        for c in range(d // COL_TILE):
            cols = slice(c * COL_TILE, (c + 1) * COL_TILE)
            y = jnp.dot(m, w_ref[:, cols], preferred_element_type=F32)
            v = alpha * res_ref[:, cols] + mod_ref[2:3, cols] * y
            v_ref[:, cols] = v
            s1 += jnp.sum(v, axis=-1, keepdims=True)
        mu = s1 * (1.0 / d)
        vc = v_ref[...] - mu
        var = jnp.mean(vc * vc, axis=-1, keepdims=True)
        out = vc * lax.rsqrt(var + LN_EPS) * lnw_ref[...] + lnb_ref[...]
        o_ref[...] = out
        if emit_u:
            u_ref[...] = (out * (1.0 + nmod_ref[1:2, :]) + nmod_ref[0:1, :]).astype(BF16)

    if first_tile == 0:
        @pl.when(i == 0)
        def _():
            run(c_ref)

        @pl.when(i > 0)
        def _():
            run(x_ref)
    else:
        run(x_ref)


def _out_proj(m, w_out, ctx_src, lat_src, modsel, ln_w, ln_b, alpha, skip_ctx, next_modsel=None):
    b, t, d = m.shape
    off = 1 if skip_ctx else 0
    nt = t // ROW_TILE - off
    emit_u = next_modsel is not None
    kern = functools.partial(_out_kernel, alpha=alpha, first_tile=off, emit_u=emit_u)
    vec = lambda: pl.BlockSpec((1, d), lambda bi, i: (0, 0))
    mod = lambda: pl.BlockSpec((None, None, 3, d), lambda bi, i: (bi, jnp.minimum(i + off, 1), 0, 0))
    row_out = lambda: pl.BlockSpec((None, ROW_TILE, d), lambda bi, i: (bi, i, 0))
    x_shape = jax.ShapeDtypeStruct((b, nt * ROW_TILE, d), F32)
    return pl.pallas_call(
        kern,
        out_shape=(x_shape, jax.ShapeDtypeStruct(x_shape.shape, BF16)) if emit_u else x_shape,
        grid=(b, nt),
        in_specs=[pl.BlockSpec((None, ROW_TILE, d), lambda bi, i: (bi, i + off, 0)),
                  pl.BlockSpec((d, d), lambda bi, i: (0, 0)),
                  *_token_specs(ctx_src, lat_src, off),
                  mod(), vec(), vec(), *([mod()] if emit_u else [])],
        out_specs=(row_out(), row_out()) if emit_u else row_out(),
        scratch_shapes=[pltpu.VMEM((ROW_TILE, d), F32)],
        compiler_params=_params("parallel", "arbitrary"),
        name="out_proj_norm",
    )(m, w_out, ctx_src, lat_src, modsel, ln_w.reshape(1, d), ln_b.reshape(1, d),
      *([next_modsel] if emit_u else []))


def _rope_tables(n, n_ctx):
    rows = n // GRID_W
    row = jnp.repeat(jnp.arange(rows, dtype=F32), GRID_W)
    col = jnp.tile(jnp.arange(GRID_W, dtype=F32), rows)

    def table(head_dim):
        per_axis = head_dim // 4
        inv = ROPE_THETA ** (-jnp.arange(per_axis, dtype=F32) / per_axis)
        ang = jnp.concatenate([row[:, None] * inv, col[:, None] * inv], axis=-1)
        cos = jnp.concatenate([jnp.ones((n_ctx, head_dim // 2), F32), jnp.cos(ang)], axis=0)
        sin = jnp.concatenate([jnp.zeros((n_ctx, head_dim // 2), F32), jnp.sin(ang)], axis=0)
        return cos, sin

    cos_r, sin_r = table(RET_DK)
    cos_a, sin_a = table(ATT_HD)
    return (cos_r, sin_r, jnp.concatenate([cos_a, cos_a], -1), jnp.concatenate([-sin_a, sin_a], -1))


def kernel(x, c, ctx, c_ctx, ada_w, ada_b, w_in, ret_log_decay, ret_gn_w, att_q_norm, att_k_norm, s5_a_re, s5_a_im, s5_log_dt, s5_b_re, s5_b_im, s5_c_re, s5_c_im, s5_d, s5_glu_w, s5_glu_b, w_br_ret, w_br_att, w_br_s5, w_out, ln_w, ln_b):
    b, n, d = x.shape
    n_ctx = ctx.shape[1]
    depth = w_in.shape[0]
    assert n_ctx == ROW_TILE and n % ROW_TILE == 0 and n % GRID_W == 0
    assert w_in.shape[2] == 4 * RET_W + 2 * ATT_W + 2 * ATT_KV_W + 2 * S5_W + N_BRANCH * d
    alpha = (2.0 * depth) ** 0.25
    s5_col0 = 4 * RET_W + 2 * ATT_W + 2 * ATT_KV_W

    t = n_ctx + n
    ctx_src, lat_src = ctx, x
    cos_r, sin_r, cos_a, sin_a = _rope_tables(n, n_ctx)
    crows = 16
    cvec = jnp.concatenate([c, c_ctx[None, :], jnp.zeros((crows - b - 1, d), F32)], axis=0)
    wb = w_in.astype(BF16)
    s5w = jax.vmap(_s5_weights)(s5_a_re, s5_a_im, s5_log_dt, s5_b_re, s5_b_im, s5_c_re, s5_c_im, s5_d)

    modsels = []
    for l in range(depth):
        mod = _modulation(cvec, ada_w, ada_b, l).reshape(crows, 3, d)
        modsels.append(jnp.stack([jnp.broadcast_to(mod[b][None], (b, 3, d)), mod[:b]], axis=1))

    u = _modulate(ctx_src, lat_src, t, modsels[0])
    for l in range(depth):
        last = l == depth - 1
        modsel = modsels[l]
        w_s5 = wb[l, :, s5_col0:s5_col0 + 2 * S5_W]
        mix = _mix_proj(u, wb, l, cos_r, sin_r, cos_a, sin_a,
                        att_q_norm[l].reshape(1, ATT_HD), att_k_norm[l].reshape(1, ATT_HD))
        r = _retention(mix, ret_log_decay[l], n_ctx)
        a = _attention(mix, n_ctx)
        s_pre = _s5(_s5_in_proj(u, w_s5), s5w, l, n_ctx)
        z_s = _s5_glu_gate(s_pre, u, w_s5, s5_glu_w[l].astype(BF16), s5_glu_b[l].reshape(1, S5_W))
        z_ra = _branch_gates(u, wb, l, r, a, ret_gn_w[l].reshape(1, RET_W))
        m = _merge(u, z_ra, z_s, wb, l, w_br_ret[l].astype(BF16), w_br_att[l].astype(BF16),
                   w_br_s5[l].astype(BF16))
        res = _out_proj(m, w_out[l].astype(BF16), ctx_src, lat_src, modsel, ln_w[l], ln_b[l], alpha, last,
                        None if last else modsels[l + 1])
        if last:
            return res
        xa, u = res
        ctx_src = lat_src = xa
```

```python
import functools
import math

import jax
import jax.numpy as jnp
from jax import lax
from jax.experimental import pallas as pl
from jax.experimental.pallas import tpu as pltpu

F32 = jnp.float32
BF16 = jnp.bfloat16

GRID_W = 64
RET_HEADS = 4
RET_DK = 256
RET_DV = 256
RET_W = RET_HEADS * RET_DV
ATT_HEADS = 8
ATT_KV_HEADS = 2
ATT_GROUP = ATT_HEADS // ATT_KV_HEADS
ATT_HD = 128
ATT_W = ATT_HEADS * ATT_HD
ATT_KV_W = ATT_KV_HEADS * ATT_HD
ROPE_THETA = 10000.0
S5_GROUP = 16
S5_W = 768
S5_GROUPS = S5_W // S5_GROUP
S5_STATE = 64
N_BRANCH = 3
LN_EPS = 1e-6
RMS_EPS = 1e-6

LANES = 128
VMEM_LIMIT_BYTES = 56 * 1024 * 1024

CHUNK = 128
S5_CHUNK = 16
ROW_TILE = 256
COL_TILE = 512
MIX_W = 4608
ATT_SUB = 2048
LOG2E = math.log2(math.e)


def _params(*sem):
    return pltpu.CompilerParams(dimension_semantics=sem, vmem_limit_bytes=VMEM_LIMIT_BYTES)


def _big_row_tile(t):
    for cand in (1056, 1024, 768, 640, 512, 256):
        if t % cand == 0:
            return cand
    raise ValueError(f"unsupported token count {t}")


def _resident_row_tile(t):
    return 528 if t % 528 == 0 else ROW_TILE


def _mod_kernel(c_ref, w_ref, b_ref, o_ref):
    c = c_ref[...]
    s = c * jax.nn.sigmoid(c)
    s_hi = s.astype(BF16)
    s_lo = (s - s_hi.astype(F32)).astype(BF16)
    w = w_ref[...]
    w_hi = w.astype(BF16)
    w_lo = (w - w_hi.astype(F32)).astype(BF16)
    acc = jnp.dot(s_hi, w_hi, preferred_element_type=F32)
    acc += jnp.dot(s_lo, w_hi, preferred_element_type=F32)
    acc += jnp.dot(s_hi, w_lo, preferred_element_type=F32)
    o_ref[...] = acc + b_ref[...]


def _modulation(cvec, ada_w, ada_b, layer):
    rows, d = cvec.shape
    n = ada_w.shape[2]
    tn = 768
    return pl.pallas_call(
        _mod_kernel,
        out_shape=jax.ShapeDtypeStruct((rows, n), F32),
        grid=(n // tn,),
        in_specs=[pl.BlockSpec((rows, d), lambda j: (0, 0)),
                  pl.BlockSpec((None, d, tn), lambda j: (layer, 0, j)),
                  pl.BlockSpec((None, 1, tn), lambda j: (layer, 0, j))],
        out_specs=pl.BlockSpec((rows, tn), lambda j: (0, j)),
        compiler_params=_params("arbitrary"),
        name="adaln_modulation",
    )(cvec, ada_w, ada_b.reshape(ada_b.shape[0], 1, n))


def _modulate_kernel(c_ref, x_ref, mod_ref, o_ref):
    i = pl.program_id(1)
    m = mod_ref[...]

    @pl.when(i == 0)
    def _():
        o_ref[...] = (c_ref[...] * (1.0 + m[1:2]) + m[0:1]).astype(BF16)

    @pl.when(i > 0)
    def _():
        o_ref[...] = (x_ref[...] * (1.0 + m[1:2]) + m[0:1]).astype(BF16)


def _token_specs(ctx_src, lat_src, first_tile=0):
    d = ctx_src.shape[2]
    shift = first_tile - (0 if lat_src is ctx_src else 1)
    return (pl.BlockSpec((None, ROW_TILE, d), lambda bi, i: (bi, 0, 0)),
            pl.BlockSpec((None, ROW_TILE, d), lambda bi, i: (bi, jnp.maximum(i + shift, 0), 0)))


def _modulate(ctx_src, lat_src, t, modsel):
    b, _, d = ctx_src.shape
    return pl.pallas_call(
        _modulate_kernel,
        out_shape=jax.ShapeDtypeStruct((b, t, d), BF16),
        grid=(b, t // ROW_TILE),
        in_specs=[*_token_specs(ctx_src, lat_src),
                  pl.BlockSpec((None, None, 3, d), lambda bi, i: (bi, jnp.minimum(i, 1), 0, 0))],
        out_specs=pl.BlockSpec((None, ROW_TILE, d), lambda bi, i: (bi, i, 0)),
        compiler_params=_params("parallel", "arbitrary"),
        name="modulate",
    )(ctx_src, lat_src, modsel)


def _mix_kernel(u_ref, wa_ref, wb_ref, cr_ref, sr_ref, ca_ref, sa_ref, qn_ref, kn_ref, o_ref):
    u = u_ref[...]

    def proj(w_ref, tile):
        return jnp.dot(u, w_ref[:, tile * COL_TILE:(tile + 1) * COL_TILE], preferred_element_type=F32)

    def rms_rope(x, w, scale):
        ss = jnp.dot((x * x).astype(BF16), jnp.ones((ATT_HD, ATT_HD), BF16), preferred_element_type=F32)
        y = x * lax.rsqrt(ss * (1.0 / ATT_HD) + RMS_EPS) * w
        y = y * ca_ref[...] + pltpu.roll(y, ATT_HD // 2, 1) * sa_ref[...]
        return (y * scale).astype(BF16)

    cos = cr_ref[...]
    sin = sr_ref[...]
    for tile in range(4):
        acc = proj(wa_ref, tile)
        scale = RET_DK ** -0.5 if tile >= 2 else 1.0
        for h in range(COL_TILE // RET_DK):
            c0 = tile * COL_TILE + h * RET_DK
            x1 = acc[:, h * RET_DK: h * RET_DK + LANES]
            x2 = acc[:, h * RET_DK + LANES: (h + 1) * RET_DK]
            o_ref[:, c0:c0 + LANES] = ((x1 * cos - x2 * sin) * scale).astype(BF16)
            o_ref[:, c0 + LANES:c0 + RET_DK] = ((x2 * cos + x1 * sin) * scale).astype(BF16)
    for tile in range(4, 6):
        o_ref[:, tile * COL_TILE:(tile + 1) * COL_TILE] = proj(wa_ref, tile).astype(BF16)
    for tile in range(2):
        acc = proj(wb_ref, tile)
        for h in range(COL_TILE // ATT_HD):
            c0 = (6 + tile) * COL_TILE + h * ATT_HD
            o_ref[:, c0:c0 + ATT_HD] = rms_rope(acc[:, h * ATT_HD:(h + 1) * ATT_HD], qn_ref[...],
                                                ATT_HD ** -0.5 * LOG2E)
    acc = proj(wb_ref, 2)
    c0 = 8 * COL_TILE
    for h in range(ATT_KV_HEADS):
        o_ref[:, c0 + h * ATT_HD:c0 + (h + 1) * ATT_HD] = rms_rope(acc[:, h * ATT_HD:(h + 1) * ATT_HD],
                                                                   kn_ref[...], 1.0)
    o_ref[:, c0 + ATT_KV_W:] = acc[:, ATT_KV_W:].astype(BF16)


def _mix_proj(u, wb, layer, cos_r, sin_r, cos_a, sin_a, qn, kn):
    b, t, d = u.shape
    tm = _resident_row_tile(t)
    tab = lambda: pl.BlockSpec((tm, LANES), lambda bi, i: (i, 0))
    vec = lambda: pl.BlockSpec((1, LANES), lambda bi, i: (0, 0))
    ret_w = 3 * RET_W
    att_w = 2 * ATT_W
    assert (4 * RET_W) % att_w == 0
    resident = lambda width, blk: pl.BlockSpec((None, d, width), lambda bi, i: (layer, 0, blk),
                                               pipeline_mode=pl.Buffered(1))
    return pl.pallas_call(
        _mix_kernel,
        out_shape=jax.ShapeDtypeStruct((b, t, MIX_W), BF16),
        grid=(b, t // tm),
        in_specs=[pl.BlockSpec((None, tm, d), lambda bi, i: (bi, i, 0)),
                  resident(ret_w, 0), resident(att_w, 4 * RET_W // att_w),
                  tab(), tab(), tab(), tab(), vec(), vec()],
        out_specs=pl.BlockSpec((None, tm, MIX_W), lambda bi, i: (bi, i, 0)),
        compiler_params=_params("parallel", "arbitrary"),
        name="mixer_in_proj",
    )(u, wb, wb, cos_r, sin_r, cos_a, sin_a, qn, kn)


def _ret_kernel(ld_ref, q_ref, k_ref, v_ref, o_ref, acc_ref, sf_ref, sb_ref, *, n_ctx_chunks, n_chunks):
    h = pl.program_id(1)
    lgf = ld_ref[0, h]
    lgb = ld_ref[1, h]
    c = CHUNK
    ri = lax.broadcasted_iota(jnp.int32, (c, c), 0).astype(F32)
    ci = lax.broadcasted_iota(jnp.int32, (c, c), 1).astype(F32)
    rel = ri - ci
    mask = jnp.where(rel >= 0, jnp.exp(lgf * jnp.maximum(rel, 0.0)), jnp.exp(lgb * jnp.maximum(-rel, 0.0)))
    row = lax.broadcasted_iota(jnp.int32, (c, RET_DV), 0).astype(F32)
    qdec_f = jnp.exp(lgf * (row + 1.0))
    qdec_b = jnp.exp(lgb * (c - row))
    kdec_f = jnp.exp(lgf * (c - 1.0 - row))
    kdec_b = jnp.exp(lgb * row)
    cdec_f = jnp.exp(jnp.full((1, RET_DV), lgf * c, F32))
    cdec_b = jnp.exp(jnp.full((1, RET_DV), lgb * c, F32))
    contract0 = (((0,), (0,)), ((), ()))
    contract1 = (((1,), (1,)), ((), ()))

    def load(ci_):
        r0 = pl.multiple_of(ci_ * c, c)
        rows = pl.ds(r0, c)
        return rows, q_ref[rows, :], k_ref[rows, :], v_ref[rows, :]

    def bwd(ci_):
        rows, q, k, v = load(ci_)
        s = sb_ref[...]
        acc_ref[rows, :] += jnp.dot(q, s.astype(BF16), preferred_element_type=F32) * qdec_b
        kd = (k.astype(F32) * kdec_b).astype(BF16)
        sb_ref[...] = s * cdec_b + lax.dot_general(kd, v, contract0, preferred_element_type=F32)

    def fwd(ci_):
        rows, q, k, v = load(ci_)
        s = sf_ref[...]
        sc = lax.dot_general(q, k, contract1, preferred_element_type=F32) * mask
        o = jnp.dot(sc.astype(BF16), v, preferred_element_type=F32)
        o += jnp.dot(q, s.astype(BF16), preferred_element_type=F32) * qdec_f
        acc_ref[rows, :] += o
        kd = (k.astype(F32) * kdec_f).astype(BF16)
        sf_ref[...] = s * cdec_f + lax.dot_general(kd, v, contract0, preferred_element_type=F32)

    sf_ref[...] = jnp.zeros_like(sf_ref)
    sb_ref[...] = jnp.zeros_like(sb_ref)
    acc_ref[...] = jnp.zeros_like(acc_ref)

    def ctx_part(t, carry):
        fwd(t)
        bwd(n_ctx_chunks - 1 - t)
        return carry

    def lat_part(t, carry):
        fwd(t)
        bwd(n_chunks - 1 - (t - n_ctx_chunks))
        return carry

    lax.fori_loop(0, n_ctx_chunks, ctx_part, 0, unroll=2)
    lax.fori_loop(n_ctx_chunks, n_chunks, lat_part, 0, unroll=8)

    def emit(t, carry):
        rows = pl.ds(pl.multiple_of(t * c, c), c)
        o_ref[rows, :] = acc_ref[rows, :].astype(BF16)
        return carry

    lax.fori_loop(0, n_chunks, emit, 0, unroll=6)


def _retention(mix, log_decay, n_ctx):
    b, t, _ = mix.shape
    kern = functools.partial(_ret_kernel, n_ctx_chunks=n_ctx // CHUNK, n_chunks=t // CHUNK)
    blk = lambda off: pl.BlockSpec((None, t, RET_DK), lambda bi, h: (bi, 0, off + h))
    return pl.pallas_call(
        kern,
        out_shape=jax.ShapeDtypeStruct((b, t, RET_W), BF16),
        grid=(b, RET_HEADS),
        in_specs=[pl.BlockSpec(memory_space=pltpu.SMEM), blk(0), blk(RET_HEADS), blk(2 * RET_HEADS)],
        out_specs=pl.BlockSpec((None, t, RET_DV), lambda bi, h: (bi, 0, h)),
        scratch_shapes=[pltpu.VMEM((t, RET_DV), F32), pltpu.VMEM((RET_DK, RET_DV), F32),
                        pltpu.VMEM((RET_DK, RET_DV), F32)],
        compiler_params=_params("parallel", "arbitrary"),
        name="retention",
    )(log_decay, mix, mix, mix)


def _att_kernel(q_ref, k_ref, v_ref, o_ref, v1_ref, q4_ref, s_ref, mx_ref, m_ref, acc_ref, *, chunks, n_ctx, t):
    i = pl.program_id(2)
    contract1 = (((1,), (1,)), ((), ()))
    tq = q_ref.shape[0]
    sb = ATT_SUB

    @pl.when(i == 0)
    def _():
        v1_ref[:, :ATT_HD] = v_ref[...]
        v1_ref[:, ATT_HD:] = jnp.ones((t, ATT_HD), BF16)

    for g in range(ATT_GROUP):
        q4_ref[g * tq:(g + 1) * tq, :] = q_ref[:, g * ATT_HD:(g + 1) * ATT_HD]
    neg = jnp.full(mx_ref.shape, -jnp.inf, F32)
    mx_ref[...] = neg
    m_ref[...] = neg
    acc_ref[...] = jnp.zeros(acc_ref.shape, F32)

    def a_step(buf, j, r0, size):
        k = k_ref[pl.ds(r0, size), :]
        s = lax.dot_general(q4_ref[...], k, contract1, preferred_element_type=F32)
        s_ref[buf, j, :, 0:size] = s
        mx = mx_ref[...]
        for tt in range(size // LANES):
            mx = jnp.maximum(mx, s[:, tt * LANES:(tt + 1) * LANES])
        mx_ref[...] = mx

    def settle():
        mx = mx_ref[...]
        mc = jnp.broadcast_to(jnp.max(mx, axis=-1, keepdims=True), mx.shape)
        m_old = m_ref[...]
        m_new = jnp.maximum(m_old, mc)
        alpha = jnp.exp2(m_old - m_new)
        m_ref[...] = m_new
        acc_ref[:, :ATT_HD] = acc_ref[:, :ATT_HD] * alpha
        acc_ref[:, ATT_HD:] = acc_ref[:, ATT_HD:] * alpha
        mx_ref[...] = neg

    def b_step(buf, j, r0, size):
        mb = m_ref[...]
        p = jnp.concatenate([jnp.exp2(s_ref[buf, j, :, tt * LANES:(tt + 1) * LANES] - mb)
                             for tt in range(size // LANES)], axis=1).astype(BF16)
        acc_ref[...] += jnp.dot(p, v1_ref[pl.ds(r0, size), :], preferred_element_type=F32)

    def stage(ca, cb):
        sides = [(a_step, ca), (b_step, cb)]
        sides = [(fn, ch) for fn, ch in sides if ch is not None]
        common = min(ch[2] // sb for _, ch in sides)

        def body(j, carry):
            off = pl.multiple_of(j * sb, sb)
            for fn, (buf, start, _) in sides:
                fn(buf, j, start + off, sb)
            return carry

        if common:
            lax.fori_loop(0, common, body, 0)
        for fn, (buf, start, size) in sides:
            for j in range(common, -(-size // sb)):
                fn(buf, j, start + j * sb, min(sb, size - j * sb))
        if ca is not None:
            settle()

    @pl.when(i == 0)
    def _():
        stage((0, 0, n_ctx), None)
        stage(None, (0, 0, n_ctx))

    @pl.when(i > 0)
    def _():
        bufs = [(c % 2, start, size) for c, (start, size) in enumerate(chunks)]
        for c in range(len(bufs) + 1):
            stage(bufs[c] if c < len(bufs) else None, bufs[c - 1] if c > 0 else None)

    for g in range(ATT_GROUP):
        rows = slice(g * tq, (g + 1) * tq)
        o_ref[:, g * ATT_HD:(g + 1) * ATT_HD] = (acc_ref[rows, :ATT_HD] / acc_ref[rows, ATT_HD:]).astype(BF16)


def _attention(mix, n_ctx):
    b, t, _ = mix.shape
    tq = ROW_TILE
    n_lat = t - n_ctx
    nch = 4 if n_lat % (4 * COL_TILE) == 0 else 1
    chunks = tuple((0, n_ctx + n_lat // nch) if c == 0 else (n_ctx + c * (n_lat // nch), n_lat // nch)
                   for c in range(nch))
    mrows = ATT_GROUP * tq
    q_blk0 = (3 * RET_W) // (ATT_GROUP * ATT_HD)
    k_blk0 = (3 * RET_W + ATT_W) // ATT_HD
    v_blk0 = k_blk0 + ATT_KV_HEADS
    kern = functools.partial(_att_kernel, chunks=chunks, n_ctx=n_ctx, t=t)
    return pl.pallas_call(
        kern,
        out_shape=jax.ShapeDtypeStruct((b, t, ATT_W), BF16),
        grid=(b, ATT_KV_HEADS, t // tq),
        in_specs=[pl.BlockSpec((None, tq, ATT_GROUP * ATT_HD), lambda bi, hk, i: (bi, i, q_blk0 + hk)),
                  pl.BlockSpec((None, t, ATT_HD), lambda bi, hk, i: (bi, 0, k_blk0 + hk)),
                  pl.BlockSpec((None, t, ATT_HD), lambda bi, hk, i: (bi, 0, v_blk0 + hk))],
        out_specs=pl.BlockSpec((None, tq, ATT_GROUP * ATT_HD), lambda bi, hk, i: (bi, i, hk)),
        scratch_shapes=[pltpu.VMEM((t, 2 * ATT_HD), BF16),
                        pltpu.VMEM((mrows, ATT_HD), BF16),
                        pltpu.VMEM((2, -(-chunks[0][1] // ATT_SUB), mrows, ATT_SUB), F32),
                        pltpu.VMEM((mrows, LANES), F32),
                        pltpu.VMEM((mrows, LANES), F32),
                        pltpu.VMEM((mrows, 2 * ATT_HD), F32)],
        compiler_params=_params("parallel", "parallel", "arbitrary"),
        name="gqa_attention",
    )(mix, mix, mix)


def _s5_kernel(u_ref, win_ref, tsum_ref, wout_ref, lam_ref, y_ref, uc_ref, d_ref, xin_ref, yw_ref, *, nc, ncc):
    p2 = 2 * S5_STATE
    sub = 8
    ln = S5_CHUNK
    gpb = LANES // S5_GROUP
    us = [pltpu.bitcast(u_ref[pl.ds(s, nc, stride=ln), :].astype(BF16), jnp.uint32) for s in range(ln)]
    for g in range(gpb):
        uc = pltpu.bitcast(
            jnp.concatenate([us[s][:, g * S5_GROUP:(g + 1) * S5_GROUP] for s in range(ln)], axis=1), BF16)
        uc_ref[g] = uc
        d_ref[g] = jnp.dot(uc, win_ref[g], preferred_element_type=F32)
    rowid = lax.broadcasted_iota(jnp.int32, (sub, p2), 0)

    def tile(g, reverse, rows, c, cs):
        base = 32 if reverse else 0
        col = 2 * p2 if reverse else 0

        def mult(i):
            return tuple(jnp.broadcast_to(lam_ref[g, base + 3 * i + r:base + 3 * i + r + 1, :], (sub, p2))
                         for r in range(3))

        def shifted(v, n):
            if reverse:
                return jnp.where(rowid < sub - n, pltpu.roll(v, sub - n, 0), 0.0)
            return jnp.where(rowid >= n, pltpu.roll(v, n, 0), 0.0)

        x = d_ref[g, rows, col:col + p2]
        xs = d_ref[g, rows, col + p2:col + 2 * p2]
        for i, n in enumerate((1, 2, 4)):
            a, b, bs = mult(i)
            xr, xsr = shifted(x, n), shifted(xs, n)
            x, xs = x + a * xr + b * xsr, xs + a * xsr + bs * xr
        pa = lam_ref[g, base + 16:base + 24, :]
        pb = lam_ref[g, base + 24:base + 32, :]
        xin_ref[g, rows, col // 2:col // 2 + p2] = shifted(x, 1) + pa * c + pb * cs
        edge = 0 if reverse else sub - 1
        last = jnp.broadcast_to(x[edge:edge + 1, :], (sub, p2))
        lasts = jnp.broadcast_to(xs[edge:edge + 1, :], (sub, p2))
        a, b, bs = mult(3)
        return a * c + b * cs + last, a * cs + bs * c + lasts

    nt = nc // sub
    nct = ncc // sub

    def step(mf, mb, carry):
        rf = pl.ds(pl.multiple_of(mf * sub, sub), sub)
        rb = pl.ds(pl.multiple_of(mb * sub, sub), sub)
        return tuple(tile(g, False, rf, carry[g][0], carry[g][1]) + tile(g, True, rb, carry[g][2], carry[g][3])
                     for g in range(gpb))

    zero = jnp.zeros((sub, p2), F32)
    init = tuple((zero, zero, zero, zero) for _ in range(gpb))
    carry = lax.fori_loop(0, nct, lambda t, cr: step(t, nct - 1 - t, cr), init)
    lax.fori_loop(nct, nt, lambda t, cr: step(t, nt - 1 - (t - nct), cr), carry)

    for g in range(gpb):
        y = jnp.dot(uc_ref[g], tsum_ref[g], preferred_element_type=F32)
        y += jnp.dot(xin_ref[g].astype(BF16), wout_ref[g], preferred_element_type=F32)
        yw_ref[g] = pltpu.bitcast(y.astype(BF16), jnp.uint32)
    for s in range(ln):
        piece = jnp.concatenate([yw_ref[g, :, s * S5_GROUP:(s + 1) * S5_GROUP] for g in range(gpb)], axis=1)
        y_ref[pl.ds(s, nc, stride=ln), :] = pltpu.bitcast(piece, BF16).astype(F32)


def _s5_weights(a_re, a_im, log_dt, b_re, b_im, c_re, c_im, d_skip):
    hi = lax.Precision.HIGHEST
    ln = S5_CHUNK
    dt = jnp.exp(log_dt)[..., None]
    mag = jnp.exp(a_re * dt)
    abr, abi = mag * jnp.cos(a_im * dt), mag * jnp.sin(a_im * dt)
    den = a_re * a_re + a_im * a_im
    fr = ((abr - 1.0) * a_re + abi * a_im) / den
    fi = (abi * a_re - (abr - 1.0) * a_im) / den
    bbr = fr[..., None] * b_re - fi[..., None] * b_im
    bbi = fr[..., None] * b_im + fi[..., None] * b_re
    n = jnp.arange(ln + 1, dtype=F32)[:, None, None, None]
    pmag = jnp.exp(a_re * dt * n)
    pr, pi = pmag * jnp.cos(a_im * dt * n), pmag * jnp.sin(a_im * dt * n)
    cr = c_re[None] * pr[:, :, :, None, :] - c_im[None] * pi[:, :, :, None, :]
    ci = c_re[None] * pi[:, :, :, None, :] + c_im[None] * pr[:, :, :, None, :]
    g_ = a_re.shape[1]
    lhs = jnp.concatenate([cr[:ln], -ci[:ln]], axis=-1).transpose(1, 2, 0, 3, 4)
    lhs = lhs.reshape(2, g_, ln * S5_GROUP, 2 * S5_STATE)
    rhs = jnp.concatenate([bbr, bbi], axis=2)
    m = jnp.einsum('dgmk,dgke->dgme', lhs, rhs, precision=hi)
    m = m.reshape(2, g_, ln, S5_GROUP, S5_GROUP).transpose(2, 0, 1, 3, 4)
    lag = jnp.arange(ln)[None, :] - jnp.arange(ln)[:, None]
    n_idx = jnp.arange(ln)[:, None, None]
    oh_f = (lag[None] == n_idx).astype(F32)
    oh_b = (-lag[None] == n_idx).astype(F32)
    tsum = (jnp.einsum('nsi,ngce->gseic', oh_f, m[:, 0], precision=hi)
            + jnp.einsum('nsi,ngce->gseic', oh_b, m[:, 1], precision=hi))
    g = a_re.shape[1]
    eye_s = jnp.eye(ln, dtype=F32)[None, :, None, :, None]
    eye_c = jnp.eye(S5_GROUP, dtype=F32)[None, None, :, None, :]
    tsum = tsum + eye_s * eye_c * d_skip.reshape(g, 1, S5_GROUP, 1, 1)
    tsum = tsum.reshape(g, ln * S5_GROUP, ln * S5_GROUP)

    def w_in(pw_r, pw_i, d):
        re = pw_r[..., None] * bbr[d][None] - pw_i[..., None] * bbi[d][None]
        im = pw_r[..., None] * bbi[d][None] + pw_i[..., None] * bbr[d][None]
        re = re.transpose(1, 0, 3, 2).reshape(g, ln * S5_GROUP, S5_STATE)
        im = im.transpose(1, 0, 3, 2).reshape(g, ln * S5_GROUP, S5_STATE)
        return jnp.concatenate([re, im, im, re], axis=-1)

    win = jnp.concatenate([w_in(pr[:ln, 0][::-1], pi[:ln, 0][::-1], 0),
                           w_in(pr[:ln, 1], pi[:ln, 1], 1)], axis=-1)

    def w_out(cr_d, ci_d):
        re = cr_d.transpose(1, 3, 0, 2).reshape(g, S5_STATE, ln * S5_GROUP)
        im = (-ci_d).transpose(1, 3, 0, 2).reshape(g, S5_STATE, ln * S5_GROUP)
        return jnp.concatenate([re, im], axis=1)

    wout = jnp.concatenate([w_out(cr[1:ln + 1, 0], ci[1:ln + 1, 0]),
                            w_out(cr[1:ln + 1, 1][::-1], ci[1:ln + 1, 1][::-1])], axis=1)

    def cpow(k):
        e = (jnp.asarray(k, F32) * ln)[..., None, None, None]
        mg = jnp.exp(a_re * dt * e)
        return mg * jnp.cos(a_im * dt * e), mg * jnp.sin(a_im * dt * e)

    def dup(a, b):
        return jnp.concatenate([a, b], axis=-1)

    mr, mi = cpow(jnp.array([1, 2, 4, 8]))
    mult = jnp.stack([dup(mr, mr), dup(-mi, mi), dup(mi, -mi)], axis=1).reshape(12, 2, g, 2 * S5_STATE)
    qr, qi = cpow(jnp.arange(8))
    qr = jnp.stack([qr[:, 0], qr[::-1, 1]], axis=1)
    qi = jnp.stack([qi[:, 0], qi[::-1, 1]], axis=1)
    lam = jnp.concatenate([mult, jnp.zeros((4, 2, g, 2 * S5_STATE), F32), dup(qr, qr), dup(-qi, qi)], axis=0)
    lam = lam.transpose(2, 1, 0, 3).reshape(g, 64, 2 * S5_STATE)
    return win.astype(BF16), tsum.astype(BF16), wout.astype(BF16), lam


def _s5_in_kernel(u_ref, w_ref, o_ref):
    o_ref[...] = jnp.dot(u_ref[...], w_ref[...], preferred_element_type=F32)


def _s5_in_proj(u, w_s5):
    b, t, d = u.shape
    tm = _big_row_tile(t)
    return pl.pallas_call(
        _s5_in_kernel,
        out_shape=jax.ShapeDtypeStruct((b, t, S5_W), F32),
        grid=(b, t // tm),
        in_specs=[pl.BlockSpec((None, tm, d), lambda bi, i: (bi, i, 0)),
                  pl.BlockSpec((d, S5_W), lambda bi, i: (0, 0))],
        out_specs=pl.BlockSpec((None, tm, S5_W), lambda bi, i: (bi, i, 0)),
        compiler_params=_params("parallel", "arbitrary"),
        name="s5_in_proj",
    )(u, w_s5)


def _s5(u5, s5w, layer, n_ctx):
    b, t, _ = u5.shape
    win, tsum, wout, lam = s5w
    gpb = LANES // S5_GROUP
    nc = t // S5_CHUNK
    kw = S5_CHUNK * S5_GROUP
    assert (n_ctx // S5_CHUNK) % 8 == 0 and nc % 8 == 0
    kern = functools.partial(_s5_kernel, nc=nc, ncc=n_ctx // S5_CHUNK)
    wspec = lambda r, c: pl.BlockSpec((None, gpb, r, c), lambda bi, gb: (layer, gb, 0, 0))
    return pl.pallas_call(
        kern,
        out_shape=jax.ShapeDtypeStruct((b, t, S5_W), F32),
        grid=(b, S5_W // LANES),
        in_specs=[pl.BlockSpec((None, t, LANES), lambda bi, gb: (bi, 0, gb)),
                  wspec(kw, 8 * S5_STATE), wspec(kw, kw), wspec(4 * S5_STATE, kw), wspec(64, 2 * S5_STATE)],
        out_specs=pl.BlockSpec((None, t, LANES), lambda bi, gb: (bi, 0, gb)),
        scratch_shapes=[pltpu.VMEM((gpb, nc, kw), BF16),
                        pltpu.VMEM((gpb, nc, 8 * S5_STATE), F32),
                        pltpu.VMEM((gpb, nc, 4 * S5_STATE), F32),
                        pltpu.VMEM((gpb, nc // 2, kw), jnp.uint32)],
        compiler_params=_params("parallel", "parallel"),
        name="s5_scan",
    )(u5, win, tsum, wout, lam)


def _glu_kernel(s_ref, u_ref, wg_ref, w_ref, b_ref, o_ref):
    s = jax.nn.gelu(s_ref[...])
    z = jnp.dot(s.astype(BF16), w_ref[...], preferred_element_type=F32) + b_ref[...]
    g = jnp.dot(u_ref[...], wg_ref[...], preferred_element_type=F32)
    o_ref[...] = (s * jax.nn.sigmoid(z) * (g * jax.nn.sigmoid(g))).astype(BF16)


def _s5_glu_gate(s_pre, u, w_s5, glu_w, glu_b):
    b, t, d = u.shape
    tm = _big_row_tile(t)
    return pl.pallas_call(
        _glu_kernel,
        out_shape=jax.ShapeDtypeStruct((b, t, S5_W), BF16),
        grid=(b, t // tm),
        in_specs=[pl.BlockSpec((None, tm, S5_W), lambda bi, i: (bi, i, 0)),
                  pl.BlockSpec((None, tm, d), lambda bi, i: (bi, i, 0)),
                  pl.BlockSpec((d, S5_W), lambda bi, i: (0, 1)),
                  pl.BlockSpec((S5_W, S5_W), lambda bi, i: (0, 0)),
                  pl.BlockSpec((1, S5_W), lambda bi, i: (0, 0))],
        out_specs=pl.BlockSpec((None, tm, S5_W), lambda bi, i: (bi, i, 0)),
        compiler_params=_params("parallel", "arbitrary"),
        name="s5_glu_gate",
    )(s_pre, u, w_s5, glu_w, glu_b)


def _gate_kernel(u_ref, wr_ref, wa0_ref, wa1_ref, r_ref, a_ref, gn_ref, o_ref):
    u = u_ref[...]

    def gated(w, act, cols):
        g = jnp.dot(u, w, preferred_element_type=F32)
        o_ref[:, cols] = (act * (g * jax.nn.sigmoid(g))).astype(BF16)

    def group_norm(x):
        mu = jnp.mean(x, axis=-1, keepdims=True)
        xc = x - mu
        var = jnp.mean(xc * xc, axis=-1, keepdims=True)
        return xc * lax.rsqrt(var + LN_EPS)

    for tile in range(RET_W // COL_TILE):
        cols = slice(tile * COL_TILE, (tile + 1) * COL_TILE)
        heads = [group_norm(r_ref[:, c0:c0 + RET_DV].astype(F32))
                 for c0 in range(cols.start, cols.stop, RET_DV)]
        gated(wr_ref[:, cols], jnp.concatenate(heads, axis=1) * gn_ref[:, cols], cols)
    for tile, w_ref in enumerate((wa0_ref, wa1_ref)):
        cols = slice(tile * COL_TILE, (tile + 1) * COL_TILE)
        gated(w_ref[...], a_ref[:, cols].astype(F32), slice(RET_W + cols.start, RET_W + cols.stop))


def _branch_gates(u, wb, layer, r, a, gn_w):
    b, t, d = u.shape
    tm = _resident_row_tile(t)
    assert ATT_W == 2 * COL_TILE
    ret_g_blk = 3
    att_g0 = (4 * RET_W + ATT_W + 2 * ATT_KV_W) // COL_TILE
    resident = lambda width, blk: pl.BlockSpec((None, d, width), lambda bi, i: (layer, 0, blk),
                                               pipeline_mode=pl.Buffered(1))
    return pl.pallas_call(
        _gate_kernel,
        out_shape=jax.ShapeDtypeStruct((b, t, RET_W + ATT_W), BF16),
        grid=(b, t // tm),
        in_specs=[pl.BlockSpec((None, tm, d), lambda bi, i: (bi, i, 0)),
                  resident(RET_W, ret_g_blk), resident(COL_TILE, att_g0), resident(COL_TILE, att_g0 + 1),
                  pl.BlockSpec((None, tm, RET_W), lambda bi, i: (bi, i, 0)),
                  pl.BlockSpec((None, tm, ATT_W), lambda bi, i: (bi, i, 0)),
                  pl.BlockSpec((1, RET_W), lambda bi, i: (0, 0))],
        out_specs=pl.BlockSpec((None, tm, RET_W + ATT_W), lambda bi, i: (bi, i, 0)),
        compiler_params=_params("parallel", "arbitrary"),
        name="branch_gates",
    )(u, wb, wb, wb, r, a, gn_w)


def _merge_kernel(u_ref, zra_ref, zs_ref, wm0_ref, wm1_ref, wm2_ref, wr_ref, wa_ref, ws_ref, o_ref):
    tm = u_ref.shape[0]
    nsplit = 2 if tm % 32 == 0 else 1
    hr = tm // nsplit
    for r in range(nsplit):
        rows = slice(r * hr, (r + 1) * hr)
        u = u_ref[rows, :]
        acts = (zra_ref[rows, :RET_W], zra_ref[rows, RET_W:], zs_ref[rows, :])
        acc = None
        for act, wm_ref, wbr_ref in zip(acts, (wm0_ref, wm1_ref, wm2_ref), (wr_ref, wa_ref, ws_ref)):
            gate = jax.nn.sigmoid(jnp.dot(u, wm_ref[...], preferred_element_type=F32))
            proj = jnp.dot(act, wbr_ref[...], preferred_element_type=F32)
            acc = gate * proj if acc is None else acc + gate * proj
        o_ref[rows, :] = acc.astype(BF16)


def _merge(u, z_ra, z_s, wb, layer, w_ret, w_att, w_s5):
    b, t, d = u.shape
    tm = _big_row_tile(t)
    m0 = (wb.shape[2] - N_BRANCH * d) // COL_TILE
    per = d // COL_TILE
    wm = lambda br: pl.BlockSpec((None, d, COL_TILE), lambda bi, i, j: (layer, 0, m0 + br * per + j))
    wbr = lambda k: pl.BlockSpec((k, COL_TILE), lambda bi, i, j: (0, j))
    return pl.pallas_call(
        _merge_kernel,
        out_shape=jax.ShapeDtypeStruct((b, t, d), BF16),
        grid=(b, t // tm, d // COL_TILE),
        in_specs=[pl.BlockSpec((None, tm, d), lambda bi, i, j: (bi, i, 0)),
                  pl.BlockSpec((None, tm, RET_W + ATT_W), lambda bi, i, j: (bi, i, 0)),
                  pl.BlockSpec((None, tm, S5_W), lambda bi, i, j: (bi, i, 0)),
                  wm(0), wm(1), wm(2), wbr(RET_W), wbr(ATT_W), wbr(S5_W)],
        out_specs=pl.BlockSpec((None, tm, COL_TILE), lambda bi, i, j: (bi, i, j)),
        compiler_params=_params("parallel", "parallel", "arbitrary"),
        name="gated_merge",
    )(u, z_ra, z_s, wb, wb, wb, w_ret, w_att, w_s5)


def _out_kernel(*refs, alpha, first_tile, emit_u):
    if emit_u:
        m_ref, w_ref, c_ref, x_ref, mod_ref, lnw_ref, lnb_ref, nmod_ref, o_ref, u_ref, v_ref = refs
    else:
        m_ref, w_ref, c_ref, x_ref, mod_ref, lnw_ref, lnb_ref, o_ref, v_ref = refs
    i = pl.program_id(1)
    d = w_ref.shape[1]

    def run(res_ref):
        m = m_ref[...]
        s1 = jnp.zeros((m.shape[0], 1), F32)
        for c in range(d // COL_TILE):
            cols = slice(c * COL_TILE, (c + 1) * COL_TILE)
            y = jnp.dot(m, w_ref[:, cols], preferred_element_type=F32)
            v = alpha * res_ref[:, cols] + mod_ref[2:3, cols] * y
            v_ref[:, cols] = v
            s1 += jnp.sum(v, axis=-1, keepdims=True)
        mu = s1 * (1.0 / d)
        vc = v_ref[...] - mu
        var = jnp.mean(vc * vc, axis=-1, keepdims=True)
        out = vc * lax.rsqrt(var + LN_EPS) * lnw_ref[...] + lnb_ref[...]
        o_ref[...] = out
        if emit_u:
            u_ref[...] = (out * (1.0 + nmod_ref[1:2, :]) + nmod_ref[0:1, :]).astype(BF16)

    if first_tile == 0:
        @pl.when(i == 0)
        def _():
            run(c_ref)

        @pl.when(i > 0)
        def _():
            run(x_ref)
    else:
        run(x_ref)


def _out_proj(m, w_out, ctx_src, lat_src, modsel, ln_w, ln_b, alpha, skip_ctx, next_modsel=None):
    b, t, d = m.shape
    off = 1 if skip_ctx else 0
    nt = t // ROW_TILE - off
    emit_u = next_modsel is not None
    kern = functools.partial(_out_kernel, alpha=alpha, first_tile=off, emit_u=emit_u)
    vec = lambda: pl.BlockSpec((1, d), lambda bi, i: (0, 0))
    mod = lambda: pl.BlockSpec((None, None, 3, d), lambda bi, i: (bi, jnp.minimum(i + off, 1), 0, 0))
    row_out = lambda: pl.BlockSpec((None, ROW_TILE, d), lambda bi, i: (bi, i, 0))
    x_shape = jax.ShapeDtypeStruct((b, nt * ROW_TILE, d), F32)
    return pl.pallas_call(
        kern,
        out_shape=(x_shape, jax.ShapeDtypeStruct(x_shape.shape, BF16)) if emit_u else x_shape,
        grid=(b, nt),
        in_specs=[pl.BlockSpec((None, ROW_TILE, d), lambda bi, i: (bi, i + off, 0)),
                  pl.BlockSpec((d, d), lambda bi, i: (0, 0)),
                  *_token_specs(ctx_src, lat_src, off),
                  mod(), vec(), vec(), *([mod()] if emit_u else [])],
        out_specs=(row_out(), row_out()) if emit_u else row_out(),
        scratch_shapes=[pltpu.VMEM((ROW_TILE, d), F32)],
        compiler_params=_params("parallel", "arbitrary"),
        name="out_proj_norm",
    )(m, w_out, ctx_src, lat_src, modsel, ln_w.reshape(1, d), ln_b.reshape(1, d),
      *([next_modsel] if emit_u else []))


def _rope_tables(n, n_ctx):
    rows = n // GRID_W
    row = jnp.repeat(jnp.arange(rows, dtype=F32), GRID_W)
    col = jnp.tile(jnp.arange(GRID_W, dtype=F32), rows)

    def table(head_dim):
        per_axis = head_dim // 4
        inv = ROPE_THETA ** (-jnp.arange(per_axis, dtype=F32) / per_axis)
        ang = jnp.concatenate([row[:, None] * inv, col[:, None] * inv], axis=-1)
        cos = jnp.concatenate([jnp.ones((n_ctx, head_dim // 2), F32), jnp.cos(ang)], axis=0)
        sin = jnp.concatenate([jnp.zeros((n_ctx, head_dim // 2), F32), jnp.sin(ang)], axis=0)
        return cos, sin

    cos_r, sin_r = table(RET_DK)
    cos_a, sin_a = table(ATT_HD)
    return (cos_r, sin_r, jnp.concatenate([cos_a, cos_a], -1), jnp.concatenate([-sin_a, sin_a], -1))


def kernel(x, c, ctx, c_ctx, ada_w, ada_b, w_in, ret_log_decay, ret_gn_w, att_q_norm, att_k_norm, s5_a_re, s5_a_im, s5_log_dt, s5_b_re, s5_b_im, s5_c_re, s5_c_im, s5_d, s5_glu_w, s5_glu_b, w_br_ret, w_br_att, w_br_s5, w_out, ln_w, ln_b):
    b, n, d = x.shape
    n_ctx = ctx.shape[1]
    depth = w_in.shape[0]
    assert n_ctx == ROW_TILE and n % ROW_TILE == 0 and n % GRID_W == 0
    assert w_in.shape[2] == 4 * RET_W + 2 * ATT_W + 2 * ATT_KV_W + 2 * S5_W + N_BRANCH * d
    alpha = (2.0 * depth) ** 0.25
    s5_col0 = 4 * RET_W + 2 * ATT_W + 2 * ATT_KV_W

    t = n_ctx + n
    ctx_src, lat_src = ctx, x
    cos_r, sin_r, cos_a, sin_a = _rope_tables(n, n_ctx)
    crows = 16
    cvec = jnp.concatenate([c, c_ctx[None, :], jnp.zeros((crows - b - 1, d), F32)], axis=0)
    wb = w_in.astype(BF16)
    s5w = jax.vmap(_s5_weights)(s5_a_re, s5_a_im, s5_log_dt, s5_b_re, s5_b_im, s5_c_re, s5_c_im, s5_d)

    modsels = []
    for l in range(depth):
        mod = _modulation(cvec, ada_w, ada_b, l).reshape(crows, 3, d)
        modsels.append(jnp.stack([jnp.broadcast_to(mod[b][None], (b, 3, d)), mod[:b]], axis=1))

    u = _modulate(ctx_src, lat_src, t, modsels[0])
    for l in range(depth):
        last = l == depth - 1
        modsel = modsels[l]
        w_s5 = wb[l, :, s5_col0:s5_col0 + 2 * S5_W]
        mix = _mix_proj(u, wb, l, cos_r, sin_r, cos_a, sin_a,
                        att_q_norm[l].reshape(1, ATT_HD), att_k_norm[l].reshape(1, ATT_HD))
        r = _retention(mix, ret_log_decay[l], n_ctx)
        a = _attention(mix, n_ctx)
        s_pre = _s5(_s5_in_proj(u, w_s5), s5w, l, n_ctx)
        z_s = _s5_glu_gate(s_pre, u, w_s5, s5_glu_w[l].astype(BF16), s5_glu_b[l].reshape(1, S5_W))
        z_ra = _branch_gates(u, wb, l, r, a, ret_gn_w[l].reshape(1, RET_W))
        m = _merge(u, z_ra, z_s, wb, l, w_br_ret[l].astype(BF16), w_br_att[l].astype(BF16),
                   w_br_s5[l].astype(BF16))
        res = _out_proj(m, w_out[l].astype(BF16), ctx_src, lat_src, modsel, ln_w[l], ln_b[l], alpha, last,
                        None if last else modsels[l + 1])
        if last:
            return res
        xa, u = res
        ctx_src = lat_src = xa
```
